```python
import math
import jax, jax.numpy as jnp
from jax import lax
import numpy as np

D_MODEL = 1024
BATCH = 16
SEQ = 2048
DEPTH = 1
DEC_BATCH = 4
DEC_SEQ = 4096
PAST_LEN = 128

HEAD_DIM = 64
N_HEADS_A = 8
N_KV_A = 2
N_HEADS_B = 8
N_KV_B = 2
G_A = N_HEADS_A // N_KV_A
G_B = N_HEADS_B // N_KV_B
WIDTH_A = N_HEADS_A * HEAD_DIM
WIDTH_B = N_HEADS_B * HEAD_DIM
KVW_A = N_KV_A * HEAD_DIM
KVW_B = N_KV_B * HEAD_DIM
IN_WIDTH = WIDTH_A + 2 * KVW_A + WIDTH_B + 2 * KVW_B
BLOCK = 128
WINDOW = 128
GRID_W = 64
ROPE_THETA = 10000.0
ROPE_HALF = HEAD_DIM // 2
N_BUCKETS = 32
MAX_DISTANCE = 128
D_FF = int(math.ceil((8 * D_MODEL / 3) / 256) * 256)
EPS = 1e-6
NEG_INF = -1e30

kernel_name = "hybrid_gqa_window_sink_encoder"


def rmsnorm(x, gain):
    xf = x.astype(jnp.float32)
    var = jnp.mean(xf * xf, axis=-1, keepdims=True)
    return (xf * lax.rsqrt(var + EPS) * gain.astype(jnp.float32)).astype(x.dtype)


def rotate_half(x):
    x1, x2 = jnp.split(x, 2, axis=-1)
    return jnp.concatenate([-x2, x1], axis=-1)


def axial_rope_tables(S):
    ROWS = S // GRID_W
    pos = jnp.arange(S)
    rows = jnp.repeat(jnp.arange(ROWS), GRID_W).astype(jnp.float32)
    cols = (pos % GRID_W).astype(jnp.float32)
    inv_freq = 1.0 / (ROPE_THETA ** (jnp.arange(0, ROPE_HALF, 2, dtype=jnp.float32) / ROPE_HALF))
    fr = rows[:, None] * inv_freq[None, :]
    fc = cols[:, None] * inv_freq[None, :]
    er = jnp.concatenate([fr, fr], axis=-1)
    ec = jnp.concatenate([fc, fc], axis=-1)
    return jnp.cos(er), jnp.sin(er), jnp.cos(ec), jnp.sin(ec)


def apply_axial_rope(x, tables):
    cr, sr, cc, sc = tables
    xr, xc = x[..., :ROPE_HALF], x[..., ROPE_HALF:]
    dt = x.dtype
    cr, sr, cc, sc = (t[None, :, None, :].astype(dt) for t in (cr, sr, cc, sc))
    yr = xr * cr + rotate_half(xr) * sr
    yc = xc * cc + rotate_half(xc) * sc
    return jnp.concatenate([yr, yc], axis=-1)


def t5_bucket(rel):
    half = N_BUCKETS // 2
    max_exact = half // 2
    ret = jnp.where(rel > 0, half, 0)
    n = jnp.abs(rel)
    nf = jnp.maximum(n, 1).astype(jnp.float32)
    large = max_exact + (jnp.log(nf / max_exact) / math.log(MAX_DISTANCE / max_exact) * (half - max_exact)).astype(jnp.int32)
    large = jnp.minimum(large, half - 1)
    return ret + jnp.where(n < max_exact, n, large)


def global_attention(q, k, v):
    B, S, KV, G, D = q.shape
    nb = S // BLOCK
    scale = 1.0 / math.sqrt(D)
    qb = q.reshape(B, nb, BLOCK, KV, G, D).transpose(1, 0, 2, 3, 4, 5)
    kf = k.astype(jnp.float32)
    vf = v.astype(jnp.float32)

    def one_block(qblk):
        s = jnp.einsum('bqkgd,bskd->bkgqs', qblk.astype(jnp.float32), kf) * scale
        p = jax.nn.softmax(s, axis=-1)
        o = jnp.einsum('bkgqs,bskd->bqkgd', p, vf)
        return o.astype(q.dtype)

    out = lax.map(one_block, qb)
    return out.transpose(1, 0, 2, 3, 4, 5).reshape(B, S, KV * G * D)


def window_attention(q, k, v, sink, rel_bias):
    B, S, KV, G, D = q.shape
    nb = S // BLOCK
    scale = 1.0 / math.sqrt(D)
    qb = q.reshape(B, nb, BLOCK, KV, G, D)
    pad = ((0, 0), (BLOCK, BLOCK), (0, 0), (0, 0))
    kp = jnp.pad(k, pad).reshape(B, nb + 2, BLOCK, KV, D)
    vp = jnp.pad(v, pad).reshape(B, nb + 2, BLOCK, KV, D)
    kw = jnp.concatenate([kp[:, :-2], kp[:, 1:-1], kp[:, 2:]], axis=2)
    vw = jnp.concatenate([vp[:, :-2], vp[:, 1:-1], vp[:, 2:]], axis=2)
    s = jnp.einsum('bnqkgd,bnckd->bnkgqc', qb.astype(jnp.float32), kw.astype(jnp.float32)) * scale
    a = jnp.arange(BLOCK)[:, None]
    c = jnp.arange(3 * BLOCK)[None, :]
    rel = c - BLOCK - a
    qpos = jnp.arange(nb)[:, None, None] * BLOCK + a[None]
    kpos = qpos + rel[None]
    valid = (jnp.abs(rel)[None] <= WINDOW) & (kpos >= 0) & (kpos < S)
    bias = rel_bias.astype(jnp.float32)[t5_bucket(rel)]
    bias = bias.transpose(2, 0, 1).reshape(KV, G, BLOCK, 3 * BLOCK)
    s = s + bias[None, None]
    s = jnp.where(valid[None, :, None, None], s, NEG_INF)
    sk = sink.astype(jnp.float32).reshape(1, 1, KV, G, 1, 1)
    m = jnp.maximum(jnp.max(s, axis=-1, keepdims=True), sk)
    p = jnp.exp(s - m)
    denom = jnp.sum(p, axis=-1, keepdims=True) + jnp.exp(sk - m)
    o = jnp.einsum('bnkgqc,bnckd->bnqkgd', p / denom, vw.astype(jnp.float32))
    return o.astype(q.dtype).reshape(B, S, KV * G * D)


def encoder_layer(x, norm_mix_pre, norm_mix_post, w_in, q_norm_a, k_norm_a, sink_b, rel_bias,
                  w_branch_a, w_branch_b, w_gate, b_gate, w_out,
                  norm_ffn_pre, norm_ffn_post, w_ffn_gate, w_ffn_up, w_ffn_down):
    B, S, _ = x.shape
    h = rmsnorm(x, norm_mix_pre)
    proj = h @ w_in
    o1 = WIDTH_A
    o2 = o1 + KVW_A
    o3 = o2 + KVW_A
    o4 = o3 + WIDTH_B
    o5 = o4 + KVW_B
    qa = proj[..., :o1].reshape(B, S, N_HEADS_A, HEAD_DIM)
    ka = proj[..., o1:o2].reshape(B, S, N_KV_A, HEAD_DIM)
    va = proj[..., o2:o3].reshape(B, S, N_KV_A, HEAD_DIM)
    qb = proj[..., o3:o4].reshape(B, S, N_KV_B, G_B, HEAD_DIM)
    kb = proj[..., o4:o5].reshape(B, S, N_KV_B, HEAD_DIM)
    vb = proj[..., o5:].reshape(B, S, N_KV_B, HEAD_DIM)

    tables = axial_rope_tables(S)
    qa = apply_axial_rope(rmsnorm(qa, q_norm_a), tables).reshape(B, S, N_KV_A, G_A, HEAD_DIM)
    ka = apply_axial_rope(rmsnorm(ka, k_norm_a), tables)
    ya = global_attention(qa, ka, va)

    yb = window_attention(qb, kb, vb, sink_b, rel_bias)

    gates = jax.nn.sigmoid((h @ w_gate + b_gate).astype(jnp.float32)).astype(x.dtype)
    mix = gates[..., :D_MODEL] * (ya @ w_branch_a) + gates[..., D_MODEL:] * (yb @ w_branch_b)
    x = x + rmsnorm(mix @ w_out, norm_mix_post)

    h2 = rmsnorm(x, norm_ffn_pre)
    f = (jax.nn.silu(h2 @ w_ffn_gate) * (h2 @ w_ffn_up)) @ w_ffn_down
    return x + rmsnorm(f, norm_ffn_post)


def setup_inputs(seed: int = 0) -> dict:
    key = jax.random.key(seed)
    ks = jax.random.split(key, 24)
    f32 = jnp.float32

    def w(k, shape, fan_in):
        return jax.random.normal(k, shape, f32) * (fan_in ** -0.5)

    def gain(k, shape):
        return 1.0 + 0.05 * jax.random.normal(k, shape, f32)

    return {
        "x_prompt": jax.random.normal(ks[0], (BATCH, SEQ, D_MODEL), f32),
        "x_sample": jax.random.normal(ks[1], (DEC_BATCH, DEC_SEQ, D_MODEL), f32),
        "norm_mix_pre": gain(ks[2], (DEPTH, D_MODEL)),
        "norm_mix_post": gain(ks[3], (DEPTH, D_MODEL)),
        "w_in": w(ks[4], (DEPTH, D_MODEL, IN_WIDTH), D_MODEL),
        "q_norm_a": gain(ks[5], (DEPTH, HEAD_DIM)),
        "k_norm_a": gain(ks[6], (DEPTH, HEAD_DIM)),
        "sink_b": 0.5 * jax.random.normal(ks[7], (DEPTH, N_HEADS_B), f32),
        "rel_bias": 0.1 * jax.random.normal(ks[8], (N_BUCKETS, N_HEADS_B), f32),
        "w_branch_a": w(ks[9], (DEPTH, WIDTH_A, D_MODEL), WIDTH_A),
        "w_branch_b": w(ks[10], (DEPTH, WIDTH_B, D_MODEL), WIDTH_B),
        "w_gate": w(ks[11], (DEPTH, D_MODEL, 2 * D_MODEL), D_MODEL),
        "b_gate": 0.02 * jax.random.normal(ks[12], (DEPTH, 2 * D_MODEL), f32),
        "w_out": w(ks[13], (DEPTH, D_MODEL, D_MODEL), D_MODEL),
        "norm_ffn_pre": gain(ks[14], (DEPTH, D_MODEL)),
        "norm_ffn_post": gain(ks[15], (DEPTH, D_MODEL)),
        "w_ffn_gate": w(ks[16], (DEPTH, D_MODEL, D_FF), D_MODEL),
        "w_ffn_up": w(ks[17], (DEPTH, D_MODEL, D_FF), D_MODEL),
        "w_ffn_down": w(ks[18], (DEPTH, D_FF, D_MODEL), D_FF),
    }


def reference(x_prompt, x_sample, norm_mix_pre, norm_mix_post, w_in, q_norm_a, k_norm_a, sink_b,
              rel_bias, w_branch_a, w_branch_b, w_gate, b_gate, w_out,
              norm_ffn_pre, norm_ffn_post, w_ffn_gate, w_ffn_up, w_ffn_down):
    y_prompt = x_prompt
    y_sample = x_sample
    for l in range(DEPTH):
        layer_args = (norm_mix_pre[l], norm_mix_post[l], w_in[l], q_norm_a[l], k_norm_a[l], sink_b[l],
                      rel_bias, w_branch_a[l], w_branch_b[l], w_gate[l], b_gate[l], w_out[l],
                      norm_ffn_pre[l], norm_ffn_post[l], w_ffn_gate[l], w_ffn_up[l], w_ffn_down[l])
        y_prompt = encoder_layer(y_prompt, *layer_args)
        y_sample = encoder_layer(y_sample, *layer_args)
    return (y_prompt, y_sample)
```

```python
import functools
import math

import numpy as np
import jax
import jax.numpy as jnp
from jax import lax
from jax.experimental import pallas as pl
from jax.experimental.pallas import tpu as pltpu

F32 = jnp.float32
BF16 = jnp.bfloat16

D_MODEL = 1024
HEAD_DIM = 64
N_HEADS = 8
N_KV = 2
GROUP = N_HEADS // N_KV
Q_WIDTH = N_HEADS * HEAD_DIM
KV_WIDTH = N_KV * HEAD_DIM
IN_WIDTH = 2 * (Q_WIDTH + 2 * KV_WIDTH)
BLOCK = 128
WINDOW = 128
GRID_W = 64
ROPE_THETA = 10000.0
ROPE_HALF = HEAD_DIM // 2
ROPE_QUARTER = ROPE_HALF // 2
N_BUCKETS = 32
MAX_DISTANCE = 128
D_FF = 2816
EPS = 1e-6
NEG_INF = -1e30
LOG2E = 1.4426950408889634
Q_SCALE = LOG2E / math.sqrt(HEAD_DIM)

LANES = 128
N_PAIRS = N_HEADS // 2
PAIRS_PER_KV = N_PAIRS // N_KV

SLAB_WIDTH = 2 * (Q_WIDTH + 4 * LANES)
COL_QA, COL_KA, COL_VA, COL_QB, COL_KB, COL_VB = 0, 4, 6, 8, 12, 14

TM_PRE = 512
TM_MIX = 512
TM_FFN = 512
TQ_GLOBAL = 256
TQ_WINDOW = 512
VMEM_LIMIT = 56 * 1024 * 1024


def _rmsnorm(x, gain):
    var = jnp.mean(x * x, axis=-1, keepdims=True)
    return x * lax.rsqrt(var + EPS) * gain


def _const_spec(shape):
    zeros = (0,) * len(shape)
    return pl.BlockSpec(shape, lambda *_: zeros, pipeline_mode=pl.Buffered(1))


def _pre_kernel(x_ref, g_ref, w_ref, qkg_ref, rope_ref, bd_ref, o_ref):
    x = x_ref[...]
    h = _rmsnorm(x, g_ref[...]).astype(BF16)
    proj = jnp.dot(h, w_ref[...], preferred_element_type=F32)
    bd = bd_ref[...]
    cos, sin_up, sin_dn = rope_ref[0], rope_ref[1], rope_ref[2]
    lane = lax.broadcasted_iota(jnp.int32, (x.shape[0], LANES), 1)
    lo_half = lane < HEAD_DIM

    def qk_norm_rope(z, gain):
        sq = z * z
        hi = sq.astype(BF16)
        lo = (sq - hi.astype(F32)).astype(BF16)
        ssq = (jnp.dot(hi, bd, preferred_element_type=F32)
               + jnp.dot(lo, bd, preferred_element_type=F32))
        zn = z * lax.rsqrt(ssq * (1.0 / HEAD_DIM) + EPS) * gain
        return (zn * cos
                + pltpu.roll(zn, LANES - ROPE_QUARTER, 1) * sin_up
                + pltpu.roll(zn, ROPE_QUARTER, 1) * sin_dn)

    def dup_heads(z):
        swapped = pltpu.roll(z, HEAD_DIM, 1)
        return jnp.where(lo_half, z, swapped), jnp.where(lo_half, swapped, z)

    def put(col, val):
        o_ref[:, col * LANES:(col + 1) * LANES] = val.astype(BF16)

    for s in range(4):
        q = qk_norm_rope(proj[:, s * LANES:(s + 1) * LANES], qkg_ref[0])
        put(COL_QA + s, q * Q_SCALE)
    k0, k1 = dup_heads(qk_norm_rope(proj[:, 512:640], qkg_ref[1]))
    put(COL_KA, k0)
    put(COL_KA + 1, k1)
    v0, v1 = dup_heads(proj[:, 640:768])
    put(COL_VA, v0)
    put(COL_VA + 1, v1)
    for s in range(4):
        put(COL_QB + s, proj[:, 768 + s * LANES:768 + (s + 1) * LANES] * Q_SCALE)
    k0, k1 = dup_heads(proj[:, 1280:1408])
    put(COL_KB, k0)
    put(COL_KB + 1, k1)
    v0, v1 = dup_heads(proj[:, 1408:1536])
    put(COL_VB, v0)
    put(COL_VB + 1, v1)


def _pre_call(x2d, seq, g_pre, w_in, qk_gain, rope, bd):
    n_tok = x2d.shape[0]
    tm = TM_PRE
    tiles_per_seq = seq // tm
    return pl.pallas_call(
        _pre_kernel,
        grid=(n_tok // tm,),
        in_specs=[
            pl.BlockSpec((tm, D_MODEL), lambda i: (i, 0)),
            _const_spec((1, D_MODEL)),
            _const_spec((D_MODEL, IN_WIDTH)),
            _const_spec((2, 1, LANES)),
            pl.BlockSpec((3, tm, LANES), lambda i: (0, i % tiles_per_seq, 0)),
            _const_spec((LANES, LANES)),
        ],
        out_specs=pl.BlockSpec((tm, SLAB_WIDTH), lambda i: (i, 0)),
        out_shape=jax.ShapeDtypeStruct((n_tok, SLAB_WIDTH), BF16),
        compiler_params=pltpu.CompilerParams(
            dimension_semantics=("arbitrary",), vmem_limit_bytes=VMEM_LIMIT),
        name="pre",
    )(x2d, g_pre, w_in, qk_gain, rope, bd)


def _gattn_kernel(q_ref, k_ref, v_ref, o_ref):
    q = q_ref[0]
    k = k_ref[0]
    v = v_ref[0]
    lane = lax.broadcasted_iota(jnp.int32, q.shape, 1)
    lo_half = lane < HEAD_DIM
    zero = jnp.zeros_like(q)
    outs = []
    for head_is_lo in (True, False):
        qm = jnp.where(lo_half == head_is_lo, q, zero)
        s = lax.dot_general(qm, k, (((1,), (1,)), ((), ())), preferred_element_type=F32)
        m = jnp.max(s, axis=-1, keepdims=True)
        p = jnp.exp2(s - m)
        denom = jnp.sum(p, axis=-1, keepdims=True)
        o = jnp.dot(p.astype(BF16), v, preferred_element_type=F32)
        outs.append(o / denom)
    o_ref[0] = jnp.where(lo_half, outs[0], outs[1]).astype(BF16)


def _gattn_call(slab, batch, seq):
    tq = TQ_GLOBAL
    slab3 = slab.reshape(batch, seq, SLAB_WIDTH)
    return pl.pallas_call(
        _gattn_kernel,
        grid=(batch, N_PAIRS, seq // tq),
        in_specs=[
            pl.BlockSpec((1, tq, LANES), lambda b, p, i: (b, i, COL_QA + p)),
            pl.BlockSpec((1, seq, LANES), lambda b, p, i: (b, 0, COL_KA + p // PAIRS_PER_KV)),
            pl.BlockSpec((1, seq, LANES), lambda b, p, i: (b, 0, COL_VA + p // PAIRS_PER_KV)),
        ],
        out_specs=pl.BlockSpec((1, tq, LANES), lambda b, p, i: (b, i, p)),
        out_shape=jax.ShapeDtypeStruct((batch, seq, Q_WIDTH), BF16),
        compiler_params=pltpu.CompilerParams(
            dimension_semantics=("arbitrary", "arbitrary", "arbitrary"),
            vmem_limit_bytes=VMEM_LIMIT),
        name="gattn",
    )(slab3, slab3, slab3)


def _t5_bucket_table():
    half = N_BUCKETS // 2
    max_exact = half // 2
    a = np.arange(BLOCK)[:, None]
    c = np.arange(3 * BLOCK)[None, :]
    rel = c - BLOCK - a
    n = np.abs(rel)
    assert MAX_DISTANCE // max_exact == 16 and half - max_exact == 8
    large = np.zeros_like(n)
    for kk in range(1, 2 * half):
        large += ((n * n) >= (2 ** kk) * max_exact * max_exact)
    large = np.minimum(max_exact + large, half - 1)
    bucket = np.where(rel > 0, half, 0) + np.where(n < max_exact, n, large)
    return bucket.astype(np.int32)


def _wattn_kernel(seq, q_ref, k_ref, v_ref, bkt_ref, rb_ref, sink_ref, o_ref,
                  kpad, vpad, bias_scr):
    pair = pl.program_id(1)
    i = pl.program_id(2)
    n_blk_step = q_ref.shape[1] // BLOCK
    n_blk_seq = seq // BLOCK

    @pl.when(i == 0)
    def _():
        zpad = jnp.zeros((BLOCK, LANES), BF16)
        kpad[0:BLOCK, :] = zpad
        kpad[BLOCK + seq:2 * BLOCK + seq, :] = zpad
        kpad[BLOCK:BLOCK + seq, :] = k_ref[0]
        vpad[0:BLOCK, :] = zpad
        vpad[BLOCK + seq:2 * BLOCK + seq, :] = zpad
        vpad[BLOCK:BLOCK + seq, :] = v_ref[0]
        bucket = bkt_ref[...]
        row = lax.broadcasted_iota(jnp.int32, bucket.shape, 0)
        col = lax.broadcasted_iota(jnp.int32, bucket.shape, 1)
        band = jnp.abs(col - BLOCK - row) <= WINDOW
        for hh in range(2):
            head = pair * 2 + hh
            bias = jnp.zeros(bucket.shape, F32)
            for b in range(N_BUCKETS):
                bias = jnp.where(bucket == b, rb_ref[b, head], bias)
            bias = bias * LOG2E
            bias_scr[hh, 0] = jnp.where(band & (col >= BLOCK), bias, NEG_INF)
            bias_scr[hh, 1] = jnp.where(band, bias, NEG_INF)
            bias_scr[hh, 2] = jnp.where(band & (col < 2 * BLOCK), bias, NEG_INF)

    lane = lax.broadcasted_iota(jnp.int32, (BLOCK, LANES), 1)
    lo_half = lane < HEAD_DIM
    zero = jnp.zeros((BLOCK, LANES), BF16)
    for j in range(n_blk_step):
        blk = i * n_blk_step + j
        which = jnp.where(blk == 0, 0, jnp.where(blk == n_blk_seq - 1, 2, 1))
        start = pl.multiple_of(blk * BLOCK, BLOCK)
        q = q_ref[0, j * BLOCK:(j + 1) * BLOCK, :]
        kw = kpad[pl.ds(start, 3 * BLOCK), :]
        vw = vpad[pl.ds(start, 3 * BLOCK), :]
        outs = []
        for hh in range(2):
            sink = sink_ref[pair * 2 + hh] * LOG2E
            qm = jnp.where(lo_half == (hh == 0), q, zero)
            s = lax.dot_general(qm, kw, (((1,), (1,)), ((), ())), preferred_element_type=F32)
            s = s + bias_scr[hh, which]
            m = jnp.maximum(jnp.max(s, axis=-1, keepdims=True), sink)
            p = jnp.exp2(s - m)
            denom = jnp.sum(p, axis=-1, keepdims=True) + jnp.exp2(sink - m)
            o = jnp.dot(p.astype(BF16), vw, preferred_element_type=F32)
            outs.append(o / denom)
        o_ref[0, j * BLOCK:(j + 1) * BLOCK, :] = jnp.where(lo_half, outs[0], outs[1]).astype(BF16)


def _wattn_call(slab, batch, seq, bucket, rel_bias, sink):
    tq = TQ_WINDOW
    slab3 = slab.reshape(batch, seq, SLAB_WIDTH)
    return pl.pallas_call(
        functools.partial(_wattn_kernel, seq),
        grid=(batch, N_PAIRS, seq // tq),
        in_specs=[
            pl.BlockSpec((1, tq, LANES), lambda b, p, i: (b, i, COL_QB + p)),
            pl.BlockSpec((1, seq, LANES), lambda b, p, i: (b, 0, COL_KB + p // PAIRS_PER_KV)),
            pl.BlockSpec((1, seq, LANES), lambda b, p, i: (b, 0, COL_VB + p // PAIRS_PER_KV)),
            pl.BlockSpec((BLOCK, 3 * BLOCK), lambda b, p, i: (0, 0)),
            pl.BlockSpec(memory_space=pltpu.SMEM),
            pl.BlockSpec(memory_space=pltpu.SMEM),
        ],
        out_specs=pl.BlockSpec((1, tq, LANES), lambda b, p, i: (b, i, p)),
        out_shape=jax.ShapeDtypeStruct((batch, seq, Q_WIDTH), BF16),
        scratch_shapes=[
            pltpu.VMEM((seq + 2 * BLOCK, LANES), BF16),
            pltpu.VMEM((seq + 2 * BLOCK, LANES), BF16),
            pltpu.VMEM((2, 3, BLOCK, 3 * BLOCK), F32),
        ],
        compiler_params=pltpu.CompilerParams(
            dimension_semantics=("arbitrary", "arbitrary", "arbitrary"),
            vmem_limit_bytes=VMEM_LIMIT),
        name="wattn",
    )(slab3, slab3, slab3, bucket, rel_bias, sink)


def _mix_kernel(x_ref, ya_ref, yb_ref, gpre_ref, wg_ref, bg_ref, wa_ref, wb_ref, wo_ref, gpost_ref, o_ref):
    x = x_ref[...]
    h = _rmsnorm(x, gpre_ref[...]).astype(BF16)
    z = jnp.dot(h, wg_ref[...], preferred_element_type=F32) + bg_ref[...]
    gates = 1.0 / (1.0 + jnp.exp(-z))
    a = jnp.dot(ya_ref[...], wa_ref[...], preferred_element_type=F32)
    b = jnp.dot(yb_ref[...], wb_ref[...], preferred_element_type=F32)
    mix = gates[:, :D_MODEL] * a + gates[:, D_MODEL:] * b
    o = jnp.dot(mix.astype(BF16), wo_ref[...], preferred_element_type=F32)
    o_ref[...] = x + _rmsnorm(o, gpost_ref[...])


def _mix_call(x2d, ya, yb, g_pre, w_gate, b_gate, w_a, w_b, w_out, g_post):
    n_tok = x2d.shape[0]
    tm = TM_MIX
    return pl.pallas_call(
        _mix_kernel,
        grid=(n_tok // tm,),
        in_specs=[
            pl.BlockSpec((tm, D_MODEL), lambda i: (i, 0)),
            pl.BlockSpec((tm, Q_WIDTH), lambda i: (i, 0)),
            pl.BlockSpec((tm, Q_WIDTH), lambda i: (i, 0)),
            _const_spec((1, D_MODEL)),
            _const_spec((D_MODEL, 2 * D_MODEL)),
            _const_spec((1, 2 * D_MODEL)),
            _const_spec((Q_WIDTH, D_MODEL)),
            _const_spec((Q_WIDTH, D_MODEL)),
            _const_spec((D_MODEL, D_MODEL)),
            _const_spec((1, D_MODEL)),
        ],
        out_specs=pl.BlockSpec((tm, D_MODEL), lambda i: (i, 0)),
        out_shape=jax.ShapeDtypeStruct((n_tok, D_MODEL), F32),
        compiler_params=pltpu.CompilerParams(
            dimension_semantics=("arbitrary",), vmem_limit_bytes=VMEM_LIMIT),
        name="mix",
    )(x2d, ya, yb, g_pre, w_gate, b_gate, w_a, w_b, w_out, g_post)


def _ffn_kernel(x_ref, gpre_ref, wg_ref, wu_ref, wd_ref, gpost_ref, o_ref):
    x = x_ref[...]
    h = _rmsnorm(x, gpre_ref[...]).astype(BF16)
    g = jnp.dot(h, wg_ref[...], preferred_element_type=F32)
    u = jnp.dot(h, wu_ref[...], preferred_element_type=F32)
    act = (g / (1.0 + jnp.exp(-g))) * u
    f = jnp.dot(act.astype(BF16), wd_ref[...], preferred_element_type=F32)
    o_ref[...] = x + _rmsnorm(f, gpost_ref[...])


def _ffn_call(x2d, g_pre, w_g, w_u, w_d, g_post):
    n_tok = x2d.shape[0]
    tm = TM_FFN
    return pl.pallas_call(
        _ffn_kernel,
        grid=(n_tok // tm,),
        in_specs=[
            pl.BlockSpec((tm, D_MODEL), lambda i: (i, 0)),
            _const_spec((1, D_MODEL)),
            _const_spec((D_MODEL, D_FF)),
            _const_spec((D_MODEL, D_FF)),
            _const_spec((D_FF, D_MODEL)),
            _const_spec((1, D_MODEL)),
        ],
        out_specs=pl.BlockSpec((tm, D_MODEL), lambda i: (i, 0)),
        out_shape=jax.ShapeDtypeStruct((n_tok, D_MODEL), F32),
        compiler_params=pltpu.CompilerParams(
            dimension_semantics=("arbitrary",), vmem_limit_bytes=VMEM_LIMIT),
        name="ffn",
    )(x2d, g_pre, w_g, w_u, w_d, g_post)


def _rope_tables(seq):
    pos = jnp.arange(seq)
    rows = (pos // GRID_W).astype(F32)
    cols = (pos % GRID_W).astype(F32)
    inv_freq = 1.0 / (ROPE_THETA ** (jnp.arange(0, ROPE_HALF, 2, dtype=F32) / ROPE_HALF))
    lane = np.arange(LANES)
    d = lane % HEAD_DIM
    use_col = (d // ROPE_HALF) == 1
    j = d % ROPE_HALF
    f_idx = j % ROPE_QUARTER
    coord = jnp.where(jnp.asarray(use_col)[None, :], cols[:, None], rows[:, None])
    ang = coord * inv_freq[jnp.asarray(f_idx)][None, :]
    cos, sin = jnp.cos(ang), jnp.sin(ang)
    first = jnp.asarray(j < ROPE_QUARTER)[None, :]
    sin_up = jnp.where(first, -sin, 0.0)
    sin_dn = jnp.where(first, 0.0, sin)
    return jnp.stack([cos, sin_up, sin_dn]).astype(F32)


def _layer(x, bucket, bd, p):
    batch, seq, _ = x.shape
    assert seq % TM_PRE == 0 and seq % TQ_GLOBAL == 0 and seq % TQ_WINDOW == 0 and seq // BLOCK >= 2
    x2d = x.reshape(batch * seq, D_MODEL)
    slab = _pre_call(x2d, seq, p["g_mix_pre"], p["w_in"], p["qk_gain"], _rope_tables(seq), bd)
    ya = _gattn_call(slab, batch, seq).reshape(batch * seq, Q_WIDTH)
    yb = _wattn_call(slab, batch, seq, bucket, p["rel_bias"], p["sink"]).reshape(batch * seq, Q_WIDTH)
    x1 = _mix_call(x2d, ya, yb, p["g_mix_pre"], p["w_gate"], p["b_gate"], p["w_a"], p["w_b"], p["w_out"],
                   p["g_mix_post"])
    y = _ffn_call(x1, p["g_ffn_pre"], p["w_ffn_gate"], p["w_ffn_up"], p["w_ffn_down"], p["g_ffn_post"])
    return y.reshape(batch, seq, D_MODEL)


def kernel(x_prompt, x_sample, norm_mix_pre, norm_mix_post, w_in, q_norm_a, k_norm_a, sink_b, rel_bias,
           w_branch_a, w_branch_b, w_gate, b_gate, w_out, norm_ffn_pre, norm_ffn_post, w_ffn_gate,
           w_ffn_up, w_ffn_down):
    depth = w_in.shape[0]
    bucket = jnp.asarray(_t5_bucket_table())
    blk = np.arange(LANES) // HEAD_DIM
    bd = jnp.asarray(blk[:, None] == blk[None, :], dtype=BF16)
    y_prompt, y_sample = x_prompt, x_sample
    for l in range(depth):
        p = {
            "g_mix_pre": norm_mix_pre[l].reshape(1, D_MODEL),
            "g_mix_post": norm_mix_post[l].reshape(1, D_MODEL),
            "w_in": w_in[l].astype(BF16),
            "qk_gain": jnp.stack([jnp.tile(q_norm_a[l], 2), jnp.tile(k_norm_a[l], 2)]).reshape(2, 1, LANES),
            "sink": sink_b[l],
            "rel_bias": rel_bias,
            "w_a": w_branch_a[l].astype(BF16),
            "w_b": w_branch_b[l].astype(BF16),
            "w_gate": w_gate[l].astype(BF16),
            "b_gate": b_gate[l].reshape(1, 2 * D_MODEL),
            "w_out": w_out[l].astype(BF16),
            "g_ffn_pre": norm_ffn_pre[l].reshape(1, D_MODEL),
            "g_ffn_post": norm_ffn_post[l].reshape(1, D_MODEL),
            "w_ffn_gate": w_ffn_gate[l].astype(BF16),
            "w_ffn_up": w_ffn_up[l].astype(BF16),
            "w_ffn_down": w_ffn_down[l].astype(BF16),
        }
        y_prompt = _layer(y_prompt, bucket, bd, p)
        y_sample = _layer(y_sample, bucket, bd, p)
    return (y_prompt, y_sample)
```

```python
import functools
import math

import numpy as np
import jax
import jax.numpy as jnp
from jax import lax
from jax.experimental import pallas as pl
from jax.experimental.pallas import tpu as pltpu

F32 = jnp.float32
BF16 = jnp.bfloat16

D_MODEL = 1024
HEAD_DIM = 64
N_HEADS = 8
N_KV = 2
Q_WIDTH = N_HEADS * HEAD_DIM
KV_WIDTH = N_KV * HEAD_DIM
IN_WIDTH = 2 * (Q_WIDTH + 2 * KV_WIDTH)
BLOCK = 128
WINDOW = 128
GRID_W = 64
ROPE_THETA = 10000.0
ROPE_HALF = HEAD_DIM // 2
ROPE_QUARTER = ROPE_HALF // 2
N_BUCKETS = 32
MAX_DISTANCE = 128
D_FF = 2816
EPS = 1e-6
NEG_INF = -1e30
LOG2E = 1.4426950408889634
Q_SCALE = LOG2E / math.sqrt(HEAD_DIM)

LANES = 128
SUBLANES = 8
N_PAIRS = N_HEADS // 2
PAIRS_PER_KV = N_PAIRS // N_KV


TM_PRE = 512
TM_MIX = 512
TM_FFN = 512
TQ_GLOBAL = 512
TK_GLOBAL = 512
WQ = 256
WK = WQ + 2 * WINDOW
TQ_WINDOW = 512
VMEM_LIMIT = 56 * 1024 * 1024

NT = (((1,), (1,)), ((), ()))


def _rmsnorm(x, gain):
    var = jnp.mean(x * x, axis=-1, keepdims=True)
    return x * lax.rsqrt(var + EPS) * gain


def _const_spec(shape):
    zeros = (0,) * len(shape)
    return pl.BlockSpec(shape, lambda *_: zeros, pipeline_mode=pl.Buffered(1))


def _pre_kernel(x_ref, g_ref, w_ref, qkg_ref, rope_ref, bd_ref,
                qa_ref, ka_ref, vat_ref, qb_ref, kb_ref, vbt_ref):
    x = x_ref[...]
    tm = x.shape[0]
    h = _rmsnorm(x, g_ref[...]).astype(BF16)
    proj = jnp.dot(h, w_ref[...], preferred_element_type=F32)
    bd = bd_ref[...]
    cos, sin_up, sin_dn = rope_ref[0], rope_ref[1], rope_ref[2]
    lane = lax.broadcasted_iota(jnp.int32, (tm, LANES), 1)
    lo_half = lane < HEAD_DIM

    def qk_norm_rope(z, gain):
        sq = z * z
        hi = sq.astype(BF16)
        lo = (sq - hi.astype(F32)).astype(BF16)
        ssq = (jnp.dot(hi, bd, preferred_element_type=F32)
               + jnp.dot(lo, bd, preferred_element_type=F32))
        zn = z * lax.rsqrt(ssq * (1.0 / HEAD_DIM) + EPS) * gain
        return (zn * cos
                + pltpu.roll(zn, LANES - ROPE_QUARTER, 1) * sin_up
                + pltpu.roll(zn, ROPE_QUARTER, 1) * sin_dn)

    def put_k(ref, z):
        swapped = pltpu.roll(z, HEAD_DIM, 1)
        ref[0, 0] = jnp.where(lo_half, z, 0.0).astype(BF16)
        ref[0, 1] = jnp.where(lo_half, 0.0, swapped).astype(BF16)
        ref[1, 0] = jnp.where(lo_half, swapped, 0.0).astype(BF16)
        ref[1, 1] = jnp.where(lo_half, 0.0, z).astype(BF16)

    for s in range(N_PAIRS):
        q = qk_norm_rope(proj[:, s * LANES:(s + 1) * LANES], qkg_ref[0])
        qa_ref[s] = (q * Q_SCALE).astype(BF16)
    put_k(ka_ref, qk_norm_rope(proj[:, 512:640], qkg_ref[1]))
    vat_ref[...] = proj[:, 640:768].T.astype(BF16).reshape(N_KV, HEAD_DIM, tm)
    for s in range(N_PAIRS):
        qb_ref[s] = (proj[:, 768 + s * LANES:768 + (s + 1) * LANES] * Q_SCALE).astype(BF16)
    put_k(kb_ref, proj[:, 1280:1408])
    vbt = proj[:, 1408:1536].T.astype(BF16)
    for j in range(tm // BLOCK):
        vbt_ref[j] = vbt[:, j * BLOCK:(j + 1) * BLOCK]


def _pre_call(x2d, seq, g_pre, w_in, qk_gain, rope, bd):
    n_tok = x2d.shape[0]
    tm = TM_PRE
    tiles_per_seq = seq // tm
    q_spec = pl.BlockSpec((N_PAIRS, tm, LANES), lambda i: (0, i, 0))
    k_spec = pl.BlockSpec((N_KV, 2, tm, LANES), lambda i: (0, 0, i, 0))
    q_shape = jax.ShapeDtypeStruct((N_PAIRS, n_tok, LANES), BF16)
    k_shape = jax.ShapeDtypeStruct((N_KV, 2, n_tok, LANES), BF16)
    return pl.pallas_call(
        _pre_kernel,
        grid=(n_tok // tm,),
        in_specs=[
            pl.BlockSpec((tm, D_MODEL), lambda i: (i, 0)),
            _const_spec((1, D_MODEL)),
            _const_spec((D_MODEL, IN_WIDTH)),
            _const_spec((2, 1, LANES)),
            pl.BlockSpec((3, tm, LANES), lambda i: (0, i % tiles_per_seq, 0)),
            _const_spec((LANES, LANES)),
        ],
        out_specs=[
            q_spec, k_spec,
            pl.BlockSpec((N_KV, HEAD_DIM, tm), lambda i: (0, 0, i)),
            q_spec, k_spec,
            pl.BlockSpec((tm // BLOCK, KV_WIDTH, BLOCK), lambda i: (i, 0, 0)),
        ],
        out_shape=[
            q_shape, k_shape,
            jax.ShapeDtypeStruct((N_KV, HEAD_DIM, n_tok), BF16),
            q_shape, k_shape,
            jax.ShapeDtypeStruct((n_tok // BLOCK, KV_WIDTH, BLOCK), BF16),
        ],
        compiler_params=pltpu.CompilerParams(
            dimension_semantics=("arbitrary",), vmem_limit_bytes=VMEM_LIMIT),
        name="pre",
    )(x2d, g_pre, w_in, qk_gain, rope, bd)


def _fold8(x, op):
    return op(x.reshape(x.shape[0] // SUBLANES, SUBLANES, x.shape[1]), axis=0)


def _softmax_units(n_units, n_chunks, score_chunk, value_chunk, s_scr, sink_of, emit):
    state = {}
    for step in range(n_units + 1):
        ua, ub = step, step - 1
        for c in range(n_chunks):
            if ua < n_units:
                st = score_chunk(ua, c)
                tk = st.shape[0]
                s_scr[ua % 2, c * tk:(c + 1) * tk, :] = st
                mx = _fold8(st, jnp.max)
                state[ua, "mx"] = mx if c == 0 else jnp.maximum(state[ua, "mx"], mx)
            if ub >= 0:
                vt = value_chunk(ub, c)
                tk = vt.shape[1]
                p = jnp.exp2(s_scr[ub % 2, c * tk:(c + 1) * tk, :] - state[ub, "m"])
                pv = jnp.dot(vt, p.astype(BF16), preferred_element_type=F32)
                ps = _fold8(p, jnp.sum)
                state[ub, "acc"] = pv if c == 0 else state[ub, "acc"] + pv
                state[ub, "l8"] = ps if c == 0 else state[ub, "l8"] + ps
        if ua < n_units:
            m = jnp.max(state.pop((ua, "mx")), axis=0, keepdims=True)
            sink = sink_of(ua)
            state[ua, "m"] = m if sink is None else jnp.maximum(m, sink)
        if ub >= 0:
            m = state.pop((ub, "m"))
            denom = jnp.sum(state.pop((ub, "l8")), axis=0, keepdims=True)
            sink = sink_of(ub)
            if sink is not None:
                denom = denom + jnp.exp2(sink - m)
            emit(ub, state.pop((ub, "acc")) / denom)


def _gattn_kernel(q_ref, k_ref, vt_ref, o_ref, s_scr):
    seq = k_ref.shape[2]
    tk = min(seq, TK_GLOBAL)
    pending = {}

    def score_chunk(u, c):
        pair, half = u // 2, u % 2
        k = k_ref[pair // PAIRS_PER_KV, half, c * tk:(c + 1) * tk, :]
        return lax.dot_general(k, q_ref[pair], NT, preferred_element_type=F32)

    def value_chunk(u, c):
        return vt_ref[(u // 2) // PAIRS_PER_KV, :, c * tk:(c + 1) * tk]

    def emit(u, out):
        pair, half = u // 2, u % 2
        if half == 0:
            pending[pair] = out
        else:
            both = jnp.concatenate([pending.pop(pair), out], axis=0)
            o_ref[:, pair * LANES:(pair + 1) * LANES] = both.T.astype(BF16)

    _softmax_units(N_HEADS, seq // tk, score_chunk, value_chunk, s_scr, lambda u: None, emit)


def _gattn_call(q, k, vt, batch, seq):
    tq = TQ_GLOBAL
    nq = seq // tq
    n_tok = batch * seq
    return pl.pallas_call(
        _gattn_kernel,
        grid=(batch, nq),
        in_specs=[
            pl.BlockSpec((N_PAIRS, tq, LANES), lambda b, i: (0, b * nq + i, 0)),
            pl.BlockSpec((N_KV, 2, seq, LANES), lambda b, i: (0, 0, b, 0)),
            pl.BlockSpec((N_KV, HEAD_DIM, seq), lambda b, i: (0, 0, b)),
        ],
        out_specs=pl.BlockSpec((tq, Q_WIDTH), lambda b, i: (b * nq + i, 0)),
        out_shape=jax.ShapeDtypeStruct((n_tok, Q_WIDTH), BF16),
        scratch_shapes=[pltpu.VMEM((2, seq, tq), F32)],
        compiler_params=pltpu.CompilerParams(
            dimension_semantics=("arbitrary", "arbitrary"),
            vmem_limit_bytes=VMEM_LIMIT),
        name="gattn",
    )(q, k, vt)


def _t5_bucket_table():
    half = N_BUCKETS // 2
    max_exact = half // 2
    rel = np.arange(WK)[:, None] - WINDOW - np.arange(WQ)[None, :]
    n = np.abs(rel)
    assert MAX_DISTANCE // max_exact == 16 and half - max_exact == 8
    large = np.zeros_like(n)
    for kk in range(1, 2 * half):
        large += ((n * n) >= (2 ** kk) * max_exact * max_exact)
    large = np.minimum(max_exact + large, half - 1)
    bucket = np.where(rel > 0, half, 0) + np.where(n < max_exact, n, large)
    return bucket.astype(np.int32)


def _wattn_kernel(seq, q_ref, k_ref, vt_ref, bkt_ref, rb_ref, sink_ref, o_ref,
                  kpad, vtpad, bias_edge, bias_mid, s_scr):
    b = pl.program_id(0)
    i = pl.program_id(1)
    n_q_step = q_ref.shape[1] // WQ
    n_q_seq = seq // WQ
    n_blk_seq = seq // BLOCK
    edge = WINDOW

    @pl.when((b == 0) & (i == 0))
    def _():
        bucket = bkt_ref[...]
        row = lax.broadcasted_iota(jnp.int32, bucket.shape, 0)
        col = lax.broadcasted_iota(jnp.int32, bucket.shape, 1)
        band = jnp.abs(row - WINDOW - col) <= WINDOW
        neg = jnp.full((edge, WQ), NEG_INF, F32)
        bias_edge[0, N_HEADS] = neg
        bias_edge[1, N_HEADS] = neg
        for head in range(N_HEADS):
            bias = jnp.zeros(bucket.shape, F32)
            for bk in range(N_BUCKETS):
                bias = jnp.where(bucket == bk, rb_ref[bk, head], bias)
            bias = jnp.where(band, bias * LOG2E, NEG_INF)
            bias_edge[0, head] = bias[:edge]
            bias_mid[head] = bias[edge:WK - edge]
            bias_edge[1, head] = bias[WK - edge:]

    @pl.when(i == 0)
    def _():
        zk = jnp.zeros((WINDOW, LANES), BF16)
        for kv in range(N_KV):
            for half in range(2):
                kpad[kv, half, 0:WINDOW, :] = zk
                kpad[kv, half, WINDOW:WINDOW + seq, :] = k_ref[kv, half]
                kpad[kv, half, WINDOW + seq:2 * WINDOW + seq, :] = zk
        zv = jnp.zeros((KV_WIDTH, BLOCK), BF16)
        vtpad[0] = zv
        vtpad[pl.ds(1, n_blk_seq)] = vt_ref[...]
        vtpad[n_blk_seq + 1] = zv

    pending = {}

    def split(u):
        j, head = u // N_HEADS, u % N_HEADS
        return j, head, head // 2, (head // 2) // PAIRS_PER_KV

    def score_chunk(u, c):
        j, head, pair, kv = split(u)
        qb = i * n_q_step + j
        row0 = pl.multiple_of(qb * WQ, WQ)
        st = lax.dot_general(kpad[kv, head % 2, pl.ds(row0, WK), :], q_ref[pair, j * WQ:(j + 1) * WQ, :],
                             NT, preferred_element_type=F32)
        bias = jnp.concatenate([
            bias_edge[0, jnp.where(qb == 0, N_HEADS, head)],
            bias_mid[head],
            bias_edge[1, jnp.where(qb == n_q_seq - 1, N_HEADS, head)]], axis=0)
        return st + bias

    def value_chunk(u, c):
        j, head, pair, kv = split(u)
        blk0 = (i * n_q_step + j) * (WQ // BLOCK)
        return jnp.concatenate(
            [vtpad[blk0 + t, kv * HEAD_DIM:(kv + 1) * HEAD_DIM, :] for t in range(WK // BLOCK)], axis=1)

    def emit(u, out):
        j, head, pair, kv = split(u)
        if head % 2 == 0:
            pending[j, pair] = out
        else:
            both = jnp.concatenate([pending.pop((j, pair)), out], axis=0)
            o_ref[j * WQ:(j + 1) * WQ, pair * LANES:(pair + 1) * LANES] = both.T.astype(BF16)

    _softmax_units(n_q_step * N_HEADS, 1, score_chunk, value_chunk, s_scr,
                   lambda u: sink_ref[u % N_HEADS] * LOG2E, emit)


def _wattn_call(q, k, vt3, batch, seq, bucket, rel_bias, sink):
    tq = TQ_WINDOW
    nq = seq // tq
    n_tok = batch * seq
    n_blk_seq = seq // BLOCK
    return pl.pallas_call(
        functools.partial(_wattn_kernel, seq),
        grid=(batch, nq),
        in_specs=[
            pl.BlockSpec((N_PAIRS, tq, LANES), lambda b, i: (0, b * nq + i, 0)),
            pl.BlockSpec((N_KV, 2, seq, LANES), lambda b, i: (0, 0, b, 0)),
            pl.BlockSpec((n_blk_seq, KV_WIDTH, BLOCK), lambda b, i: (b, 0, 0)),
            _const_spec((WK, WQ)),
            pl.BlockSpec(memory_space=pltpu.SMEM),
            pl.BlockSpec(memory_space=pltpu.SMEM),
        ],
        out_specs=pl.BlockSpec((tq, Q_WIDTH), lambda b, i: (b * nq + i, 0)),
        out_shape=jax.ShapeDtypeStruct((n_tok, Q_WIDTH), BF16),
        scratch_shapes=[
            pltpu.VMEM((N_KV, 2, seq + 2 * WINDOW, LANES), BF16),
            pltpu.VMEM((n_blk_seq + 2, KV_WIDTH, BLOCK), BF16),
            pltpu.VMEM((2, N_HEADS + 1, WINDOW, WQ), F32),
            pltpu.VMEM((N_HEADS, WK - 2 * WINDOW, WQ), F32),
            pltpu.VMEM((2, WK, WQ), F32),
        ],
        compiler_params=pltpu.CompilerParams(
            dimension_semantics=("arbitrary", "arbitrary"),
            vmem_limit_bytes=VMEM_LIMIT),
        name="wattn",
    )(q, k, vt3, bucket, rel_bias, sink)


def _mix_kernel(x_ref, ya_ref, yb_ref, gpre_ref, wg_ref, bg_ref, wa_ref, wb_ref, wo_ref, gpost_ref, o_ref):
    x = x_ref[...]
    h = _rmsnorm(x, gpre_ref[...]).astype(BF16)
    z = jnp.dot(h, wg_ref[...], preferred_element_type=F32) + bg_ref[...]
    gates = 1.0 / (1.0 + jnp.exp(-z))
    a = jnp.dot(ya_ref[...], wa_ref[...], preferred_element_type=F32)
    b = jnp.dot(yb_ref[...], wb_ref[...], preferred_element_type=F32)
    mix = gates[:, :D_MODEL] * a + gates[:, D_MODEL:] * b
    o = jnp.dot(mix.astype(BF16), wo_ref[...], preferred_element_type=F32)
    o_ref[...] = x + _rmsnorm(o, gpost_ref[...])


def _mix_call(x2d, ya, yb, g_pre, w_gate, b_gate, w_a, w_b, w_out, g_post):
    n_tok = x2d.shape[0]
    tm = TM_MIX
    return pl.pallas_call(
        _mix_kernel,
        grid=(n_tok // tm,),
        in_specs=[
            pl.BlockSpec((tm, D_MODEL), lambda i: (i, 0)),
            pl.BlockSpec((tm, Q_WIDTH), lambda i: (i, 0)),
            pl.BlockSpec((tm, Q_WIDTH), lambda i: (i, 0)),
            _const_spec((1, D_MODEL)),
            _const_spec((D_MODEL, 2 * D_MODEL)),
            _const_spec((1, 2 * D_MODEL)),
            _const_spec((Q_WIDTH, D_MODEL)),
            _const_spec((Q_WIDTH, D_MODEL)),
            _const_spec((D_MODEL, D_MODEL)),
            _const_spec((1, D_MODEL)),
        ],
        out_specs=pl.BlockSpec((tm, D_MODEL), lambda i: (i, 0)),
        out_shape=jax.ShapeDtypeStruct((n_tok, D_MODEL), F32),
        compiler_params=pltpu.CompilerParams(
            dimension_semantics=("arbitrary",), vmem_limit_bytes=VMEM_LIMIT),
        name="mix",
    )(x2d, ya, yb, g_pre, w_gate, b_gate, w_a, w_b, w_out, g_post)


def _ffn_kernel(x_ref, gpre_ref, wg_ref, wu_ref, wd_ref, gpost_ref, o_ref):
    x = x_ref[...]
    h = _rmsnorm(x, gpre_ref[...]).astype(BF16)
    g = jnp.dot(h, wg_ref[...], preferred_element_type=F32)
    u = jnp.dot(h, wu_ref[...], preferred_element_type=F32)
    act = (g / (1.0 + jnp.exp(-g))) * u
    f = jnp.dot(act.astype(BF16), wd_ref[...], preferred_element_type=F32)
    o_ref[...] = x + _rmsnorm(f, gpost_ref[...])


def _ffn_call(x2d, g_pre, w_g, w_u, w_d, g_post):
    n_tok = x2d.shape[0]
    tm = TM_FFN
    return pl.pallas_call(
        _ffn_kernel,
        grid=(n_tok // tm,),
        in_specs=[
            pl.BlockSpec((tm, D_MODEL), lambda i: (i, 0)),
            _const_spec((1, D_MODEL)),
            _const_spec((D_MODEL, D_FF)),
            _const_spec((D_MODEL, D_FF)),
            _const_spec((D_FF, D_MODEL)),
            _const_spec((1, D_MODEL)),
        ],
        out_specs=pl.BlockSpec((tm, D_MODEL), lambda i: (i, 0)),
        out_shape=jax.ShapeDtypeStruct((n_tok, D_MODEL), F32),
        compiler_params=pltpu.CompilerParams(
            dimension_semantics=("arbitrary",), vmem_limit_bytes=VMEM_LIMIT),
        name="ffn",
    )(x2d, g_pre, w_g, w_u, w_d, g_post)


def _rope_tables(seq):
    pos = jnp.arange(seq)
    rows = (pos // GRID_W).astype(F32)
    cols = (pos % GRID_W).astype(F32)
    inv_freq = 1.0 / (ROPE_THETA ** (jnp.arange(0, ROPE_HALF, 2, dtype=F32) / ROPE_HALF))
    lane = np.arange(LANES)
    d = lane % HEAD_DIM
    use_col = (d // ROPE_HALF) == 1
    j = d % ROPE_HALF
    f_idx = j % ROPE_QUARTER
    coord = jnp.where(jnp.asarray(use_col)[None, :], cols[:, None], rows[:, None])
    ang = coord * inv_freq[jnp.asarray(f_idx)][None, :]
    cos, sin = jnp.cos(ang), jnp.sin(ang)
    first = jnp.asarray(j < ROPE_QUARTER)[None, :]
    sin_up = jnp.where(first, -sin, 0.0)
    sin_dn = jnp.where(first, 0.0, sin)
    return jnp.stack([cos, sin_up, sin_dn]).astype(F32)


def _layer(x, bucket, bd, p):
    batch, seq, _ = x.shape
    assert seq % TM_PRE == 0 and seq % TQ_GLOBAL == 0 and seq % TQ_WINDOW == 0 and seq // WQ >= 2
    x2d = x.reshape(batch * seq, D_MODEL)
    qa, ka, vat, qb, kb, vbt = _pre_call(x2d, seq, p["g_mix_pre"], p["w_in"], p["qk_gain"],
                                         _rope_tables(seq), bd)
    ya = _gattn_call(qa, ka, vat, batch, seq)
    yb = _wattn_call(qb, kb, vbt, batch, seq, bucket, p["rel_bias"], p["sink"])
    x1 = _mix_call(x2d, ya, yb, p["g_mix_pre"], p["w_gate"], p["b_gate"], p["w_a"], p["w_b"], p["w_out"],
                   p["g_mix_post"])
    y = _ffn_call(x1, p["g_ffn_pre"], p["w_ffn_gate"], p["w_ffn_up"], p["w_ffn_down"], p["g_ffn_post"])
    return y.reshape(batch, seq, D_MODEL)


def kernel(x_prompt, x_sample, norm_mix_pre, norm_mix_post, w_in, q_norm_a, k_norm_a, sink_b, rel_bias,
           w_branch_a, w_branch_b, w_gate, b_gate, w_out, norm_ffn_pre, norm_ffn_post, w_ffn_gate,
           w_ffn_up, w_ffn_down):
    depth = w_in.shape[0]
    bucket = jnp.asarray(_t5_bucket_table())
    blk = np.arange(LANES) // HEAD_DIM
    bd = jnp.asarray(blk[:, None] == blk[None, :], dtype=BF16)
    y_prompt, y_sample = x_prompt, x_sample
    for l in range(depth):
        p = {
            "g_mix_pre": norm_mix_pre[l].reshape(1, D_MODEL),
            "g_mix_post": norm_mix_post[l].reshape(1, D_MODEL),
            "w_in": w_in[l].astype(BF16),
            "qk_gain": jnp.stack([jnp.tile(q_norm_a[l], 2), jnp.tile(k_norm_a[l], 2)]).reshape(2, 1, LANES),
            "sink": sink_b[l],
            "rel_bias": rel_bias,
            "w_a": w_branch_a[l].astype(BF16),
            "w_b": w_branch_b[l].astype(BF16),
            "w_gate": w_gate[l].astype(BF16),
            "b_gate": b_gate[l].reshape(1, 2 * D_MODEL),
            "w_out": w_out[l].astype(BF16),
            "g_ffn_pre": norm_ffn_pre[l].reshape(1, D_MODEL),
            "g_ffn_post": norm_ffn_post[l].reshape(1, D_MODEL),
            "w_ffn_gate": w_ffn_gate[l].astype(BF16),
            "w_ffn_up": w_ffn_up[l].astype(BF16),
            "w_ffn_down": w_ffn_down[l].astype(BF16),
        }
        y_prompt = _layer(y_prompt, bucket, bd, p)
        y_sample = _layer(y_sample, bucket, bd, p)
    return (y_prompt, y_sample)
```

```python
import functools
import math

import numpy as np
import jax
import jax.numpy as jnp
from jax import lax
from jax.experimental import pallas as pl
from jax.experimental.pallas import tpu as pltpu

F32 = jnp.float32
BF16 = jnp.bfloat16

D_MODEL = 1024
HEAD_DIM = 64
N_HEADS = 8
N_KV = 2
Q_WIDTH = N_HEADS * HEAD_DIM
KV_WIDTH = N_KV * HEAD_DIM
IN_WIDTH = 2 * (Q_WIDTH + 2 * KV_WIDTH)
BLOCK = 128
WINDOW = 128
GRID_W = 64
ROPE_THETA = 10000.0
ROPE_HALF = HEAD_DIM // 2
ROPE_QUARTER = ROPE_HALF // 2
N_BUCKETS = 32
MAX_DISTANCE = 128
D_FF = 2816
EPS = 1e-6
NEG_INF = -1e30
LOG2E = 1.4426950408889634
Q_SCALE = LOG2E / math.sqrt(HEAD_DIM)

LANES = 128
SUBLANES = 8
N_PAIRS = N_HEADS // 2
PAIRS_PER_KV = N_PAIRS // N_KV


TM_PRE = 512
TM_MIX = 512
TM_FFN = 512
TQ_GLOBAL = 512
TK_SCORE = 512
TK_VALUE = 512
BF16_SUBLANES = 16
VT_ROWS = HEAD_DIM + BF16_SUBLANES
WQ = 256
WK = WQ + 2 * WINDOW
TQ_WINDOW = 1024
VMEM_LIMIT = 56 * 1024 * 1024
GLOBAL_HEADS_PER_UNIT = 1

NT = (((1,), (1,)), ((), ()))


def _rmsnorm(x, gain):
    var = jnp.mean(x * x, axis=-1, keepdims=True)
    return x * lax.rsqrt(var + EPS) * gain


def _const_spec(shape):
    zeros = (0,) * len(shape)
    return pl.BlockSpec(shape, lambda *_: zeros, pipeline_mode=pl.Buffered(1))


def _pre_kernel(x_ref, g_ref, w_ref, qkg_ref, rope_ref, bd_ref,
                qa_ref, ka_ref, vat_ref, qb_ref, kb_ref, vbt_ref):
    x = x_ref[...]
    tm = x.shape[0]
    h = _rmsnorm(x, g_ref[...]).astype(BF16)
    proj = jnp.dot(h, w_ref[...], preferred_element_type=F32)
    bd = bd_ref[...]
    cos, sin_up, sin_dn = rope_ref[0], rope_ref[1], rope_ref[2]
    lane = lax.broadcasted_iota(jnp.int32, (tm, LANES), 1)
    lo_half = lane < HEAD_DIM

    def qk_norm_rope(z, gain):
        sq = z * z
        hi = sq.astype(BF16)
        lo = (sq - hi.astype(F32)).astype(BF16)
        ssq = (jnp.dot(hi, bd, preferred_element_type=F32)
               + jnp.dot(lo, bd, preferred_element_type=F32))
        zn = z * lax.rsqrt(ssq * (1.0 / HEAD_DIM) + EPS) * gain
        return (zn * cos
                + pltpu.roll(zn, LANES - ROPE_QUARTER, 1) * sin_up
                + pltpu.roll(zn, ROPE_QUARTER, 1) * sin_dn)

    def put_k(ref, z):
        swapped = pltpu.roll(z, HEAD_DIM, 1)
        ref[0, 0] = jnp.where(lo_half, z, 0.0).astype(BF16)
        ref[0, 1] = jnp.where(lo_half, 0.0, swapped).astype(BF16)
        ref[1, 0] = jnp.where(lo_half, swapped, 0.0).astype(BF16)
        ref[1, 1] = jnp.where(lo_half, 0.0, z).astype(BF16)

    for s in range(N_PAIRS):
        q = qk_norm_rope(proj[:, s * LANES:(s + 1) * LANES], qkg_ref[0])
        qa_ref[s] = (q * Q_SCALE).astype(BF16)
    put_k(ka_ref, qk_norm_rope(proj[:, 512:640], qkg_ref[1]))

    def vt_with_ones(v):
        vt = v.T.astype(BF16)
        ones = jnp.ones((VT_ROWS - HEAD_DIM, tm), BF16)
        return jnp.concatenate([vt[:HEAD_DIM], ones, vt[HEAD_DIM:], ones], axis=0)

    vat_ref[...] = vt_with_ones(proj[:, 640:768]).reshape(N_KV, VT_ROWS, tm)
    for s in range(N_PAIRS):
        qb_ref[s] = (proj[:, 768 + s * LANES:768 + (s + 1) * LANES] * Q_SCALE).astype(BF16)
    put_k(kb_ref, proj[:, 1280:1408])
    vbt = vt_with_ones(proj[:, 1408:1536])
    for j in range(tm // BLOCK):
        vbt_ref[j] = vbt[:, j * BLOCK:(j + 1) * BLOCK]


def _pre_call(x2d, seq, g_pre, w_in, qk_gain, rope, bd):
    n_tok = x2d.shape[0]
    tm = TM_PRE
    tiles_per_seq = seq // tm
    q_spec = pl.BlockSpec((N_PAIRS, tm, LANES), lambda i: (0, i, 0))
    k_spec = pl.BlockSpec((N_KV, 2, tm, LANES), lambda i: (0, 0, i, 0))
    q_shape = jax.ShapeDtypeStruct((N_PAIRS, n_tok, LANES), BF16)
    k_shape = jax.ShapeDtypeStruct((N_KV, 2, n_tok, LANES), BF16)
    return pl.pallas_call(
        _pre_kernel,
        grid=(n_tok // tm,),
        in_specs=[
            pl.BlockSpec((tm, D_MODEL), lambda i: (i, 0)),
            _const_spec((1, D_MODEL)),
            _const_spec((D_MODEL, IN_WIDTH)),
            _const_spec((2, 1, LANES)),
            pl.BlockSpec((3, tm, LANES), lambda i: (0, i % tiles_per_seq, 0)),
            _const_spec((LANES, LANES)),
        ],
        out_specs=[
            q_spec, k_spec,
            pl.BlockSpec((N_KV, VT_ROWS, tm), lambda i: (0, 0, i)),
            q_spec, k_spec,
            pl.BlockSpec((tm // BLOCK, N_KV * VT_ROWS, BLOCK), lambda i: (i, 0, 0)),
        ],
        out_shape=[
            q_shape, k_shape,
            jax.ShapeDtypeStruct((N_KV, VT_ROWS, n_tok), BF16),
            q_shape, k_shape,
            jax.ShapeDtypeStruct((n_tok // BLOCK, N_KV * VT_ROWS, BLOCK), BF16),
        ],
        compiler_params=pltpu.CompilerParams(
            dimension_semantics=("arbitrary",), vmem_limit_bytes=VMEM_LIMIT),
        name="pre",
    )(x2d, g_pre, w_in, qk_gain, rope, bd)


def _fold8(x, op):
    return op(x.reshape(x.shape[0] // SUBLANES, SUBLANES, x.shape[1]), axis=0)


def _softmax_units(n_units, group, n_keys, tk_a, tk_b, score_chunk, value_chunk, s_scr, sink_of, emit):
    assert tk_a % tk_b == 0 and n_keys % tk_a == 0 and n_units % 2 == 0
    n_a, b_per_a = n_keys // tk_a, tk_a // tk_b

    def sinks(u):
        vals = [sink_of(u, g) for g in range(group)]
        return None if vals[0] is None else vals

    def sweeps(ua, s_write, ub, m_b, s_read):
        mx = None
        acc = [None] * group
        for ca in range(n_a):
            if ua is not None:
                st = jnp.concatenate([score_chunk(ua, g, ca * tk_a, tk_a) for g in range(group)], axis=1)
                s_write[ca * tk_a:(ca + 1) * tk_a, :] = st
                cm = _fold8(st, jnp.max)
                mx = cm if mx is None else jnp.maximum(mx, cm)
            if ub is not None:
                for cb in range(ca * b_per_a, (ca + 1) * b_per_a):
                    p = jnp.exp2(s_read[cb * tk_b:(cb + 1) * tk_b, :] - m_b).astype(BF16)
                    vt = value_chunk(ub, cb * tk_b, tk_b)
                    tq = p.shape[1] // group
                    for g in range(group):
                        pv = jnp.dot(vt, p[:, g * tq:(g + 1) * tq], preferred_element_type=F32)
                        acc[g] = pv if acc[g] is None else acc[g] + pv
        if ub is not None:
            sink = sinks(ub)
            for g in range(group):
                denom = acc[g][HEAD_DIM:HEAD_DIM + 1]
                if sink is not None:
                    tq = denom.shape[1]
                    denom = denom + jnp.exp2(sink[g] - m_b[:, g * tq:(g + 1) * tq])
                emit(ub, g, acc[g][:HEAD_DIM] / denom)
        if ua is None:
            return None
        m = jnp.max(mx, axis=0, keepdims=True)
        sink = sinks(ua)
        if sink is not None:
            tq = m.shape[1] // group
            m = jnp.concatenate([jnp.maximum(m[:, g * tq:(g + 1) * tq], sink[g]) for g in range(group)],
                                axis=1)
        return m

    def step(t, m):
        u = 2 * t
        m = sweeps(u + 1, s_scr[1], u, m, s_scr[0])
        return sweeps(u + 2, s_scr[0], u + 1, m, s_scr[1])

    trips = n_units // 2 - 1 + jnp.minimum(pl.program_id(0), 0)
    m = lax.fori_loop(0, trips, step, sweeps(0, s_scr[0], None, None, None))
    m = sweeps(n_units - 1, s_scr[1], n_units - 2, m, s_scr[0])
    sweeps(None, None, n_units - 1, m, s_scr[1])


def _gattn_kernel(group, q_ref, k_ref, vt_ref, o_ref, s_even, s_odd, ot_scr):
    seq = k_ref.shape[2]
    units_per_kv = N_HEADS // N_KV // group

    def score_chunk(u, g, row0, tk):
        head = u * group + g
        k = k_ref[u // units_per_kv, head % 2, row0:row0 + tk, :]
        return lax.dot_general(k, q_ref[head // 2], NT, preferred_element_type=F32)

    def value_chunk(u, row0, tk):
        return vt_ref[u // units_per_kv, :, row0:row0 + tk]

    def emit(u, g, out):
        ot_scr[u * group + g] = out

    _softmax_units(N_HEADS // group, group, seq, TK_SCORE, TK_VALUE, score_chunk, value_chunk,
                   (s_even, s_odd), lambda u, g: None, emit)
    for pair in range(N_PAIRS):
        both = jnp.concatenate([ot_scr[2 * pair], ot_scr[2 * pair + 1]], axis=0)
        o_ref[:, pair * LANES:(pair + 1) * LANES] = both.T.astype(BF16)


def _gattn_call(q, k, vt, batch, seq):
    tq = TQ_GLOBAL
    nq = seq // tq
    n_tok = batch * seq
    group = GLOBAL_HEADS_PER_UNIT
    return pl.pallas_call(
        functools.partial(_gattn_kernel, group),
        grid=(batch, nq),
        in_specs=[
            pl.BlockSpec((N_PAIRS, tq, LANES), lambda b, i: (0, b * nq + i, 0)),
            pl.BlockSpec((N_KV, 2, seq, LANES), lambda b, i: (0, 0, b, 0)),
            pl.BlockSpec((N_KV, VT_ROWS, seq), lambda b, i: (0, 0, b)),
        ],
        out_specs=pl.BlockSpec((tq, Q_WIDTH), lambda b, i: (b * nq + i, 0)),
        out_shape=jax.ShapeDtypeStruct((n_tok, Q_WIDTH), BF16),
        scratch_shapes=[pltpu.VMEM((seq, group * tq), F32), pltpu.VMEM((seq, group * tq), F32),
                        pltpu.VMEM((N_HEADS, HEAD_DIM, tq), F32)],
        compiler_params=pltpu.CompilerParams(
            dimension_semantics=("arbitrary", "arbitrary"),
            vmem_limit_bytes=VMEM_LIMIT),
        name="gattn",
    )(q, k, vt)


def _t5_bucket_table():
    half = N_BUCKETS // 2
    max_exact = half // 2
    rel = np.arange(WK)[:, None] - WINDOW - np.arange(WQ)[None, :]
    n = np.abs(rel)
    assert MAX_DISTANCE // max_exact == 16 and half - max_exact == 8
    large = np.zeros_like(n)
    for kk in range(1, 2 * half):
        large += ((n * n) >= (2 ** kk) * max_exact * max_exact)
    large = np.minimum(max_exact + large, half - 1)
    bucket = np.where(rel > 0, half, 0) + np.where(n < max_exact, n, large)
    return bucket.astype(np.int32)


def _wattn_kernel(seq, q_ref, k_ref, vt_ref, bkt_ref, rb_ref, sink_ref, o_ref,
                  kpad, vtpad, bias_edge, bias_mid, s_even, s_odd, ot_scr):
    b = pl.program_id(0)
    i = pl.program_id(1)
    n_q_step = q_ref.shape[1] // WQ
    n_q_seq = seq // WQ
    n_blk_seq = seq // BLOCK
    edge = WINDOW

    @pl.when((b == 0) & (i == 0))
    def _():
        bucket = bkt_ref[...]
        row = lax.broadcasted_iota(jnp.int32, bucket.shape, 0)
        col = lax.broadcasted_iota(jnp.int32, bucket.shape, 1)
        band = jnp.abs(row - WINDOW - col) <= WINDOW
        neg = jnp.full((edge, WQ), NEG_INF, F32)
        bias_edge[0, N_HEADS] = neg
        bias_edge[1, N_HEADS] = neg
        for head in range(N_HEADS):
            bias = jnp.zeros(bucket.shape, F32)
            for bk in range(N_BUCKETS):
                bias = jnp.where(bucket == bk, rb_ref[bk, head], bias)
            bias = jnp.where(band, bias * LOG2E, NEG_INF)
            bias_edge[0, head] = bias[:edge]
            bias_mid[head] = bias[edge:WK - edge]
            bias_edge[1, head] = bias[WK - edge:]

    @pl.when(i == 0)
    def _():
        zk = jnp.zeros((WINDOW, LANES), BF16)
        for kv in range(N_KV):
            for half in range(2):
                kpad[kv, half, 0:WINDOW, :] = zk
                kpad[kv, half, WINDOW:WINDOW + seq, :] = k_ref[kv, half]
                kpad[kv, half, WINDOW + seq:2 * WINDOW + seq, :] = zk
        zv = jnp.zeros((N_KV, VT_ROWS, BLOCK), BF16)
        vtpad[0] = zv
        vtpad[pl.ds(1, n_blk_seq)] = vt_ref[...].reshape(n_blk_seq, N_KV, VT_ROWS, BLOCK)
        vtpad[n_blk_seq + 1] = zv

    group = N_HEADS // N_KV

    def score_chunk(u, g, row0, tk):
        assert row0 == 0 and tk == WK
        j, kv = u // N_KV, u % N_KV
        head = kv * group + g
        qb = i * n_q_step + j
        q = q_ref[head // 2, pl.ds(pl.multiple_of(j * WQ, WQ), WQ), :]
        st = lax.dot_general(kpad[kv, g % 2, pl.ds(pl.multiple_of(qb * WQ, WQ), WK), :], q,
                             NT, preferred_element_type=F32)
        bias = jnp.concatenate([
            bias_edge[0, jnp.where(qb == 0, N_HEADS, head)],
            bias_mid[head],
            bias_edge[1, jnp.where(qb == n_q_seq - 1, N_HEADS, head)]], axis=0)
        return st + bias

    def value_chunk(u, row0, tk):
        j, kv = u // N_KV, u % N_KV
        blk0 = (i * n_q_step + j) * (WQ // BLOCK) + row0 // BLOCK
        return jnp.concatenate([vtpad[blk0 + t, kv] for t in range(tk // BLOCK)], axis=1)

    def emit(u, g, out):
        ot_scr[(u // N_KV) * N_HEADS + (u % N_KV) * group + g] = out

    _softmax_units(n_q_step * N_KV, group, WK, WK, TK_VALUE, score_chunk, value_chunk, (s_even, s_odd),
                   lambda u, g: sink_ref[(u % N_KV) * group + g] * LOG2E, emit)
    for j in range(n_q_step):
        for pair in range(N_PAIRS):
            u = j * N_HEADS + 2 * pair
            both = jnp.concatenate([ot_scr[u], ot_scr[u + 1]], axis=0)
            o_ref[j * WQ:(j + 1) * WQ, pair * LANES:(pair + 1) * LANES] = both.T.astype(BF16)


def _wattn_call(q, k, vt3, batch, seq, bucket, rel_bias, sink):
    tq = TQ_WINDOW
    nq = seq // tq
    n_tok = batch * seq
    n_blk_seq = seq // BLOCK
    return pl.pallas_call(
        functools.partial(_wattn_kernel, seq),
        grid=(batch, nq),
        in_specs=[
            pl.BlockSpec((N_PAIRS, tq, LANES), lambda b, i: (0, b * nq + i, 0)),
            pl.BlockSpec((N_KV, 2, seq, LANES), lambda b, i: (0, 0, b, 0)),
            pl.BlockSpec((n_blk_seq, N_KV * VT_ROWS, BLOCK), lambda b, i: (b, 0, 0)),
            _const_spec((WK, WQ)),
            pl.BlockSpec(memory_space=pltpu.SMEM),
            pl.BlockSpec(memory_space=pltpu.SMEM),
        ],
        out_specs=pl.BlockSpec((tq, Q_WIDTH), lambda b, i: (b * nq + i, 0)),
        out_shape=jax.ShapeDtypeStruct((n_tok, Q_WIDTH), BF16),
        scratch_shapes=[
            pltpu.VMEM((N_KV, 2, seq + 2 * WINDOW, LANES), BF16),
            pltpu.VMEM((n_blk_seq + 2, N_KV, VT_ROWS, BLOCK), BF16),
            pltpu.VMEM((2, N_HEADS + 1, WINDOW, WQ), F32),
            pltpu.VMEM((N_HEADS, WK - 2 * WINDOW, WQ), F32),
            pltpu.VMEM((WK, N_HEADS // N_KV * WQ), F32),
            pltpu.VMEM((WK, N_HEADS // N_KV * WQ), F32),
            pltpu.VMEM((tq // WQ * N_HEADS, HEAD_DIM, WQ), F32),
        ],
        compiler_params=pltpu.CompilerParams(
            dimension_semantics=("arbitrary", "arbitrary"),
            vmem_limit_bytes=VMEM_LIMIT),
        name="wattn",
    )(q, k, vt3, bucket, rel_bias, sink)


def _mix_kernel(x_ref, ya_ref, yb_ref, gpre_ref, wg_ref, bg_ref, wa_ref, wb_ref, wo_ref, gpost_ref, o_ref):
    x = x_ref[...]
    h = _rmsnorm(x, gpre_ref[...]).astype(BF16)
    z = jnp.dot(h, wg_ref[...], preferred_element_type=F32) + bg_ref[...]
    gates = 1.0 / (1.0 + jnp.exp(-z))
    a = jnp.dot(ya_ref[...], wa_ref[...], preferred_element_type=F32)
    b = jnp.dot(yb_ref[...], wb_ref[...], preferred_element_type=F32)
    mix = gates[:, :D_MODEL] * a + gates[:, D_MODEL:] * b
    o = jnp.dot(mix.astype(BF16), wo_ref[...], preferred_element_type=F32)
    o_ref[...] = x + _rmsnorm(o, gpost_ref[...])


def _mix_call(x2d, ya, yb, g_pre, w_gate, b_gate, w_a, w_b, w_out, g_post):
    n_tok = x2d.shape[0]
    tm = TM_MIX
    return pl.pallas_call(
        _mix_kernel,
        grid=(n_tok // tm,),
        in_specs=[
            pl.BlockSpec((tm, D_MODEL), lambda i: (i, 0)),
            pl.BlockSpec((tm, Q_WIDTH), lambda i: (i, 0)),
            pl.BlockSpec((tm, Q_WIDTH), lambda i: (i, 0)),
            _const_spec((1, D_MODEL)),
            _const_spec((D_MODEL, 2 * D_MODEL)),
            _const_spec((1, 2 * D_MODEL)),
            _const_spec((Q_WIDTH, D_MODEL)),
            _const_spec((Q_WIDTH, D_MODEL)),
            _const_spec((D_MODEL, D_MODEL)),
            _const_spec((1, D_MODEL)),
        ],
        out_specs=pl.BlockSpec((tm, D_MODEL), lambda i: (i, 0)),
        out_shape=jax.ShapeDtypeStruct((n_tok, D_MODEL), F32),
        compiler_params=pltpu.CompilerParams(
            dimension_semantics=("arbitrary",), vmem_limit_bytes=VMEM_LIMIT),
        name="mix",
    )(x2d, ya, yb, g_pre, w_gate, b_gate, w_a, w_b, w_out, g_post)


def _ffn_kernel(x_ref, gpre_ref, wg_ref, wu_ref, wd_ref, gpost_ref, o_ref):
    x = x_ref[...]
    h = _rmsnorm(x, gpre_ref[...]).astype(BF16)
    g = jnp.dot(h, wg_ref[...], preferred_element_type=F32)
    u = jnp.dot(h, wu_ref[...], preferred_element_type=F32)
    act = (g / (1.0 + jnp.exp(-g))) * u
    f = jnp.dot(act.astype(BF16), wd_ref[...], preferred_element_type=F32)
    o_ref[...] = x + _rmsnorm(f, gpost_ref[...])


def _ffn_call(x2d, g_pre, w_g, w_u, w_d, g_post):
    n_tok = x2d.shape[0]
    tm = TM_FFN
    return pl.pallas_call(
        _ffn_kernel,
        grid=(n_tok // tm,),
        in_specs=[
            pl.BlockSpec((tm, D_MODEL), lambda i: (i, 0)),
            _const_spec((1, D_MODEL)),
            _const_spec((D_MODEL, D_FF)),
            _const_spec((D_MODEL, D_FF)),
            _const_spec((D_FF, D_MODEL)),
            _const_spec((1, D_MODEL)),
        ],
        out_specs=pl.BlockSpec((tm, D_MODEL), lambda i: (i, 0)),
        out_shape=jax.ShapeDtypeStruct((n_tok, D_MODEL), F32),
        compiler_params=pltpu.CompilerParams(
            dimension_semantics=("arbitrary",), vmem_limit_bytes=VMEM_LIMIT),
        name="ffn",
    )(x2d, g_pre, w_g, w_u, w_d, g_post)


def _rope_tables(seq):
    pos = jnp.arange(seq)
    rows = (pos // GRID_W).astype(F32)
    cols = (pos % GRID_W).astype(F32)
    inv_freq = 1.0 / (ROPE_THETA ** (jnp.arange(0, ROPE_HALF, 2, dtype=F32) / ROPE_HALF))
    lane = np.arange(LANES)
    d = lane % HEAD_DIM
    use_col = (d // ROPE_HALF) == 1
    j = d % ROPE_HALF
    f_idx = j % ROPE_QUARTER
    coord = jnp.where(jnp.asarray(use_col)[None, :], cols[:, None], rows[:, None])
    ang = coord * inv_freq[jnp.asarray(f_idx)][None, :]
    cos, sin = jnp.cos(ang), jnp.sin(ang)
    first = jnp.asarray(j < ROPE_QUARTER)[None, :]
    sin_up = jnp.where(first, -sin, 0.0)
    sin_dn = jnp.where(first, 0.0, sin)
    return jnp.stack([cos, sin_up, sin_dn]).astype(F32)


def _layer(x, bucket, bd, p):
    batch, seq, _ = x.shape
    assert seq % TM_PRE == 0 and seq % TQ_GLOBAL == 0 and seq % TQ_WINDOW == 0 and seq // WQ >= 2
    x2d = x.reshape(batch * seq, D_MODEL)
    qa, ka, vat, qb, kb, vbt = _pre_call(x2d, seq, p["g_mix_pre"], p["w_in"], p["qk_gain"],
                                         _rope_tables(seq), bd)
    ya = _gattn_call(qa, ka, vat, batch, seq)
    yb = _wattn_call(qb, kb, vbt, batch, seq, bucket, p["rel_bias"], p["sink"])
    x1 = _mix_call(x2d, ya, yb, p["g_mix_pre"], p["w_gate"], p["b_gate"], p["w_a"], p["w_b"], p["w_out"],
                   p["g_mix_post"])
    y = _ffn_call(x1, p["g_ffn_pre"], p["w_ffn_gate"], p["w_ffn_up"], p["w_ffn_down"], p["g_ffn_post"])
    return y.reshape(batch, seq, D_MODEL)


def kernel(x_prompt, x_sample, norm_mix_pre, norm_mix_post, w_in, q_norm_a, k_norm_a, sink_b, rel_bias,
           w_branch_a, w_branch_b, w_gate, b_gate, w_out, norm_ffn_pre, norm_ffn_post, w_ffn_gate,
           w_ffn_up, w_ffn_down):
    depth = w_in.shape[0]
    bucket = jnp.asarray(_t5_bucket_table())
    blk = np.arange(LANES) // HEAD_DIM
    bd = jnp.asarray(blk[:, None] == blk[None, :], dtype=BF16)
    y_prompt, y_sample = x_prompt, x_sample
    for l in range(depth):
        p = {
            "g_mix_pre": norm_mix_pre[l].reshape(1, D_MODEL),
            "g_mix_post": norm_mix_post[l].reshape(1, D_MODEL),
            "w_in": w_in[l].astype(BF16),
            "qk_gain": jnp.stack([jnp.tile(q_norm_a[l], 2), jnp.tile(k_norm_a[l], 2)]).reshape(2, 1, LANES),
            "sink": sink_b[l],
            "rel_bias": rel_bias,
            "w_a": w_branch_a[l].astype(BF16),
            "w_b": w_branch_b[l].astype(BF16),
            "w_gate": w_gate[l].astype(BF16),
            "b_gate": b_gate[l].reshape(1, 2 * D_MODEL),
            "w_out": w_out[l].astype(BF16),
            "g_ffn_pre": norm_ffn_pre[l].reshape(1, D_MODEL),
            "g_ffn_post": norm_ffn_post[l].reshape(1, D_MODEL),
            "w_ffn_gate": w_ffn_gate[l].astype(BF16),
            "w_ffn_up": w_ffn_up[l].astype(BF16),
            "w_ffn_down": w_ffn_down[l].astype(BF16),
        }
        y_prompt = _layer(y_prompt, bucket, bd, p)
        y_sample = _layer(y_sample, bucket, bd, p)
    return (y_prompt, y_sample)
```

```python
import functools
import math

import numpy as np
import jax
import jax.numpy as jnp
from jax import lax
from jax.experimental import pallas as pl
from jax.experimental.pallas import tpu as pltpu

F32 = jnp.float32
BF16 = jnp.bfloat16

D_MODEL = 1024
HEAD_DIM = 64
N_HEADS = 8
N_KV = 2
Q_WIDTH = N_HEADS * HEAD_DIM
KV_WIDTH = N_KV * HEAD_DIM
IN_WIDTH = 2 * (Q_WIDTH + 2 * KV_WIDTH)
BLOCK = 128
WINDOW = 128
GRID_W = 64
ROPE_THETA = 10000.0
ROPE_HALF = HEAD_DIM // 2
ROPE_QUARTER = ROPE_HALF // 2
N_BUCKETS = 32
MAX_DISTANCE = 128
D_FF = 2816
EPS = 1e-6
NEG_INF = -1e30
LOG2E = 1.4426950408889634
Q_SCALE = LOG2E / math.sqrt(HEAD_DIM)

LANES = 128
SUBLANES = 8
N_PAIRS = N_HEADS // 2
PAIRS_PER_KV = N_PAIRS // N_KV


TM_PRE = 512
TM_MIX = 512
TM_FFN = 512
TQ_GLOBAL = 512
TK_SCORE = 512
TK_VALUE = 256
BF16_SUBLANES = 16
VT_ROWS = HEAD_DIM + BF16_SUBLANES
WQ = 256
WK = WQ + 2 * WINDOW
TQ_WINDOW = 1024
VMEM_LIMIT = 56 * 1024 * 1024
SCORE_PAD_LANES = 128
GLOBAL_HEADS_PER_UNIT = 1

NT = (((1,), (1,)), ((), ()))


def _rmsnorm(x, gain):
    var = jnp.mean(x * x, axis=-1, keepdims=True)
    return x * lax.rsqrt(var + EPS) * gain


def _const_spec(shape):
    zeros = (0,) * len(shape)
    return pl.BlockSpec(shape, lambda *_: zeros, pipeline_mode=pl.Buffered(1))


def _pre_kernel(x_ref, g_ref, w_ref, qkg_ref, rope_ref, bd_ref,
                qa_ref, ka_ref, vat_ref, qb_ref, kb_ref, vbt_ref):
    x = x_ref[...]
    tm = x.shape[0]
    h = _rmsnorm(x, g_ref[...]).astype(BF16)
    half_w = IN_WIDTH // 2
    proj_a = jnp.dot(h, w_ref[:, :half_w], preferred_element_type=F32)
    proj_b = jnp.dot(h, w_ref[:, half_w:], preferred_element_type=F32)
    bd = bd_ref[...]
    cos, sin_up, sin_dn = rope_ref[0], rope_ref[1], rope_ref[2]
    lane = lax.broadcasted_iota(jnp.int32, (tm, LANES), 1)
    lo_half = lane < HEAD_DIM

    def qk_norm_rope(z, gain):
        sq = z * z
        hi = sq.astype(BF16)
        lo = (sq - hi.astype(F32)).astype(BF16)
        ssq = jnp.dot(jnp.concatenate([hi, lo], axis=1), bd, preferred_element_type=F32)
        zn = z * lax.rsqrt(ssq * (1.0 / HEAD_DIM) + EPS) * gain
        return (zn * cos
                + pltpu.roll(zn, LANES - ROPE_QUARTER, 1) * sin_up
                + pltpu.roll(zn, ROPE_QUARTER, 1) * sin_dn)

    def put_k(ref, z):
        swapped = pltpu.roll(z, HEAD_DIM, 1)
        ref[0, 0] = jnp.where(lo_half, z, 0.0).astype(BF16)
        ref[0, 1] = jnp.where(lo_half, 0.0, swapped).astype(BF16)
        ref[1, 0] = jnp.where(lo_half, swapped, 0.0).astype(BF16)
        ref[1, 1] = jnp.where(lo_half, 0.0, z).astype(BF16)

    for s in range(N_PAIRS):
        q = qk_norm_rope(proj_a[:, s * LANES:(s + 1) * LANES], qkg_ref[0])
        qa_ref[s] = (q * Q_SCALE).astype(BF16)
    put_k(ka_ref, qk_norm_rope(proj_a[:, Q_WIDTH:Q_WIDTH + KV_WIDTH], qkg_ref[1]))

    def vt_with_ones(v):
        vt = v.T.astype(BF16)
        ones = jnp.ones((VT_ROWS - HEAD_DIM, tm), BF16)
        return jnp.concatenate([vt[:HEAD_DIM], ones, vt[HEAD_DIM:], ones], axis=0)

    vat_ref[...] = vt_with_ones(proj_a[:, Q_WIDTH + KV_WIDTH:]).reshape(N_KV, VT_ROWS, tm)
    for s in range(N_PAIRS):
        qb_ref[s] = (proj_b[:, s * LANES:(s + 1) * LANES] * Q_SCALE).astype(BF16)
    put_k(kb_ref, proj_b[:, Q_WIDTH:Q_WIDTH + KV_WIDTH])
    vbt = vt_with_ones(proj_b[:, Q_WIDTH + KV_WIDTH:])
    for j in range(tm // BLOCK):
        vbt_ref[j] = vbt[:, j * BLOCK:(j + 1) * BLOCK]


def _pre_call(x2d, seq, g_pre, w_in, qk_gain, rope, bd):
    n_tok = x2d.shape[0]
    tm = TM_PRE
    tiles_per_seq = seq // tm
    q_spec = pl.BlockSpec((N_PAIRS, tm, LANES), lambda i: (0, i, 0))
    k_spec = pl.BlockSpec((N_KV, 2, tm, LANES), lambda i: (0, 0, i, 0))
    q_shape = jax.ShapeDtypeStruct((N_PAIRS, n_tok, LANES), BF16)
    k_shape = jax.ShapeDtypeStruct((N_KV, 2, n_tok, LANES), BF16)
    return pl.pallas_call(
        _pre_kernel,
        grid=(n_tok // tm,),
        in_specs=[
            pl.BlockSpec((tm, D_MODEL), lambda i: (i, 0)),
            _const_spec((1, D_MODEL)),
            _const_spec((D_MODEL, IN_WIDTH)),
            _const_spec((2, 1, LANES)),
            pl.BlockSpec((3, tm, LANES), lambda i: (0, i % tiles_per_seq, 0)),
            _const_spec((2 * LANES, LANES)),
        ],
        out_specs=[
            q_spec, k_spec,
            pl.BlockSpec((N_KV, VT_ROWS, tm), lambda i: (0, 0, i)),
            q_spec, k_spec,
            pl.BlockSpec((tm // BLOCK, N_KV * VT_ROWS, BLOCK), lambda i: (i, 0, 0)),
        ],
        out_shape=[
            q_shape, k_shape,
            jax.ShapeDtypeStruct((N_KV, VT_ROWS, n_tok), BF16),
            q_shape, k_shape,
            jax.ShapeDtypeStruct((n_tok // BLOCK, N_KV * VT_ROWS, BLOCK), BF16),
        ],
        compiler_params=pltpu.CompilerParams(
            dimension_semantics=("arbitrary",), vmem_limit_bytes=VMEM_LIMIT),
        name="pre",
    )(x2d, g_pre, w_in, qk_gain, rope, bd)


def _fold8(x, op):
    return op(x.reshape(x.shape[0] // SUBLANES, SUBLANES, x.shape[1]), axis=0)


def _softmax_units(n_units, group, n_keys, tk_a, tk_b, score_chunk, value_chunk, s_scr, sink_of, emit):
    assert tk_a % tk_b == 0 and n_keys % tk_a == 0 and n_units % 2 == 0
    n_a, b_per_a = n_keys // tk_a, tk_a // tk_b

    def sinks(u):
        vals = [sink_of(u, g) for g in range(group)]
        return None if vals[0] is None else vals

    def sweeps(ua, s_write, ub, m_b, s_read):
        mx = None
        acc = [None] * group
        for ca in range(n_a):
            if ua is not None:
                st = jnp.concatenate([score_chunk(ua, g, ca * tk_a, tk_a) for g in range(group)], axis=1)
                s_write[ca * tk_a:(ca + 1) * tk_a, :st.shape[1]] = st
                cm = _fold8(st, jnp.max)
                mx = cm if mx is None else jnp.maximum(mx, cm)
            if ub is not None:
                for cb in range(ca * b_per_a, (ca + 1) * b_per_a):
                    p = jnp.exp2(s_read[cb * tk_b:(cb + 1) * tk_b, :m_b.shape[1]] - m_b).astype(BF16)
                    vt = value_chunk(ub, cb * tk_b, tk_b)
                    tq = p.shape[1] // group
                    for g in range(group):
                        pv = jnp.dot(vt, p[:, g * tq:(g + 1) * tq], preferred_element_type=F32)
                        acc[g] = pv if acc[g] is None else acc[g] + pv
        if ub is not None:
            sink = sinks(ub)
            for g in range(group):
                denom = acc[g][HEAD_DIM:HEAD_DIM + 1]
                if sink is not None:
                    tq = denom.shape[1]
                    denom = denom + jnp.exp2(sink[g] - m_b[:, g * tq:(g + 1) * tq])
                emit(ub, g, acc[g][:HEAD_DIM] / denom)
        if ua is None:
            return None
        m = jnp.max(mx, axis=0, keepdims=True)
        sink = sinks(ua)
        if sink is not None:
            tq = m.shape[1] // group
            m = jnp.concatenate([jnp.maximum(m[:, g * tq:(g + 1) * tq], sink[g]) for g in range(group)],
                                axis=1)
        return m

    def step(t, m):
        u = 2 * t
        m = sweeps(u + 1, s_scr[1], u, m, s_scr[0])
        return sweeps(u + 2, s_scr[0], u + 1, m, s_scr[1])

    trips = n_units // 2 - 1 + jnp.minimum(pl.program_id(0), 0)
    m = lax.fori_loop(0, trips, step, sweeps(0, s_scr[0], None, None, None))
    m = sweeps(n_units - 1, s_scr[1], n_units - 2, m, s_scr[0])
    sweeps(None, None, n_units - 1, m, s_scr[1])


def _gattn_kernel(group, q_ref, k_ref, vt_ref, o_ref, s_even, s_odd, ot_scr):
    seq = k_ref.shape[2]
    units_per_kv = N_HEADS // N_KV // group

    def score_chunk(u, g, row0, tk):
        head = u * group + g
        k = k_ref[u // units_per_kv, head % 2, row0:row0 + tk, :]
        return lax.dot_general(k, q_ref[head // 2], NT, preferred_element_type=F32)

    def value_chunk(u, row0, tk):
        return vt_ref[u // units_per_kv, :, row0:row0 + tk]

    def emit(u, g, out):
        ot_scr[u * group + g] = out

    _softmax_units(N_HEADS // group, group, seq, TK_SCORE, TK_VALUE, score_chunk, value_chunk,
                   (s_even, s_odd), lambda u, g: None, emit)
    for pair in range(N_PAIRS):
        both = jnp.concatenate([ot_scr[2 * pair], ot_scr[2 * pair + 1]], axis=0)
        o_ref[:, pair * LANES:(pair + 1) * LANES] = both.T.astype(BF16)


def _gattn_call(q, k, vt, batch, seq):
    tq = TQ_GLOBAL
    nq = seq // tq
    n_tok = batch * seq
    group = GLOBAL_HEADS_PER_UNIT
    return pl.pallas_call(
        functools.partial(_gattn_kernel, group),
        grid=(batch, nq),
        in_specs=[
            pl.BlockSpec((N_PAIRS, tq, LANES), lambda b, i: (0, b * nq + i, 0)),
            pl.BlockSpec((N_KV, 2, seq, LANES), lambda b, i: (0, 0, b, 0)),
            pl.BlockSpec((N_KV, VT_ROWS, seq), lambda b, i: (0, 0, b)),
        ],
        out_specs=pl.BlockSpec((tq, Q_WIDTH), lambda b, i: (b * nq + i, 0)),
        out_shape=jax.ShapeDtypeStruct((n_tok, Q_WIDTH), BF16),
        scratch_shapes=[pltpu.VMEM((seq, group * tq + SCORE_PAD_LANES), F32),
                        pltpu.VMEM((seq, group * tq + SCORE_PAD_LANES), F32),
                        pltpu.VMEM((N_HEADS, HEAD_DIM, tq), F32)],
        compiler_params=pltpu.CompilerParams(
            dimension_semantics=("arbitrary", "arbitrary"),
            vmem_limit_bytes=VMEM_LIMIT),
        name="gattn",
    )(q, k, vt)


def _t5_bucket_table():
    half = N_BUCKETS // 2
    max_exact = half // 2
    rel = np.arange(WK)[:, None] - WINDOW - np.arange(WQ)[None, :]
    n = np.abs(rel)
    assert MAX_DISTANCE // max_exact == 16 and half - max_exact == 8
    large = np.zeros_like(n)
    for kk in range(1, 2 * half):
        large += ((n * n) >= (2 ** kk) * max_exact * max_exact)
    large = np.minimum(max_exact + large, half - 1)
    bucket = np.where(rel > 0, half, 0) + np.where(n < max_exact, n, large)
    return bucket.astype(np.int32)


def _wattn_kernel(seq, q_ref, k_ref, vt_ref, bkt_ref, rb_ref, sink_ref, o_ref,
                  kpad, vtpad, bias_edge, bias_mid, s_even, s_odd, ot_scr):
    b = pl.program_id(0)
    i = pl.program_id(1)
    n_q_step = q_ref.shape[1] // WQ
    n_q_seq = seq // WQ
    n_blk_seq = seq // BLOCK
    edge = WINDOW

    @pl.when((b == 0) & (i == 0))
    def _():
        bucket = bkt_ref[...]
        row = lax.broadcasted_iota(jnp.int32, bucket.shape, 0)
        col = lax.broadcasted_iota(jnp.int32, bucket.shape, 1)
        band = jnp.abs(row - WINDOW - col) <= WINDOW
        neg = jnp.full((edge, WQ), NEG_INF, F32)
        bias_edge[0, N_HEADS] = neg
        bias_edge[1, N_HEADS] = neg
        for head in range(N_HEADS):
            bias = jnp.zeros(bucket.shape, F32)
            for bk in range(N_BUCKETS):
                bias = jnp.where(bucket == bk, rb_ref[bk, head], bias)
            bias = jnp.where(band, bias * LOG2E, NEG_INF)
            bias_edge[0, head] = bias[:edge]
            bias_mid[head] = bias[edge:WK - edge]
            bias_edge[1, head] = bias[WK - edge:]

    @pl.when(i == 0)
    def _():
        zk = jnp.zeros((WINDOW, LANES), BF16)
        for kv in range(N_KV):
            for half in range(2):
                kpad[kv, half, 0:WINDOW, :] = zk
                kpad[kv, half, WINDOW:WINDOW + seq, :] = k_ref[kv, half]
                kpad[kv, half, WINDOW + seq:2 * WINDOW + seq, :] = zk
        zv = jnp.zeros((N_KV, VT_ROWS, BLOCK), BF16)
        vtpad[0] = zv
        vtpad[pl.ds(1, n_blk_seq)] = vt_ref[...].reshape(n_blk_seq, N_KV, VT_ROWS, BLOCK)
        vtpad[n_blk_seq + 1] = zv

    group = N_HEADS // N_KV

    def score_chunk(u, g, row0, tk):
        assert row0 == 0 and tk == WK
        j, kv = u // N_KV, u % N_KV
        head = kv * group + g
        qb = i * n_q_step + j
        q = q_ref[head // 2, pl.ds(pl.multiple_of(j * WQ, WQ), WQ), :]
        st = lax.dot_general(kpad[kv, g % 2, pl.ds(pl.multiple_of(qb * WQ, WQ), WK), :], q,
                             NT, preferred_element_type=F32)
        bias = jnp.concatenate([
            bias_edge[0, jnp.where(qb == 0, N_HEADS, head)],
            bias_mid[head],
            bias_edge[1, jnp.where(qb == n_q_seq - 1, N_HEADS, head)]], axis=0)
        return st + bias

    def value_chunk(u, row0, tk):
        j, kv = u // N_KV, u % N_KV
        blk0 = (i * n_q_step + j) * (WQ // BLOCK) + row0 // BLOCK
        return jnp.concatenate([vtpad[blk0 + t, kv] for t in range(tk // BLOCK)], axis=1)

    def emit(u, g, out):
        ot_scr[(u // N_KV) * N_HEADS + (u % N_KV) * group + g] = out

    _softmax_units(n_q_step * N_KV, group, WK, WK, TK_VALUE, score_chunk, value_chunk, (s_even, s_odd),
                   lambda u, g: sink_ref[(u % N_KV) * group + g] * LOG2E, emit)
    for j in range(n_q_step):
        for pair in range(N_PAIRS):
            u = j * N_HEADS + 2 * pair
            both = jnp.concatenate([ot_scr[u], ot_scr[u + 1]], axis=0)
            o_ref[j * WQ:(j + 1) * WQ, pair * LANES:(pair + 1) * LANES] = both.T.astype(BF16)


def _wattn_call(q, k, vt3, batch, seq, bucket, rel_bias, sink):
    tq = TQ_WINDOW
    nq = seq // tq
    n_tok = batch * seq
    n_blk_seq = seq // BLOCK
    return pl.pallas_call(
        functools.partial(_wattn_kernel, seq),
        grid=(batch, nq),
        in_specs=[
            pl.BlockSpec((N_PAIRS, tq, LANES), lambda b, i: (0, b * nq + i, 0)),
            pl.BlockSpec((N_KV, 2, seq, LANES), lambda b, i: (0, 0, b, 0)),
            pl.BlockSpec((n_blk_seq, N_KV * VT_ROWS, BLOCK), lambda b, i: (b, 0, 0)),
            _const_spec((WK, WQ)),
            pl.BlockSpec(memory_space=pltpu.SMEM),
            pl.BlockSpec(memory_space=pltpu.SMEM),
        ],
        out_specs=pl.BlockSpec((tq, Q_WIDTH), lambda b, i: (b * nq + i, 0)),
        out_shape=jax.ShapeDtypeStruct((n_tok, Q_WIDTH), BF16),
        scratch_shapes=[
            pltpu.VMEM((N_KV, 2, seq + 2 * WINDOW, LANES), BF16),
            pltpu.VMEM((n_blk_seq + 2, N_KV, VT_ROWS, BLOCK), BF16),
            pltpu.VMEM((2, N_HEADS + 1, WINDOW, WQ), F32),
            pltpu.VMEM((N_HEADS, WK - 2 * WINDOW, WQ), F32),
            pltpu.VMEM((WK, N_HEADS // N_KV * WQ + SCORE_PAD_LANES), F32),
            pltpu.VMEM((WK, N_HEADS // N_KV * WQ + SCORE_PAD_LANES), F32),
            pltpu.VMEM((tq // WQ * N_HEADS, HEAD_DIM, WQ), F32),
        ],
        compiler_params=pltpu.CompilerParams(
            dimension_semantics=("arbitrary", "arbitrary"),
            vmem_limit_bytes=VMEM_LIMIT),
        name="wattn",
    )(q, k, vt3, bucket, rel_bias, sink)


def _mix_kernel(x_ref, ya_ref, yb_ref, gpre_ref, wg_ref, bg_ref, wa_ref, wb_ref, wo_ref, gpost_ref, o_ref):
    x = x_ref[...]
    h = _rmsnorm(x, gpre_ref[...]).astype(BF16)
    z = jnp.dot(h, wg_ref[...], preferred_element_type=F32) + bg_ref[...]
    gates = 1.0 / (1.0 + jnp.exp(-z))
    a = jnp.dot(ya_ref[...], wa_ref[...], preferred_element_type=F32)
    b = jnp.dot(yb_ref[...], wb_ref[...], preferred_element_type=F32)
    mix = gates[:, :D_MODEL] * a + gates[:, D_MODEL:] * b
    o = jnp.dot(mix.astype(BF16), wo_ref[...], preferred_element_type=F32)
    o_ref[...] = x + _rmsnorm(o, gpost_ref[...])


def _mix_call(x2d, ya, yb, g_pre, w_gate, b_gate, w_a, w_b, w_out, g_post):
    n_tok = x2d.shape[0]
    tm = TM_MIX
    return pl.pallas_call(
        _mix_kernel,
        grid=(n_tok // tm,),
        in_specs=[
            pl.BlockSpec((tm, D_MODEL), lambda i: (i, 0)),
            pl.BlockSpec((tm, Q_WIDTH), lambda i: (i, 0)),
            pl.BlockSpec((tm, Q_WIDTH), lambda i: (i, 0)),
            _const_spec((1, D_MODEL)),
            _const_spec((D_MODEL, 2 * D_MODEL)),
            _const_spec((1, 2 * D_MODEL)),
            _const_spec((Q_WIDTH, D_MODEL)),
            _const_spec((Q_WIDTH, D_MODEL)),
            _const_spec((D_MODEL, D_MODEL)),
            _const_spec((1, D_MODEL)),
        ],
        out_specs=pl.BlockSpec((tm, D_MODEL), lambda i: (i, 0)),
        out_shape=jax.ShapeDtypeStruct((n_tok, D_MODEL), F32),
        compiler_params=pltpu.CompilerParams(
            dimension_semantics=("arbitrary",), vmem_limit_bytes=VMEM_LIMIT),
        name="mix",
    )(x2d, ya, yb, g_pre, w_gate, b_gate, w_a, w_b, w_out, g_post)


def _ffn_kernel(x_ref, gpre_ref, wg_ref, wu_ref, wd_ref, gpost_ref, o_ref):
    x = x_ref[...]
    h = _rmsnorm(x, gpre_ref[...]).astype(BF16)
    g = jnp.dot(h, wg_ref[...], preferred_element_type=F32)
    u = jnp.dot(h, wu_ref[...], preferred_element_type=F32)
    act = (g / (1.0 + jnp.exp(-g))) * u
    f = jnp.dot(act.astype(BF16), wd_ref[...], preferred_element_type=F32)
    o_ref[...] = x + _rmsnorm(f, gpost_ref[...])


def _ffn_call(x2d, g_pre, w_g, w_u, w_d, g_post):
    n_tok = x2d.shape[0]
    tm = TM_FFN
    return pl.pallas_call(
        _ffn_kernel,
        grid=(n_tok // tm,),
        in_specs=[
            pl.BlockSpec((tm, D_MODEL), lambda i: (i, 0)),
            _const_spec((1, D_MODEL)),
            _const_spec((D_MODEL, D_FF)),
            _const_spec((D_MODEL, D_FF)),
            _const_spec((D_FF, D_MODEL)),
            _const_spec((1, D_MODEL)),
        ],
        out_specs=pl.BlockSpec((tm, D_MODEL), lambda i: (i, 0)),
        out_shape=jax.ShapeDtypeStruct((n_tok, D_MODEL), F32),
        compiler_params=pltpu.CompilerParams(
            dimension_semantics=("arbitrary",), vmem_limit_bytes=VMEM_LIMIT),
        name="ffn",
    )(x2d, g_pre, w_g, w_u, w_d, g_post)


def _rope_tables(seq):
    pos = np.arange(seq)
    rows = (pos // GRID_W).astype(np.float32)
    cols = (pos % GRID_W).astype(np.float32)
    inv_freq = (np.float32(1.0) / np.power(
        np.float32(ROPE_THETA), np.arange(0, ROPE_HALF, 2, dtype=np.float32) / np.float32(ROPE_HALF)))
    lane = np.arange(LANES)
    d = lane % HEAD_DIM
    use_col = (d // ROPE_HALF) == 1
    j = d % ROPE_HALF
    f_idx = j % ROPE_QUARTER
    coord = np.where(use_col[None, :], cols[:, None], rows[:, None])
    ang = (coord * inv_freq.astype(np.float32)[f_idx][None, :]).astype(np.float32).astype(np.float64)
    cos, sin = np.cos(ang), np.sin(ang)
    first = (j < ROPE_QUARTER)[None, :]
    sin_up = np.where(first, -sin, 0.0)
    sin_dn = np.where(first, 0.0, sin)
    return jnp.asarray(np.stack([cos, sin_up, sin_dn]).astype(np.float32))


def _layer(x, bucket, bd, p):
    batch, seq, _ = x.shape
    assert seq % TM_PRE == 0 and seq % TQ_GLOBAL == 0 and seq % TQ_WINDOW == 0 and seq // WQ >= 2
    x2d = x.reshape(batch * seq, D_MODEL)
    qa, ka, vat, qb, kb, vbt = _pre_call(x2d, seq, p["g_mix_pre"], p["w_in"], p["qk_gain"],
                                         _rope_tables(seq), bd)
    ya = _gattn_call(qa, ka, vat, batch, seq)
    yb = _wattn_call(qb, kb, vbt, batch, seq, bucket, p["rel_bias"], p["sink"])
    x1 = _mix_call(x2d, ya, yb, p["g_mix_pre"], p["w_gate"], p["b_gate"], p["w_a"], p["w_b"], p["w_out"],
                   p["g_mix_post"])
    y = _ffn_call(x1, p["g_ffn_pre"], p["w_ffn_gate"], p["w_ffn_up"], p["w_ffn_down"], p["g_ffn_post"])
    return y.reshape(batch, seq, D_MODEL)


def kernel(x_prompt, x_sample, norm_mix_pre, norm_mix_post, w_in, q_norm_a, k_norm_a, sink_b, rel_bias,
           w_branch_a, w_branch_b, w_gate, b_gate, w_out, norm_ffn_pre, norm_ffn_post, w_ffn_gate,
           w_ffn_up, w_ffn_down):
    depth = w_in.shape[0]
    bucket = jnp.asarray(_t5_bucket_table())
    blk = np.arange(LANES) // HEAD_DIM
    bd = jnp.asarray(np.tile(blk[:, None] == blk[None, :], (2, 1)), dtype=BF16)
    y_prompt, y_sample = x_prompt, x_sample
    for l in range(depth):
        p = {
            "g_mix_pre": norm_mix_pre[l].reshape(1, D_MODEL),
            "g_mix_post": norm_mix_post[l].reshape(1, D_MODEL),
            "w_in": w_in[l].astype(BF16),
            "qk_gain": jnp.stack([jnp.tile(q_norm_a[l], 2), jnp.tile(k_norm_a[l], 2)]).reshape(2, 1, LANES),
            "sink": sink_b[l],
            "rel_bias": rel_bias,
            "w_a": w_branch_a[l].astype(BF16),
            "w_b": w_branch_b[l].astype(BF16),
            "w_gate": w_gate[l].astype(BF16),
            "b_gate": b_gate[l].reshape(1, 2 * D_MODEL),
            "w_out": w_out[l].astype(BF16),
            "g_ffn_pre": norm_ffn_pre[l].reshape(1, D_MODEL),
            "g_ffn_post": norm_ffn_post[l].reshape(1, D_MODEL),
            "w_ffn_gate": w_ffn_gate[l].astype(BF16),
            "w_ffn_up": w_ffn_up[l].astype(BF16),
            "w_ffn_down": w_ffn_down[l].astype(BF16),
        }
        y_prompt = _layer(y_prompt, bucket, bd, p)
        y_sample = _layer(y_sample, bucket, bd, p)
    return (y_prompt, y_sample)
```

```python
import functools
import math

import numpy as np
import jax
import jax.numpy as jnp
from jax import lax
from jax.experimental import pallas as pl
from jax.experimental.pallas import tpu as pltpu

F32 = jnp.float32
BF16 = jnp.bfloat16

D_MODEL = 1024
HEAD_DIM = 64
N_HEADS = 8
N_KV = 2
Q_WIDTH = N_HEADS * HEAD_DIM
KV_WIDTH = N_KV * HEAD_DIM
IN_WIDTH = 2 * (Q_WIDTH + 2 * KV_WIDTH)
BLOCK = 128
WINDOW = 128
GRID_W = 64
ROPE_THETA = 10000.0
ROPE_HALF = HEAD_DIM // 2
ROPE_QUARTER = ROPE_HALF // 2
N_BUCKETS = 32
MAX_DISTANCE = 128
D_FF = 2816
EPS = 1e-6
NEG_INF = -1e30
LOG2E = 1.4426950408889634
Q_SCALE = LOG2E / math.sqrt(HEAD_DIM)

LANES = 128
SUBLANES = 8
N_PAIRS = N_HEADS // 2
PAIRS_PER_KV = N_PAIRS // N_KV


TM_PRE = 512
TM_MIX = 512
TM_FFN = 512
TQ_GLOBAL = 512
TK_SCORE = 512
TK_VALUE = 256
BF16_SUBLANES = 16
VT_ROWS = HEAD_DIM + BF16_SUBLANES
WQ = 256
WK = WQ + 2 * WINDOW
TQ_WINDOW = 1024
VMEM_LIMIT = 56 * 1024 * 1024
GLOBAL_HEADS_PER_UNIT = 1

NT = (((1,), (1,)), ((), ()))


def _rmsnorm(x, gain):
    var = jnp.mean(x * x, axis=-1, keepdims=True)
    return x * lax.rsqrt(var + EPS) * gain


def _const_spec(shape):
    zeros = (0,) * len(shape)
    return pl.BlockSpec(shape, lambda *_: zeros, pipeline_mode=pl.Buffered(1))


def _pre_kernel(n_tiles, x_ref, g_ref, w_ref, qkg_ref, rope_ref, bd_ref,
                qa_ref, ka_ref, vat_ref, qb_ref, kb_ref, vbt_ref, proj_even, proj_odd):
    i = pl.program_id(0)
    tm = x_ref.shape[0]

    def project(proj_ref):
        h = _rmsnorm(x_ref[...], g_ref[...]).astype(BF16)
        proj_ref[...] = jnp.dot(h, w_ref[...], preferred_element_type=F32)

    def finish(proj_ref):
        bd = bd_ref[...]
        cos, sin_up, sin_dn = rope_ref[0], rope_ref[1], rope_ref[2]
        lane = lax.broadcasted_iota(jnp.int32, (tm, LANES), 1)
        lo_half = lane < HEAD_DIM

        def slab(col):
            return proj_ref[:, col * LANES:(col + 1) * LANES]

        def qk_norm_rope(z, gain):
            sq = z * z
            hi = sq.astype(BF16)
            lo = (sq - hi.astype(F32)).astype(BF16)
            ssq = jnp.dot(jnp.concatenate([hi, lo], axis=1), bd, preferred_element_type=F32)
            zn = z * lax.rsqrt(ssq * (1.0 / HEAD_DIM) + EPS) * gain
            return (zn * cos
                    + pltpu.roll(zn, LANES - ROPE_QUARTER, 1) * sin_up
                    + pltpu.roll(zn, ROPE_QUARTER, 1) * sin_dn)

        def put_k(ref, z):
            swapped = pltpu.roll(z, HEAD_DIM, 1)
            ref[0, 0] = jnp.where(lo_half, z, 0.0).astype(BF16)
            ref[0, 1] = jnp.where(lo_half, 0.0, swapped).astype(BF16)
            ref[1, 0] = jnp.where(lo_half, swapped, 0.0).astype(BF16)
            ref[1, 1] = jnp.where(lo_half, 0.0, z).astype(BF16)

        def vt_with_ones(v):
            vt = v.T.astype(BF16)
            ones = jnp.ones((VT_ROWS - HEAD_DIM, tm), BF16)
            return jnp.concatenate([vt[:HEAD_DIM], ones, vt[HEAD_DIM:], ones], axis=0)

        for s in range(N_PAIRS):
            qa_ref[s] = (qk_norm_rope(slab(s), qkg_ref[0]) * Q_SCALE).astype(BF16)
        put_k(ka_ref, qk_norm_rope(slab(N_PAIRS), qkg_ref[1]))
        vat_ref[...] = vt_with_ones(slab(N_PAIRS + 1)).reshape(N_KV, VT_ROWS, tm)
        base = N_PAIRS + 2
        for s in range(N_PAIRS):
            qb_ref[s] = (slab(base + s) * Q_SCALE).astype(BF16)
        put_k(kb_ref, slab(base + N_PAIRS))
        vbt = vt_with_ones(slab(base + N_PAIRS + 1))
        for j in range(tm // BLOCK):
            vbt_ref[j] = vbt[:, j * BLOCK:(j + 1) * BLOCK]

    steady = (i > 0) & (i < n_tiles)

    @pl.when(i == 0)
    def _():
        project(proj_even)

    @pl.when(steady & (i % 2 == 1))
    def _():
        finish(proj_even)
        project(proj_odd)

    @pl.when(steady & (i % 2 == 0))
    def _():
        finish(proj_odd)
        project(proj_even)

    @pl.when(i == n_tiles)
    def _():
        finish(proj_even if (n_tiles - 1) % 2 == 0 else proj_odd)


def _pre_call(x2d, seq, g_pre, w_in, qk_gain, rope, bd):
    n_tok = x2d.shape[0]
    tm = TM_PRE
    n_tiles = n_tok // tm
    tiles_per_seq = seq // tm

    def done(i):
        return jnp.maximum(i - 1, 0)

    q_spec = pl.BlockSpec((N_PAIRS, tm, LANES), lambda i: (0, done(i), 0))
    k_spec = pl.BlockSpec((N_KV, 2, tm, LANES), lambda i: (0, 0, done(i), 0))
    q_shape = jax.ShapeDtypeStruct((N_PAIRS, n_tok, LANES), BF16)
    k_shape = jax.ShapeDtypeStruct((N_KV, 2, n_tok, LANES), BF16)
    return pl.pallas_call(
        functools.partial(_pre_kernel, n_tiles),
        grid=(n_tiles + 1,),
        in_specs=[
            pl.BlockSpec((tm, D_MODEL), lambda i: (jnp.minimum(i, n_tiles - 1), 0)),
            _const_spec((1, D_MODEL)),
            _const_spec((D_MODEL, IN_WIDTH)),
            _const_spec((2, 1, LANES)),
            pl.BlockSpec((3, tm, LANES), lambda i: (0, done(i) % tiles_per_seq, 0)),
            _const_spec((2 * LANES, LANES)),
        ],
        out_specs=[
            q_spec, k_spec,
            pl.BlockSpec((N_KV, VT_ROWS, tm), lambda i: (0, 0, done(i))),
            q_spec, k_spec,
            pl.BlockSpec((tm // BLOCK, N_KV * VT_ROWS, BLOCK), lambda i: (done(i), 0, 0)),
        ],
        out_shape=[
            q_shape, k_shape,
            jax.ShapeDtypeStruct((N_KV, VT_ROWS, n_tok), BF16),
            q_shape, k_shape,
            jax.ShapeDtypeStruct((n_tok // BLOCK, N_KV * VT_ROWS, BLOCK), BF16),
        ],
        scratch_shapes=[pltpu.VMEM((tm, IN_WIDTH), F32), pltpu.VMEM((tm, IN_WIDTH), F32)],
        compiler_params=pltpu.CompilerParams(
            dimension_semantics=("arbitrary",), vmem_limit_bytes=VMEM_LIMIT),
        name="pre",
    )(x2d, g_pre, w_in, qk_gain, rope, bd)


def _fold8(x, op):
    return op(x.reshape(x.shape[0] // SUBLANES, SUBLANES, x.shape[1]), axis=0)


def _softmax_units(n_units, group, n_keys, tk_a, tk_b, score_chunk, value_chunk, s_scr, sink_of, emit):
    assert tk_a % tk_b == 0 and n_keys % tk_a == 0 and n_units % 2 == 0
    n_a, b_per_a = n_keys // tk_a, tk_a // tk_b

    def sinks(u):
        vals = [sink_of(u, g) for g in range(group)]
        return None if vals[0] is None else vals

    def sweeps(ua, s_write, ub, m_b, s_read):
        mx = None
        acc = [None] * group
        for ca in range(n_a):
            if ua is not None:
                st = jnp.concatenate([score_chunk(ua, g, ca * tk_a, tk_a) for g in range(group)], axis=1)
                s_write[ca * tk_a:(ca + 1) * tk_a, :] = st
                cm = _fold8(st, jnp.max)
                mx = cm if mx is None else jnp.maximum(mx, cm)
            if ub is not None:
                for cb in range(ca * b_per_a, (ca + 1) * b_per_a):
                    p = jnp.exp2(s_read[cb * tk_b:(cb + 1) * tk_b, :] - m_b).astype(BF16)
                    vt = value_chunk(ub, cb * tk_b, tk_b)
                    tq = p.shape[1] // group
                    for g in range(group):
                        pv = jnp.dot(vt, p[:, g * tq:(g + 1) * tq], preferred_element_type=F32)
                        acc[g] = pv if acc[g] is None else acc[g] + pv
        if ub is not None:
            sink = sinks(ub)
            for g in range(group):
                denom = acc[g][HEAD_DIM:HEAD_DIM + 1]
                if sink is not None:
                    tq = denom.shape[1]
                    denom = denom + jnp.exp2(sink[g] - m_b[:, g * tq:(g + 1) * tq])
                emit(ub, g, acc[g][:HEAD_DIM] / denom)
        if ua is None:
            return None
        m = jnp.max(mx, axis=0, keepdims=True)
        sink = sinks(ua)
        if sink is not None:
            tq = m.shape[1] // group
            m = jnp.concatenate([jnp.maximum(m[:, g * tq:(g + 1) * tq], sink[g]) for g in range(group)],
                                axis=1)
        return m

    def step(t, m):
        u = 2 * t
        m = sweeps(u + 1, s_scr[1], u, m, s_scr[0])
        return sweeps(u + 2, s_scr[0], u + 1, m, s_scr[1])

    trips = n_units // 2 - 1 + jnp.minimum(pl.program_id(0), 0)
    m = lax.fori_loop(0, trips, step, sweeps(0, s_scr[0], None, None, None))
    m = sweeps(n_units - 1, s_scr[1], n_units - 2, m, s_scr[0])
    sweeps(None, None, n_units - 1, m, s_scr[1])


def _gattn_kernel(group, q_ref, k_ref, vt_ref, o_ref, s_even, s_odd, ot_scr):
    seq = k_ref.shape[2]
    units_per_kv = N_HEADS // N_KV // group

    def score_chunk(u, g, row0, tk):
        head = u * group + g
        k = k_ref[u // units_per_kv, head % 2, row0:row0 + tk, :]
        return lax.dot_general(k, q_ref[head // 2], NT, preferred_element_type=F32)

    def value_chunk(u, row0, tk):
        return vt_ref[u // units_per_kv, :, row0:row0 + tk]

    def emit(u, g, out):
        ot_scr[u * group + g] = out

    _softmax_units(N_HEADS // group, group, seq, TK_SCORE, TK_VALUE, score_chunk, value_chunk,
                   (s_even, s_odd), lambda u, g: None, emit)
    for pair in range(N_PAIRS):
        both = jnp.concatenate([ot_scr[2 * pair], ot_scr[2 * pair + 1]], axis=0)
        o_ref[:, pair * LANES:(pair + 1) * LANES] = both.T.astype(BF16)


def _gattn_call(q, k, vt, batch, seq):
    tq = TQ_GLOBAL
    nq = seq // tq
    n_tok = batch * seq
    group = GLOBAL_HEADS_PER_UNIT
    return pl.pallas_call(
        functools.partial(_gattn_kernel, group),
        grid=(batch, nq),
        in_specs=[
            pl.BlockSpec((N_PAIRS, tq, LANES), lambda b, i: (0, b * nq + i, 0)),
            pl.BlockSpec((N_KV, 2, seq, LANES), lambda b, i: (0, 0, b, 0)),
            pl.BlockSpec((N_KV, VT_ROWS, seq), lambda b, i: (0, 0, b)),
        ],
        out_specs=pl.BlockSpec((tq, Q_WIDTH), lambda b, i: (b * nq + i, 0)),
        out_shape=jax.ShapeDtypeStruct((n_tok, Q_WIDTH), BF16),
        scratch_shapes=[pltpu.VMEM((seq, group * tq), F32), pltpu.VMEM((seq, group * tq), F32),
                        pltpu.VMEM((N_HEADS, HEAD_DIM, tq), F32)],
        compiler_params=pltpu.CompilerParams(
            dimension_semantics=("arbitrary", "arbitrary"),
            vmem_limit_bytes=VMEM_LIMIT),
        name="gattn",
    )(q, k, vt)


def _t5_bucket_table():
    half = N_BUCKETS // 2
    max_exact = half // 2
    rel = np.arange(WK)[:, None] - WINDOW - np.arange(WQ)[None, :]
    n = np.abs(rel)
    assert MAX_DISTANCE // max_exact == 16 and half - max_exact == 8
    large = np.zeros_like(n)
    for kk in range(1, 2 * half):
        large += ((n * n) >= (2 ** kk) * max_exact * max_exact)
    large = np.minimum(max_exact + large, half - 1)
    bucket = np.where(rel > 0, half, 0) + np.where(n < max_exact, n, large)
    return bucket.astype(np.int32)


def _wattn_kernel(seq, q_ref, k_ref, vt_ref, bkt_ref, rb_ref, sink_ref, o_ref,
                  kpad, vtpad, bias_edge, bias_mid, s_even, s_odd, ot_scr):
    b = pl.program_id(0)
    i = pl.program_id(1)
    n_q_step = q_ref.shape[1] // WQ
    n_q_seq = seq // WQ
    n_blk_seq = seq // BLOCK
    edge = WINDOW

    @pl.when((b == 0) & (i == 0))
    def _():
        bucket = bkt_ref[...]
        row = lax.broadcasted_iota(jnp.int32, bucket.shape, 0)
        col = lax.broadcasted_iota(jnp.int32, bucket.shape, 1)
        band = jnp.abs(row - WINDOW - col) <= WINDOW
        neg = jnp.full((edge, WQ), NEG_INF, F32)
        bias_edge[0, N_HEADS] = neg
        bias_edge[1, N_HEADS] = neg
        for head in range(N_HEADS):
            bias = jnp.zeros(bucket.shape, F32)
            for bk in range(N_BUCKETS):
                bias = jnp.where(bucket == bk, rb_ref[bk, head], bias)
            bias = jnp.where(band, bias * LOG2E, NEG_INF)
            bias_edge[0, head] = bias[:edge]
            bias_mid[head] = bias[edge:WK - edge]
            bias_edge[1, head] = bias[WK - edge:]

    @pl.when(i == 0)
    def _():
        zk = jnp.zeros((WINDOW, LANES), BF16)
        for kv in range(N_KV):
            for half in range(2):
                kpad[kv, half, 0:WINDOW, :] = zk
                kpad[kv, half, WINDOW:WINDOW + seq, :] = k_ref[kv, half]
                kpad[kv, half, WINDOW + seq:2 * WINDOW + seq, :] = zk
        zv = jnp.zeros((N_KV, VT_ROWS, BLOCK), BF16)
        vtpad[0] = zv
        vtpad[pl.ds(1, n_blk_seq)] = vt_ref[...].reshape(n_blk_seq, N_KV, VT_ROWS, BLOCK)
        vtpad[n_blk_seq + 1] = zv

    group = N_HEADS // N_KV

    def score_chunk(u, g, row0, tk):
        assert row0 == 0 and tk == WK
        j, kv = u // N_KV, u % N_KV
        head = kv * group + g
        qb = i * n_q_step + j
        q = q_ref[head // 2, pl.ds(pl.multiple_of(j * WQ, WQ), WQ), :]
        st = lax.dot_general(kpad[kv, g % 2, pl.ds(pl.multiple_of(qb * WQ, WQ), WK), :], q,
                             NT, preferred_element_type=F32)
        bias = jnp.concatenate([
            bias_edge[0, jnp.where(qb == 0, N_HEADS, head)],
            bias_mid[head],
            bias_edge[1, jnp.where(qb == n_q_seq - 1, N_HEADS, head)]], axis=0)
        return st + bias

    def value_chunk(u, row0, tk):
        j, kv = u // N_KV, u % N_KV
        blk0 = (i * n_q_step + j) * (WQ // BLOCK) + row0 // BLOCK
        return jnp.concatenate([vtpad[blk0 + t, kv] for t in range(tk // BLOCK)], axis=1)

    def emit(u, g, out):
        ot_scr[(u // N_KV) * N_HEADS + (u % N_KV) * group + g] = out

    _softmax_units(n_q_step * N_KV, group, WK, WK, TK_VALUE, score_chunk, value_chunk, (s_even, s_odd),
                   lambda u, g: sink_ref[(u % N_KV) * group + g] * LOG2E, emit)
    for j in range(n_q_step):
        for pair in range(N_PAIRS):
            u = j * N_HEADS + 2 * pair
            both = jnp.concatenate([ot_scr[u], ot_scr[u + 1]], axis=0)
            o_ref[j * WQ:(j + 1) * WQ, pair * LANES:(pair + 1) * LANES] = both.T.astype(BF16)


def _wattn_call(q, k, vt3, batch, seq, bucket, rel_bias, sink):
    tq = TQ_WINDOW
    nq = seq // tq
    n_tok = batch * seq
    n_blk_seq = seq // BLOCK
    return pl.pallas_call(
        functools.partial(_wattn_kernel, seq),
        grid=(batch, nq),
        in_specs=[
            pl.BlockSpec((N_PAIRS, tq, LANES), lambda b, i: (0, b * nq + i, 0)),
            pl.BlockSpec((N_KV, 2, seq, LANES), lambda b, i: (0, 0, b, 0)),
            pl.BlockSpec((n_blk_seq, N_KV * VT_ROWS, BLOCK), lambda b, i: (b, 0, 0)),
            _const_spec((WK, WQ)),
            pl.BlockSpec(memory_space=pltpu.SMEM),
            pl.BlockSpec(memory_space=pltpu.SMEM),
        ],
        out_specs=pl.BlockSpec((tq, Q_WIDTH), lambda b, i: (b * nq + i, 0)),
        out_shape=jax.ShapeDtypeStruct((n_tok, Q_WIDTH), BF16),
        scratch_shapes=[
            pltpu.VMEM((N_KV, 2, seq + 2 * WINDOW, LANES), BF16),
            pltpu.VMEM((n_blk_seq + 2, N_KV, VT_ROWS, BLOCK), BF16),
            pltpu.VMEM((2, N_HEADS + 1, WINDOW, WQ), F32),
            pltpu.VMEM((N_HEADS, WK - 2 * WINDOW, WQ), F32),
            pltpu.VMEM((WK, N_HEADS // N_KV * WQ), F32),
            pltpu.VMEM((WK, N_HEADS // N_KV * WQ), F32),
            pltpu.VMEM((tq // WQ * N_HEADS, HEAD_DIM, WQ), F32),
        ],
        compiler_params=pltpu.CompilerParams(
            dimension_semantics=("arbitrary", "arbitrary"),
            vmem_limit_bytes=VMEM_LIMIT),
        name="wattn",
    )(q, k, vt3, bucket, rel_bias, sink)


def _mix_kernel(x_ref, ya_ref, yb_ref, gpre_ref, wg_ref, bg_ref, wa_ref, wb_ref, wo_ref, gpost_ref, o_ref):
    x = x_ref[...]
    h = _rmsnorm(x, gpre_ref[...]).astype(BF16)
    z = jnp.dot(h, wg_ref[...], preferred_element_type=F32) + bg_ref[...]
    gates = 1.0 / (1.0 + jnp.exp(-z))
    a = jnp.dot(ya_ref[...], wa_ref[...], preferred_element_type=F32)
    b = jnp.dot(yb_ref[...], wb_ref[...], preferred_element_type=F32)
    mix = gates[:, :D_MODEL] * a + gates[:, D_MODEL:] * b
    o = jnp.dot(mix.astype(BF16), wo_ref[...], preferred_element_type=F32)
    o_ref[...] = x + _rmsnorm(o, gpost_ref[...])


def _mix_call(x2d, ya, yb, g_pre, w_gate, b_gate, w_a, w_b, w_out, g_post):
    n_tok = x2d.shape[0]
    tm = TM_MIX
    return pl.pallas_call(
        _mix_kernel,
        grid=(n_tok // tm,),
        in_specs=[
            pl.BlockSpec((tm, D_MODEL), lambda i: (i, 0)),
            pl.BlockSpec((tm, Q_WIDTH), lambda i: (i, 0)),
            pl.BlockSpec((tm, Q_WIDTH), lambda i: (i, 0)),
            _const_spec((1, D_MODEL)),
            _const_spec((D_MODEL, 2 * D_MODEL)),
            _const_spec((1, 2 * D_MODEL)),
            _const_spec((Q_WIDTH, D_MODEL)),
            _const_spec((Q_WIDTH, D_MODEL)),
            _const_spec((D_MODEL, D_MODEL)),
            _const_spec((1, D_MODEL)),
        ],
        out_specs=pl.BlockSpec((tm, D_MODEL), lambda i: (i, 0)),
        out_shape=jax.ShapeDtypeStruct((n_tok, D_MODEL), F32),
        compiler_params=pltpu.CompilerParams(
            dimension_semantics=("arbitrary",), vmem_limit_bytes=VMEM_LIMIT),
        name="mix",
    )(x2d, ya, yb, g_pre, w_gate, b_gate, w_a, w_b, w_out, g_post)


def _ffn_kernel(x_ref, gpre_ref, wg_ref, wu_ref, wd_ref, gpost_ref, o_ref):
    x = x_ref[...]
    h = _rmsnorm(x, gpre_ref[...]).astype(BF16)
    g = jnp.dot(h, wg_ref[...], preferred_element_type=F32)
    u = jnp.dot(h, wu_ref[...], preferred_element_type=F32)
    act = (g / (1.0 + jnp.exp(-g))) * u
    f = jnp.dot(act.astype(BF16), wd_ref[...], preferred_element_type=F32)
    o_ref[...] = x + _rmsnorm(f, gpost_ref[...])


def _ffn_call(x2d, g_pre, w_g, w_u, w_d, g_post):
    n_tok = x2d.shape[0]
    tm = TM_FFN
    return pl.pallas_call(
        _ffn_kernel,
        grid=(n_tok // tm,),
        in_specs=[
            pl.BlockSpec((tm, D_MODEL), lambda i: (i, 0)),
            _const_spec((1, D_MODEL)),
            _const_spec((D_MODEL, D_FF)),
            _const_spec((D_MODEL, D_FF)),
            _const_spec((D_FF, D_MODEL)),
            _const_spec((1, D_MODEL)),
        ],
        out_specs=pl.BlockSpec((tm, D_MODEL), lambda i: (i, 0)),
        out_shape=jax.ShapeDtypeStruct((n_tok, D_MODEL), F32),
        compiler_params=pltpu.CompilerParams(
            dimension_semantics=("arbitrary",), vmem_limit_bytes=VMEM_LIMIT),
        name="ffn",
    )(x2d, g_pre, w_g, w_u, w_d, g_post)


def _rope_tables(seq):
    pos = np.arange(seq)
    rows = (pos // GRID_W).astype(np.float32)
    cols = (pos % GRID_W).astype(np.float32)
    inv_freq = (np.float32(1.0) / np.power(
        np.float32(ROPE_THETA), np.arange(0, ROPE_HALF, 2, dtype=np.float32) / np.float32(ROPE_HALF)))
    lane = np.arange(LANES)
    d = lane % HEAD_DIM
    use_col = (d // ROPE_HALF) == 1
    j = d % ROPE_HALF
    f_idx = j % ROPE_QUARTER
    coord = np.where(use_col[None, :], cols[:, None], rows[:, None])
    ang = (coord * inv_freq.astype(np.float32)[f_idx][None, :]).astype(np.float32).astype(np.float64)
    cos, sin = np.cos(ang), np.sin(ang)
    first = (j < ROPE_QUARTER)[None, :]
    sin_up = np.where(first, -sin, 0.0)
    sin_dn = np.where(first, 0.0, sin)
    return jnp.asarray(np.stack([cos, sin_up, sin_dn]).astype(np.float32))


def _layer(x, bucket, bd, p):
    batch, seq, _ = x.shape
    assert seq % TM_PRE == 0 and seq % TQ_GLOBAL == 0 and seq % TQ_WINDOW == 0 and seq // WQ >= 2
    x2d = x.reshape(batch * seq, D_MODEL)
    qa, ka, vat, qb, kb, vbt = _pre_call(x2d, seq, p["g_mix_pre"], p["w_in"], p["qk_gain"],
                                         _rope_tables(seq), bd)
    ya = _gattn_call(qa, ka, vat, batch, seq)
    yb = _wattn_call(qb, kb, vbt, batch, seq, bucket, p["rel_bias"], p["sink"])
    x1 = _mix_call(x2d, ya, yb, p["g_mix_pre"], p["w_gate"], p["b_gate"], p["w_a"], p["w_b"], p["w_out"],
                   p["g_mix_post"])
    y = _ffn_call(x1, p["g_ffn_pre"], p["w_ffn_gate"], p["w_ffn_up"], p["w_ffn_down"], p["g_ffn_post"])
    return y.reshape(batch, seq, D_MODEL)


def kernel(x_prompt, x_sample, norm_mix_pre, norm_mix_post, w_in, q_norm_a, k_norm_a, sink_b, rel_bias,
           w_branch_a, w_branch_b, w_gate, b_gate, w_out, norm_ffn_pre, norm_ffn_post, w_ffn_gate,
           w_ffn_up, w_ffn_down):
    depth = w_in.shape[0]
    bucket = jnp.asarray(_t5_bucket_table())
    blk = np.arange(LANES) // HEAD_DIM
    bd = jnp.asarray(np.tile(blk[:, None] == blk[None, :], (2, 1)), dtype=BF16)
    y_prompt, y_sample = x_prompt, x_sample
    for l in range(depth):
        p = {
            "g_mix_pre": norm_mix_pre[l].reshape(1, D_MODEL),
            "g_mix_post": norm_mix_post[l].reshape(1, D_MODEL),
            "w_in": w_in[l].astype(BF16),
            "qk_gain": jnp.stack([jnp.tile(q_norm_a[l], 2), jnp.tile(k_norm_a[l], 2)]).reshape(2, 1, LANES),
            "sink": sink_b[l],
            "rel_bias": rel_bias,
            "w_a": w_branch_a[l].astype(BF16),
            "w_b": w_branch_b[l].astype(BF16),
            "w_gate": w_gate[l].astype(BF16),
            "b_gate": b_gate[l].reshape(1, 2 * D_MODEL),
            "w_out": w_out[l].astype(BF16),
            "g_ffn_pre": norm_ffn_pre[l].reshape(1, D_MODEL),
            "g_ffn_post": norm_ffn_post[l].reshape(1, D_MODEL),
            "w_ffn_gate": w_ffn_gate[l].astype(BF16),
            "w_ffn_up": w_ffn_up[l].astype(BF16),
            "w_ffn_down": w_ffn_down[l].astype(BF16),
        }
        y_prompt = _layer(y_prompt, bucket, bd, p)
        y_sample = _layer(y_sample, bucket, bd, p)
    return (y_prompt, y_sample)
```

```python
import functools
import math

import numpy as np
import jax
import jax.numpy as jnp
from jax import lax
from jax.experimental import pallas as pl
from jax.experimental.pallas import tpu as pltpu

F32 = jnp.float32
BF16 = jnp.bfloat16

D_MODEL = 1024
HEAD_DIM = 64
N_HEADS = 8
N_KV = 2
Q_WIDTH = N_HEADS * HEAD_DIM
KV_WIDTH = N_KV * HEAD_DIM
IN_WIDTH = 2 * (Q_WIDTH + 2 * KV_WIDTH)
BLOCK = 128
WINDOW = 128
GRID_W = 64
ROPE_THETA = 10000.0
ROPE_HALF = HEAD_DIM // 2
ROPE_QUARTER = ROPE_HALF // 2
N_BUCKETS = 32
MAX_DISTANCE = 128
D_FF = 2816
EPS = 1e-6
NEG_INF = -1e30
LOG2E = 1.4426950408889634
Q_SCALE = LOG2E / math.sqrt(HEAD_DIM)

LANES = 128
SUBLANES = 8
N_PAIRS = N_HEADS // 2
PAIRS_PER_KV = N_PAIRS // N_KV


TM_PRE = 512
TM_MIX = 512
TM_FFN = 512
TQ_GLOBAL = 512
TK_SCORE = 512
TK_VALUE = 256
BF16_SUBLANES = 16
VT_ROWS = HEAD_DIM + BF16_SUBLANES
WQ = 256
WK = WQ + 2 * WINDOW
TQ_WINDOW = 2048
VMEM_LIMIT = 56 * 1024 * 1024
GLOBAL_ITER_UNITS = 4
WINDOW_ITER_UNITS = 4
GLOBAL_TILES_PER_STEP = 4

NT = (((1,), (1,)), ((), ()))


def _rmsnorm(x, gain):
    var = jnp.mean(x * x, axis=-1, keepdims=True)
    return x * lax.rsqrt(var + EPS) * gain


def _const_spec(shape):
    zeros = (0,) * len(shape)
    return pl.BlockSpec(shape, lambda *_: zeros, pipeline_mode=pl.Buffered(1))


def _two_stage(n_tiles, stage_a, stage_b, buf_even, buf_odd):
    i = pl.program_id(0)
    steady = (i > 0) & (i < n_tiles)

    @pl.when(i == 0)
    def _():
        stage_a(buf_even)

    @pl.when(steady & (i % 2 == 1))
    def _():
        stage_b(buf_even)
        stage_a(buf_odd)

    @pl.when(steady & (i % 2 == 0))
    def _():
        stage_b(buf_odd)
        stage_a(buf_even)

    @pl.when(i == n_tiles)
    def _():
        stage_b(buf_even if (n_tiles - 1) % 2 == 0 else buf_odd)


def _cur_tile(n_tiles):
    return lambda i: (jnp.minimum(i, n_tiles - 1), 0)


def _done_tile(i):
    return jnp.maximum(i - 1, 0)


def _pre_kernel(n_tiles, x_ref, g_ref, w_ref, qkg_ref, rope_ref, bd_ref,
                qa_ref, ka_ref, vat_ref, qb_ref, kb_ref, vbt_ref, proj_even, proj_odd):
    tm = x_ref.shape[0]

    def project(proj_ref):
        h = _rmsnorm(x_ref[...], g_ref[...]).astype(BF16)
        proj_ref[...] = jnp.dot(h, w_ref[...], preferred_element_type=F32)

    def finish(proj_ref):
        bd = bd_ref[...]
        cos, sin_up, sin_dn = rope_ref[0], rope_ref[1], rope_ref[2]
        lane = lax.broadcasted_iota(jnp.int32, (tm, LANES), 1)
        lo_half = lane < HEAD_DIM

        def slab(col):
            return proj_ref[:, col * LANES:(col + 1) * LANES]

        def qk_norm_rope(z, gain):
            sq = z * z
            hi = sq.astype(BF16)
            lo = (sq - hi.astype(F32)).astype(BF16)
            ssq = jnp.dot(jnp.concatenate([hi, lo], axis=1), bd, preferred_element_type=F32)
            zn = z * lax.rsqrt(ssq * (1.0 / HEAD_DIM) + EPS) * gain
            return (zn * cos
                    + pltpu.roll(zn, LANES - ROPE_QUARTER, 1) * sin_up
                    + pltpu.roll(zn, ROPE_QUARTER, 1) * sin_dn)

        def put_k(ref, z):
            swapped = pltpu.roll(z, HEAD_DIM, 1)
            ref[0, 0] = jnp.where(lo_half, z, 0.0).astype(BF16)
            ref[0, 1] = jnp.where(lo_half, 0.0, swapped).astype(BF16)
            ref[1, 0] = jnp.where(lo_half, swapped, 0.0).astype(BF16)
            ref[1, 1] = jnp.where(lo_half, 0.0, z).astype(BF16)

        def vt_with_ones(v):
            vt = v.T.astype(BF16)
            ones = jnp.ones((VT_ROWS - HEAD_DIM, tm), BF16)
            return jnp.concatenate([vt[:HEAD_DIM], ones, vt[HEAD_DIM:], ones], axis=0)

        for s in range(N_PAIRS):
            qa_ref[s] = (qk_norm_rope(slab(s), qkg_ref[0]) * Q_SCALE).astype(BF16)
        put_k(ka_ref, qk_norm_rope(slab(N_PAIRS), qkg_ref[1]))
        vat_ref[...] = vt_with_ones(slab(N_PAIRS + 1)).reshape(N_KV, VT_ROWS, tm)
        base = N_PAIRS + 2
        for s in range(N_PAIRS):
            qb_ref[s] = (slab(base + s) * Q_SCALE).astype(BF16)
        put_k(kb_ref, slab(base + N_PAIRS))
        vbt = vt_with_ones(slab(base + N_PAIRS + 1))
        for j in range(tm // BLOCK):
            vbt_ref[j] = vbt[:, j * BLOCK:(j + 1) * BLOCK]

    _two_stage(n_tiles, project, finish, proj_even, proj_odd)


def _pre_call(x2d, seq, g_pre, w_in, qk_gain, rope, bd):
    n_tok = x2d.shape[0]
    tm = TM_PRE
    n_tiles = n_tok // tm
    tiles_per_seq = seq // tm

    done = _done_tile
    q_spec = pl.BlockSpec((N_PAIRS, tm, LANES), lambda i: (0, done(i), 0))
    k_spec = pl.BlockSpec((N_KV, 2, tm, LANES), lambda i: (0, 0, done(i), 0))
    q_shape = jax.ShapeDtypeStruct((N_PAIRS, n_tok, LANES), BF16)
    k_shape = jax.ShapeDtypeStruct((N_KV, 2, n_tok, LANES), BF16)
    return pl.pallas_call(
        functools.partial(_pre_kernel, n_tiles),
        grid=(n_tiles + 1,),
        in_specs=[
            pl.BlockSpec((tm, D_MODEL), _cur_tile(n_tiles)),
            _const_spec((1, D_MODEL)),
            _const_spec((D_MODEL, IN_WIDTH)),
            _const_spec((2, 1, LANES)),
            pl.BlockSpec((3, tm, LANES), lambda i: (0, done(i) % tiles_per_seq, 0)),
            _const_spec((2 * LANES, LANES)),
        ],
        out_specs=[
            q_spec, k_spec,
            pl.BlockSpec((N_KV, VT_ROWS, tm), lambda i: (0, 0, done(i))),
            q_spec, k_spec,
            pl.BlockSpec((tm // BLOCK, N_KV * VT_ROWS, BLOCK), lambda i: (done(i), 0, 0)),
        ],
        out_shape=[
            q_shape, k_shape,
            jax.ShapeDtypeStruct((N_KV, VT_ROWS, n_tok), BF16),
            q_shape, k_shape,
            jax.ShapeDtypeStruct((n_tok // BLOCK, N_KV * VT_ROWS, BLOCK), BF16),
        ],
        scratch_shapes=[pltpu.VMEM((tm, IN_WIDTH), F32), pltpu.VMEM((tm, IN_WIDTH), F32)],
        compiler_params=pltpu.CompilerParams(
            dimension_semantics=("arbitrary",), vmem_limit_bytes=VMEM_LIMIT),
        name="pre",
    )(x2d, g_pre, w_in, qk_gain, rope, bd)


def _fold8(x, op):
    return op(x.reshape(x.shape[0] // SUBLANES, SUBLANES, x.shape[1]), axis=0)


def _softmax_units(n_units, group, iter_units, n_keys, tk_a, tk_b, score_chunk, value_chunk, s_scr, sink_of,
                   emit):
    assert tk_a % tk_b == 0 and n_keys % tk_a == 0
    n_a, b_per_a = n_keys // tk_a, tk_a // tk_b

    def sinks(u):
        vals = [sink_of(u, g) for g in range(group)]
        return None if vals[0] is None else vals

    def sweeps(ua, s_write, ub, m_b, s_read):
        mx = None
        acc = [None] * group
        for ca in range(n_a):
            if ua is not None:
                st = jnp.concatenate([score_chunk(ua, g, ca * tk_a, tk_a) for g in range(group)], axis=1)
                s_write[ca * tk_a:(ca + 1) * tk_a, :] = st
                cm = _fold8(st, jnp.max)
                mx = cm if mx is None else jnp.maximum(mx, cm)
            if ub is not None:
                for cb in range(ca * b_per_a, (ca + 1) * b_per_a):
                    p = jnp.exp2(s_read[cb * tk_b:(cb + 1) * tk_b, :] - m_b).astype(BF16)
                    vt = value_chunk(ub, cb * tk_b, tk_b)
                    tq = p.shape[1] // group
                    for g in range(group):
                        pv = jnp.dot(vt, p[:, g * tq:(g + 1) * tq], preferred_element_type=F32)
                        acc[g] = pv if acc[g] is None else acc[g] + pv
        if ub is not None:
            sink = sinks(ub)
            for g in range(group):
                denom = acc[g][HEAD_DIM:HEAD_DIM + 1]
                if sink is not None:
                    tq = denom.shape[1]
                    denom = denom + jnp.exp2(sink[g] - m_b[:, g * tq:(g + 1) * tq])
                emit(ub, g, acc[g][:HEAD_DIM] / denom)
        if ua is None:
            return None
        m = jnp.max(mx, axis=0, keepdims=True)
        sink = sinks(ua)
        if sink is not None:
            tq = m.shape[1] // group
            m = jnp.concatenate([jnp.maximum(m[:, g * tq:(g + 1) * tq], sink[g]) for g in range(group)],
                                axis=1)
        return m

    def sub_steps(k0, count, m):
        for d in range(count):
            m = sweeps(k0 + d + 1, s_scr[(d + 1) % 2], k0 + d, m, s_scr[d % 2])
        return m

    assert iter_units % 2 == 0
    n_sub = n_units - 1
    trips = n_sub // iter_units + jnp.minimum(pl.program_id(0), 0)
    m = lax.fori_loop(0, trips, lambda t, m: sub_steps(iter_units * t, iter_units, m),
                      sweeps(0, s_scr[0], None, None, None))
    done = n_sub // iter_units * iter_units
    m = sub_steps(done, n_sub - done, m)
    sweeps(None, None, n_units - 1, m, s_scr[(n_units - 1) % 2])


def _gattn_kernel(q_ref, k_ref, vt_ref, o_ref, s_even, s_odd, ot_scr):
    seq = k_ref.shape[2]
    tq = TQ_GLOBAL
    n_tiles = q_ref.shape[1] // tq
    heads_per_kv = N_HEADS // N_KV

    def score_chunk(u, g, row0, tk):
        head = u % N_HEADS
        k = k_ref[head // heads_per_kv, head % 2, row0:row0 + tk, :]
        q = q_ref[head // 2, pl.ds(pl.multiple_of((u // N_HEADS) * tq, tq), tq), :]
        return lax.dot_general(k, q, NT, preferred_element_type=F32)

    def value_chunk(u, row0, tk):
        return vt_ref[(u % N_HEADS) // heads_per_kv, :, row0:row0 + tk]

    def emit(u, g, out):
        ot_scr[u] = out

    _softmax_units(n_tiles * N_HEADS, 1, GLOBAL_ITER_UNITS, seq, TK_SCORE, TK_VALUE, score_chunk, value_chunk,
                   (s_even, s_odd), lambda u, g: None, emit)
    for t in range(n_tiles):
        for pair in range(N_PAIRS):
            u = t * N_HEADS + 2 * pair
            both = jnp.concatenate([ot_scr[u], ot_scr[u + 1]], axis=0)
            o_ref[t * tq:(t + 1) * tq, pair * LANES:(pair + 1) * LANES] = both.T.astype(BF16)


def _gattn_call(q, k, vt, batch, seq):
    tq = TQ_GLOBAL
    rows = tq * GLOBAL_TILES_PER_STEP
    nq = seq // rows
    n_tok = batch * seq
    return pl.pallas_call(
        _gattn_kernel,
        grid=(batch, nq),
        in_specs=[
            pl.BlockSpec((N_PAIRS, rows, LANES), lambda b, i: (0, b * nq + i, 0)),
            pl.BlockSpec((N_KV, 2, seq, LANES), lambda b, i: (0, 0, b, 0)),
            pl.BlockSpec((N_KV, VT_ROWS, seq), lambda b, i: (0, 0, b)),
        ],
        out_specs=pl.BlockSpec((rows, Q_WIDTH), lambda b, i: (b * nq + i, 0)),
        out_shape=jax.ShapeDtypeStruct((n_tok, Q_WIDTH), BF16),
        scratch_shapes=[pltpu.VMEM((seq, tq), F32), pltpu.VMEM((seq, tq), F32),
                        pltpu.VMEM((GLOBAL_TILES_PER_STEP * N_HEADS, HEAD_DIM, tq), F32)],
        compiler_params=pltpu.CompilerParams(
            dimension_semantics=("arbitrary", "arbitrary"),
            vmem_limit_bytes=VMEM_LIMIT),
        name="gattn",
    )(q, k, vt)


def _t5_bucket_table():
    half = N_BUCKETS // 2
    max_exact = half // 2
    rel = np.arange(WK)[:, None] - WINDOW - np.arange(WQ)[None, :]
    n = np.abs(rel)
    assert MAX_DISTANCE // max_exact == 16 and half - max_exact == 8
    large = np.zeros_like(n)
    for kk in range(1, 2 * half):
        large += ((n * n) >= (2 ** kk) * max_exact * max_exact)
    large = np.minimum(max_exact + large, half - 1)
    bucket = np.where(rel > 0, half, 0) + np.where(n < max_exact, n, large)
    return bucket.astype(np.int32)


def _wattn_kernel(seq, q_ref, k_ref, vt_ref, bkt_ref, rb_ref, sink_ref, o_ref,
                  kpad, vtpad, bias_edge, bias_mid, s_even, s_odd, ot_scr):
    b = pl.program_id(0)
    i = pl.program_id(1)
    n_q_step = q_ref.shape[1] // WQ
    n_q_seq = seq // WQ
    n_blk_seq = seq // BLOCK
    edge = WINDOW

    @pl.when((b == 0) & (i == 0))
    def _():
        bucket = bkt_ref[...]
        row = lax.broadcasted_iota(jnp.int32, bucket.shape, 0)
        col = lax.broadcasted_iota(jnp.int32, bucket.shape, 1)
        band = jnp.abs(row - WINDOW - col) <= WINDOW
        neg = jnp.full((edge, WQ), NEG_INF, F32)
        bias_edge[0, N_HEADS] = neg
        bias_edge[1, N_HEADS] = neg
        for head in range(N_HEADS):
            bias = jnp.zeros(bucket.shape, F32)
            for bk in range(N_BUCKETS):
                bias = jnp.where(bucket == bk, rb_ref[bk, head], bias)
            bias = jnp.where(band, bias * LOG2E, NEG_INF)
            bias_edge[0, head] = bias[:edge]
            bias_mid[head] = bias[edge:WK - edge]
            bias_edge[1, head] = bias[WK - edge:]

    @pl.when(i == 0)
    def _():
        zk = jnp.zeros((WINDOW, LANES), BF16)
        for kv in range(N_KV):
            for half in range(2):
                kpad[kv, half, 0:WINDOW, :] = zk
                kpad[kv, half, WINDOW:WINDOW + seq, :] = k_ref[kv, half]
                kpad[kv, half, WINDOW + seq:2 * WINDOW + seq, :] = zk
        zv = jnp.zeros((N_KV, VT_ROWS, BLOCK), BF16)
        vtpad[0] = zv
        vtpad[pl.ds(1, n_blk_seq)] = vt_ref[...].reshape(n_blk_seq, N_KV, VT_ROWS, BLOCK)
        vtpad[n_blk_seq + 1] = zv

    group = N_HEADS // N_KV

    def score_chunk(u, g, row0, tk):
        assert row0 == 0 and tk == WK
        j, kv = u // N_KV, u % N_KV
        head = kv * group + g
        qb = i * n_q_step + j
        q = q_ref[head // 2, pl.ds(pl.multiple_of(j * WQ, WQ), WQ), :]
        st = lax.dot_general(kpad[kv, g % 2, pl.ds(pl.multiple_of(qb * WQ, WQ), WK), :], q,
                             NT, preferred_element_type=F32)
        bias = jnp.concatenate([
            bias_edge[0, jnp.where(qb == 0, N_HEADS, head)],
            bias_mid[head],
            bias_edge[1, jnp.where(qb == n_q_seq - 1, N_HEADS, head)]], axis=0)
        return st + bias

    def value_chunk(u, row0, tk):
        j, kv = u // N_KV, u % N_KV
        blk0 = (i * n_q_step + j) * (WQ // BLOCK) + row0 // BLOCK
        return jnp.concatenate([vtpad[blk0 + t, kv] for t in range(tk // BLOCK)], axis=1)

    def emit(u, g, out):
        ot_scr[(u // N_KV) * N_HEADS + (u % N_KV) * group + g] = out

    _softmax_units(n_q_step * N_KV, group, WINDOW_ITER_UNITS, WK, WK, TK_VALUE, score_chunk, value_chunk,
                   (s_even, s_odd),
                   lambda u, g: sink_ref[(u % N_KV) * group + g] * LOG2E, emit)
    for j in range(n_q_step):
        for pair in range(N_PAIRS):
            u = j * N_HEADS + 2 * pair
            both = jnp.concatenate([ot_scr[u], ot_scr[u + 1]], axis=0)
            o_ref[j * WQ:(j + 1) * WQ, pair * LANES:(pair + 1) * LANES] = both.T.astype(BF16)


def _wattn_call(q, k, vt3, batch, seq, bucket, rel_bias, sink):
    tq = TQ_WINDOW
    nq = seq // tq
    n_tok = batch * seq
    n_blk_seq = seq // BLOCK
    return pl.pallas_call(
        functools.partial(_wattn_kernel, seq),
        grid=(batch, nq),
        in_specs=[
            pl.BlockSpec((N_PAIRS, tq, LANES), lambda b, i: (0, b * nq + i, 0)),
            pl.BlockSpec((N_KV, 2, seq, LANES), lambda b, i: (0, 0, b, 0)),
            pl.BlockSpec((n_blk_seq, N_KV * VT_ROWS, BLOCK), lambda b, i: (b, 0, 0)),
            _const_spec((WK, WQ)),
            pl.BlockSpec(memory_space=pltpu.SMEM),
            pl.BlockSpec(memory_space=pltpu.SMEM),
        ],
        out_specs=pl.BlockSpec((tq, Q_WIDTH), lambda b, i: (b * nq + i, 0)),
        out_shape=jax.ShapeDtypeStruct((n_tok, Q_WIDTH), BF16),
        scratch_shapes=[
            pltpu.VMEM((N_KV, 2, seq + 2 * WINDOW, LANES), BF16),
            pltpu.VMEM((n_blk_seq + 2, N_KV, VT_ROWS, BLOCK), BF16),
            pltpu.VMEM((2, N_HEADS + 1, WINDOW, WQ), F32),
            pltpu.VMEM((N_HEADS, WK - 2 * WINDOW, WQ), F32),
            pltpu.VMEM((WK, N_HEADS // N_KV * WQ), F32),
            pltpu.VMEM((WK, N_HEADS // N_KV * WQ), F32),
            pltpu.VMEM((tq // WQ * N_HEADS, HEAD_DIM, WQ), F32),
        ],
        compiler_params=pltpu.CompilerParams(
            dimension_semantics=("arbitrary", "arbitrary"),
            vmem_limit_bytes=VMEM_LIMIT),
        name="wattn",
    )(q, k, vt3, bucket, rel_bias, sink)


def _mix_kernel(x_ref, ya_ref, yb_ref, gpre_ref, wg_ref, bg_ref, wa_ref, wb_ref, wo_ref, gpost_ref, o_ref):
    x = x_ref[...]
    h = _rmsnorm(x, gpre_ref[...]).astype(BF16)
    z = jnp.dot(h, wg_ref[...], preferred_element_type=F32) + bg_ref[...]
    gates = 1.0 / (1.0 + jnp.exp(-z))
    a = jnp.dot(ya_ref[...], wa_ref[...], preferred_element_type=F32)
    b = jnp.dot(yb_ref[...], wb_ref[...], preferred_element_type=F32)
    mix = gates[:, :D_MODEL] * a + gates[:, D_MODEL:] * b
    o = jnp.dot(mix.astype(BF16), wo_ref[...], preferred_element_type=F32)
    o_ref[...] = x + _rmsnorm(o, gpost_ref[...])


def _mix_call(x2d, ya, yb, g_pre, w_gate, b_gate, w_a, w_b, w_out, g_post):
    n_tok = x2d.shape[0]
    tm = TM_MIX
    return pl.pallas_call(
        _mix_kernel,
        grid=(n_tok // tm,),
        in_specs=[
            pl.BlockSpec((tm, D_MODEL), lambda i: (i, 0)),
            pl.BlockSpec((tm, Q_WIDTH), lambda i: (i, 0)),
            pl.BlockSpec((tm, Q_WIDTH), lambda i: (i, 0)),
            _const_spec((1, D_MODEL)),
            _const_spec((D_MODEL, 2 * D_MODEL)),
            _const_spec((1, 2 * D_MODEL)),
            _const_spec((Q_WIDTH, D_MODEL)),
            _const_spec((Q_WIDTH, D_MODEL)),
            _const_spec((D_MODEL, D_MODEL)),
            _const_spec((1, D_MODEL)),
        ],
        out_specs=pl.BlockSpec((tm, D_MODEL), lambda i: (i, 0)),
        out_shape=jax.ShapeDtypeStruct((n_tok, D_MODEL), F32),
        compiler_params=pltpu.CompilerParams(
            dimension_semantics=("arbitrary",), vmem_limit_bytes=VMEM_LIMIT),
        name="mix",
    )(x2d, ya, yb, g_pre, w_gate, b_gate, w_a, w_b, w_out, g_post)


def _ffn_kernel(x_ref, gpre_ref, wg_ref, wu_ref, wd_ref, gpost_ref, o_ref):
    x = x_ref[...]
    h = _rmsnorm(x, gpre_ref[...]).astype(BF16)
    g = jnp.dot(h, wg_ref[...], preferred_element_type=F32)
    u = jnp.dot(h, wu_ref[...], preferred_element_type=F32)
    act = (g / (1.0 + jnp.exp(-g))) * u
    f = jnp.dot(act.astype(BF16), wd_ref[...], preferred_element_type=F32)
    o_ref[...] = x + _rmsnorm(f, gpost_ref[...])


def _ffn_call(x2d, g_pre, w_g, w_u, w_d, g_post):
    n_tok = x2d.shape[0]
    tm = TM_FFN
    return pl.pallas_call(
        _ffn_kernel,
        grid=(n_tok // tm,),
        in_specs=[
            pl.BlockSpec((tm, D_MODEL), lambda i: (i, 0)),
            _const_spec((1, D_MODEL)),
            _const_spec((D_MODEL, D_FF)),
            _const_spec((D_MODEL, D_FF)),
            _const_spec((D_FF, D_MODEL)),
            _const_spec((1, D_MODEL)),
        ],
        out_specs=pl.BlockSpec((tm, D_MODEL), lambda i: (i, 0)),
        out_shape=jax.ShapeDtypeStruct((n_tok, D_MODEL), F32),
        compiler_params=pltpu.CompilerParams(
            dimension_semantics=("arbitrary",), vmem_limit_bytes=VMEM_LIMIT),
        name="ffn",
    )(x2d, g_pre, w_g, w_u, w_d, g_post)


def _rope_tables(seq):
    pos = np.arange(seq)
    rows = (pos // GRID_W).astype(np.float32)
    cols = (pos % GRID_W).astype(np.float32)
    inv_freq = (np.float32(1.0) / np.power(
        np.float32(ROPE_THETA), np.arange(0, ROPE_HALF, 2, dtype=np.float32) / np.float32(ROPE_HALF)))
    lane = np.arange(LANES)
    d = lane % HEAD_DIM
    use_col = (d // ROPE_HALF) == 1
    j = d % ROPE_HALF
    f_idx = j % ROPE_QUARTER
    coord = np.where(use_col[None, :], cols[:, None], rows[:, None])
    ang = (coord * inv_freq.astype(np.float32)[f_idx][None, :]).astype(np.float32).astype(np.float64)
    cos, sin = np.cos(ang), np.sin(ang)
    first = (j < ROPE_QUARTER)[None, :]
    sin_up = np.where(first, -sin, 0.0)
    sin_dn = np.where(first, 0.0, sin)
    return jnp.asarray(np.stack([cos, sin_up, sin_dn]).astype(np.float32))


def _layer(x, bucket, bd, p):
    batch, seq, _ = x.shape
    assert seq % TM_PRE == 0 and seq % (TQ_GLOBAL * GLOBAL_TILES_PER_STEP) == 0
    assert seq % TQ_WINDOW == 0 and seq // WQ >= 2
    x2d = x.reshape(batch * seq, D_MODEL)
    qa, ka, vat, qb, kb, vbt = _pre_call(x2d, seq, p["g_mix_pre"], p["w_in"], p["qk_gain"],
                                         _rope_tables(seq), bd)
    ya = _gattn_call(qa, ka, vat, batch, seq)
    yb = _wattn_call(qb, kb, vbt, batch, seq, bucket, p["rel_bias"], p["sink"])
    x1 = _mix_call(x2d, ya, yb, p["g_mix_pre"], p["w_gate"], p["b_gate"], p["w_a"], p["w_b"], p["w_out"],
                   p["g_mix_post"])
    y = _ffn_call(x1, p["g_ffn_pre"], p["w_ffn_gate"], p["w_ffn_up"], p["w_ffn_down"], p["g_ffn_post"])
    return y.reshape(batch, seq, D_MODEL)


def kernel(x_prompt, x_sample, norm_mix_pre, norm_mix_post, w_in, q_norm_a, k_norm_a, sink_b, rel_bias,
           w_branch_a, w_branch_b, w_gate, b_gate, w_out, norm_ffn_pre, norm_ffn_post, w_ffn_gate,
           w_ffn_up, w_ffn_down):
    depth = w_in.shape[0]
    bucket = jnp.asarray(_t5_bucket_table())
    blk = np.arange(LANES) // HEAD_DIM
    bd = jnp.asarray(np.tile(blk[:, None] == blk[None, :], (2, 1)), dtype=BF16)
    y_prompt, y_sample = x_prompt, x_sample
    for l in range(depth):
        p = {
            "g_mix_pre": norm_mix_pre[l].reshape(1, D_MODEL),
            "g_mix_post": norm_mix_post[l].reshape(1, D_MODEL),
            "w_in": w_in[l].astype(BF16),
            "qk_gain": jnp.stack([jnp.tile(q_norm_a[l], 2), jnp.tile(k_norm_a[l], 2)]).reshape(2, 1, LANES),
            "sink": sink_b[l],
            "rel_bias": rel_bias,
            "w_a": w_branch_a[l].astype(BF16),
            "w_b": w_branch_b[l].astype(BF16),
            "w_gate": w_gate[l].astype(BF16),
            "b_gate": b_gate[l].reshape(1, 2 * D_MODEL),
            "w_out": w_out[l].astype(BF16),
            "g_ffn_pre": norm_ffn_pre[l].reshape(1, D_MODEL),
            "g_ffn_post": norm_ffn_post[l].reshape(1, D_MODEL),
            "w_ffn_gate": w_ffn_gate[l].astype(BF16),
            "w_ffn_up": w_ffn_up[l].astype(BF16),
            "w_ffn_down": w_ffn_down[l].astype(BF16),
        }
        y_prompt = _layer(y_prompt, bucket, bd, p)
        y_sample = _layer(y_sample, bucket, bd, p)
    return (y_prompt, y_sample)
```

```python
import functools
import math

import numpy as np
import jax
import jax.numpy as jnp
from jax import lax
from jax.experimental import pallas as pl
from jax.experimental.pallas import tpu as pltpu

F32 = jnp.float32
BF16 = jnp.bfloat16

D_MODEL = 1024
HEAD_DIM = 64
N_HEADS = 8
N_KV = 2
Q_WIDTH = N_HEADS * HEAD_DIM
KV_WIDTH = N_KV * HEAD_DIM
IN_WIDTH = 2 * (Q_WIDTH + 2 * KV_WIDTH)
BLOCK = 128
WINDOW = 128
GRID_W = 64
ROPE_THETA = 10000.0
ROPE_HALF = HEAD_DIM // 2
ROPE_QUARTER = ROPE_HALF // 2
N_BUCKETS = 32
MAX_DISTANCE = 128
D_FF = 2816
EPS = 1e-6
NEG_INF = -1e30
LOG2E = 1.4426950408889634
Q_SCALE = LOG2E / math.sqrt(HEAD_DIM)

LANES = 128
SUBLANES = 8
N_PAIRS = N_HEADS // 2
PAIRS_PER_KV = N_PAIRS // N_KV


TM_PRE = 512
TM_MIX = 1024
TM_FFN = 512
TQ_GLOBAL = 512
TK_SCORE = 512
TK_VALUE = 256
BF16_SUBLANES = 16
VT_ROWS = HEAD_DIM + BF16_SUBLANES
WQ = 256
WK = WQ + 2 * WINDOW
TQ_WINDOW = 2048
VMEM_LIMIT = 56 * 1024 * 1024
GLOBAL_ITER_UNITS = 8
WINDOW_ITER_UNITS = 4
GLOBAL_TILES_PER_STEP = 4

NT = (((1,), (1,)), ((), ()))


def _rmsnorm(x, gain):
    var = jnp.mean(x * x, axis=-1, keepdims=True)
    return x * lax.rsqrt(var + EPS) * gain


def _const_spec(shape):
    zeros = (0,) * len(shape)
    return pl.BlockSpec(shape, lambda *_: zeros, pipeline_mode=pl.Buffered(1))


def _two_stage(n_tiles, stage_a, stage_b, buf_even, buf_odd):
    i = pl.program_id(0)
    steady = (i > 0) & (i < n_tiles)

    @pl.when(i == 0)
    def _():
        stage_a(buf_even)

    @pl.when(steady & (i % 2 == 1))
    def _():
        stage_b(buf_even)
        stage_a(buf_odd)

    @pl.when(steady & (i % 2 == 0))
    def _():
        stage_b(buf_odd)
        stage_a(buf_even)

    @pl.when(i == n_tiles)
    def _():
        stage_b(buf_even if (n_tiles - 1) % 2 == 0 else buf_odd)


def _cur_tile(n_tiles):
    return lambda i: (jnp.minimum(i, n_tiles - 1), 0)


def _done_tile(i):
    return jnp.maximum(i - 1, 0)


def _pre_kernel(n_tiles, x_ref, g_ref, w_ref, qkg_ref, rope_ref, bd_ref,
                qa_ref, ka_ref, vat_ref, qb_ref, kb_ref, vbt_ref, proj_even, proj_odd):
    tm = x_ref.shape[0]

    def project(proj_ref):
        h = _rmsnorm(x_ref[...], g_ref[...]).astype(BF16)
        proj_ref[...] = jnp.dot(h, w_ref[...], preferred_element_type=F32)

    def finish(proj_ref):
        bd = bd_ref[...]
        cos, sin_up, sin_dn = rope_ref[0], rope_ref[1], rope_ref[2]
        lane = lax.broadcasted_iota(jnp.int32, (tm, LANES), 1)
        lo_half = lane < HEAD_DIM

        def slab(col):
            return proj_ref[:, col * LANES:(col + 1) * LANES]

        def qk_norm_rope(z, gain):
            sq = z * z
            hi = sq.astype(BF16)
            lo = (sq - hi.astype(F32)).astype(BF16)
            ssq = jnp.dot(jnp.concatenate([hi, lo], axis=1), bd, preferred_element_type=F32)
            zn = z * lax.rsqrt(ssq * (1.0 / HEAD_DIM) + EPS) * gain
            return (zn * cos
                    + pltpu.roll(zn, LANES - ROPE_QUARTER, 1) * sin_up
                    + pltpu.roll(zn, ROPE_QUARTER, 1) * sin_dn)

        def put_k(ref, z):
            swapped = pltpu.roll(z, HEAD_DIM, 1)
            ref[0, 0] = jnp.where(lo_half, z, 0.0).astype(BF16)
            ref[0, 1] = jnp.where(lo_half, 0.0, swapped).astype(BF16)
            ref[1, 0] = jnp.where(lo_half, swapped, 0.0).astype(BF16)
            ref[1, 1] = jnp.where(lo_half, 0.0, z).astype(BF16)

        def vt_with_ones(v):
            vt = v.T.astype(BF16)
            ones = jnp.ones((VT_ROWS - HEAD_DIM, tm), BF16)
            return jnp.concatenate([vt[:HEAD_DIM], ones, vt[HEAD_DIM:], ones], axis=0)

        for s in range(N_PAIRS):
            qa_ref[s] = (qk_norm_rope(slab(s), qkg_ref[0]) * Q_SCALE).astype(BF16)
        put_k(ka_ref, qk_norm_rope(slab(N_PAIRS), qkg_ref[1]))
        vat_ref[...] = vt_with_ones(slab(N_PAIRS + 1)).reshape(N_KV, VT_ROWS, tm)
        base = N_PAIRS + 2
        for s in range(N_PAIRS):
            qb_ref[s] = (slab(base + s) * Q_SCALE).astype(BF16)
        put_k(kb_ref, slab(base + N_PAIRS))
        vbt = vt_with_ones(slab(base + N_PAIRS + 1))
        for j in range(tm // BLOCK):
            vbt_ref[j] = vbt[:, j * BLOCK:(j + 1) * BLOCK]

    _two_stage(n_tiles, project, finish, proj_even, proj_odd)


def _pre_call(x2d, seq, g_pre, w_in, qk_gain, rope, bd):
    n_tok = x2d.shape[0]
    tm = TM_PRE
    n_tiles = n_tok // tm
    tiles_per_seq = seq // tm

    done = _done_tile
    q_spec = pl.BlockSpec((N_PAIRS, tm, LANES), lambda i: (0, done(i), 0))
    k_spec = pl.BlockSpec((N_KV, 2, tm, LANES), lambda i: (0, 0, done(i), 0))
    q_shape = jax.ShapeDtypeStruct((N_PAIRS, n_tok, LANES), BF16)
    k_shape = jax.ShapeDtypeStruct((N_KV, 2, n_tok, LANES), BF16)
    return pl.pallas_call(
        functools.partial(_pre_kernel, n_tiles),
        grid=(n_tiles + 1,),
        in_specs=[
            pl.BlockSpec((tm, D_MODEL), _cur_tile(n_tiles)),
            _const_spec((1, D_MODEL)),
            _const_spec((D_MODEL, IN_WIDTH)),
            _const_spec((2, 1, LANES)),
            pl.BlockSpec((3, tm, LANES), lambda i: (0, done(i) % tiles_per_seq, 0)),
            _const_spec((2 * LANES, LANES)),
        ],
        out_specs=[
            q_spec, k_spec,
            pl.BlockSpec((N_KV, VT_ROWS, tm), lambda i: (0, 0, done(i))),
            q_spec, k_spec,
            pl.BlockSpec((tm // BLOCK, N_KV * VT_ROWS, BLOCK), lambda i: (done(i), 0, 0)),
        ],
        out_shape=[
            q_shape, k_shape,
            jax.ShapeDtypeStruct((N_KV, VT_ROWS, n_tok), BF16),
            q_shape, k_shape,
            jax.ShapeDtypeStruct((n_tok // BLOCK, N_KV * VT_ROWS, BLOCK), BF16),
        ],
        scratch_shapes=[pltpu.VMEM((tm, IN_WIDTH), F32), pltpu.VMEM((tm, IN_WIDTH), F32)],
        compiler_params=pltpu.CompilerParams(
            dimension_semantics=("arbitrary",), vmem_limit_bytes=VMEM_LIMIT),
        name="pre",
    )(x2d, g_pre, w_in, qk_gain, rope, bd)


def _fold8(x, op):
    return op(x.reshape(x.shape[0] // SUBLANES, SUBLANES, x.shape[1]), axis=0)


def _softmax_units(n_units, group, iter_units, n_keys, tk_a, tk_b, score_chunk, value_chunk, s_scr, sink_of,
                   emit, put_scores=None, get_probs=None):
    assert tk_a % tk_b == 0 and n_keys % tk_a == 0
    n_a, b_per_a = n_keys // tk_a, tk_a // tk_b

    def sinks(u):
        vals = [sink_of(u, g) for g in range(group)]
        return None if vals[0] is None else vals

    def sweeps(ua, s_write, ub, m_b, s_read):
        mx = None
        acc = [None] * group
        for ca in range(n_a):
            if ua is not None:
                if put_scores is None:
                    st = jnp.concatenate([score_chunk(ua, g, ca * tk_a, tk_a) for g in range(group)], axis=1)
                    s_write[ca * tk_a:(ca + 1) * tk_a, :] = st
                    cm = _fold8(st, jnp.max)
                else:
                    assert n_a == 1
                    cm = put_scores(ua, s_write)
                mx = cm if mx is None else jnp.maximum(mx, cm)
            if ub is not None:
                for cb in range(ca * b_per_a, (ca + 1) * b_per_a):
                    if get_probs is None:
                        p = jnp.exp2(s_read[cb * tk_b:(cb + 1) * tk_b, :] - m_b).astype(BF16)
                    else:
                        p = get_probs(s_read, cb * tk_b, tk_b, m_b)
                    vt = value_chunk(ub, cb * tk_b, tk_b)
                    tq = p.shape[1] // group
                    for g in range(group):
                        pv = jnp.dot(vt, p[:, g * tq:(g + 1) * tq], preferred_element_type=F32)
                        acc[g] = pv if acc[g] is None else acc[g] + pv
        if ub is not None:
            sink = sinks(ub)
            for g in range(group):
                denom = acc[g][HEAD_DIM:HEAD_DIM + 1]
                if sink is not None:
                    tq = denom.shape[1]
                    denom = denom + jnp.exp2(sink[g] - m_b[:, g * tq:(g + 1) * tq])
                emit(ub, g, acc[g][:HEAD_DIM] / denom)
        if ua is None:
            return None
        m = jnp.max(mx, axis=0, keepdims=True)
        sink = sinks(ua)
        if sink is not None:
            tq = m.shape[1] // group
            m = jnp.concatenate([jnp.maximum(m[:, g * tq:(g + 1) * tq], sink[g]) for g in range(group)],
                                axis=1)
        return m

    def sub_steps(k0, count, m):
        for d in range(count):
            m = sweeps(k0 + d + 1, s_scr[(d + 1) % 2], k0 + d, m, s_scr[d % 2])
        return m

    assert iter_units % 2 == 0
    n_sub = n_units - 1
    trips = n_sub // iter_units + jnp.minimum(pl.program_id(0), 0)
    m = lax.fori_loop(0, trips, lambda t, m: sub_steps(iter_units * t, iter_units, m),
                      sweeps(0, s_scr[0], None, None, None))
    done = n_sub // iter_units * iter_units
    m = sub_steps(done, n_sub - done, m)
    sweeps(None, None, n_units - 1, m, s_scr[(n_units - 1) % 2])


def _gattn_kernel(q_ref, k_ref, vt_ref, o_ref, s_even, s_odd, ot_scr):
    seq = k_ref.shape[2]
    tq = TQ_GLOBAL
    n_tiles = q_ref.shape[1] // tq
    heads_per_kv = N_HEADS // N_KV

    def score_chunk(u, g, row0, tk):
        head = u % N_HEADS
        k = k_ref[head // heads_per_kv, head % 2, row0:row0 + tk, :]
        q = q_ref[head // 2, pl.ds(pl.multiple_of((u // N_HEADS) * tq, tq), tq), :]
        return lax.dot_general(k, q, NT, preferred_element_type=F32)

    def value_chunk(u, row0, tk):
        return vt_ref[(u % N_HEADS) // heads_per_kv, :, row0:row0 + tk]

    def emit(u, g, out):
        ot_scr[u] = out

    _softmax_units(n_tiles * N_HEADS, 1, GLOBAL_ITER_UNITS, seq, TK_SCORE, TK_VALUE, score_chunk, value_chunk,
                   (s_even, s_odd), lambda u, g: None, emit)
    for t in range(n_tiles):
        for pair in range(N_PAIRS):
            u = t * N_HEADS + 2 * pair
            both = jnp.concatenate([ot_scr[u], ot_scr[u + 1]], axis=0)
            o_ref[t * tq:(t + 1) * tq, pair * LANES:(pair + 1) * LANES] = both.T.astype(BF16)


def _gattn_call(q, k, vt, batch, seq):
    tq = TQ_GLOBAL
    rows = tq * GLOBAL_TILES_PER_STEP
    nq = seq // rows
    n_tok = batch * seq
    return pl.pallas_call(
        _gattn_kernel,
        grid=(batch, nq),
        in_specs=[
            pl.BlockSpec((N_PAIRS, rows, LANES), lambda b, i: (0, b * nq + i, 0)),
            pl.BlockSpec((N_KV, 2, seq, LANES), lambda b, i: (0, 0, b, 0)),
            pl.BlockSpec((N_KV, VT_ROWS, seq), lambda b, i: (0, 0, b)),
        ],
        out_specs=pl.BlockSpec((rows, Q_WIDTH), lambda b, i: (b * nq + i, 0)),
        out_shape=jax.ShapeDtypeStruct((n_tok, Q_WIDTH), BF16),
        scratch_shapes=[pltpu.VMEM((seq, tq), F32), pltpu.VMEM((seq, tq), F32),
                        pltpu.VMEM((GLOBAL_TILES_PER_STEP * N_HEADS, HEAD_DIM, tq), F32)],
        compiler_params=pltpu.CompilerParams(
            dimension_semantics=("arbitrary", "arbitrary"),
            vmem_limit_bytes=VMEM_LIMIT),
        name="gattn",
    )(q, k, vt)


def _t5_bucket_table():
    half = N_BUCKETS // 2
    max_exact = half // 2
    rel = np.arange(WK)[:, None] - WINDOW - np.arange(WQ)[None, :]
    n = np.abs(rel)
    assert MAX_DISTANCE // max_exact == 16 and half - max_exact == 8
    large = np.zeros_like(n)
    for kk in range(1, 2 * half):
        large += ((n * n) >= (2 ** kk) * max_exact * max_exact)
    large = np.minimum(max_exact + large, half - 1)
    bucket = np.where(rel > 0, half, 0) + np.where(n < max_exact, n, large)
    return bucket.astype(np.int32)


def _wattn_kernel(seq, q_ref, k_ref, vt_ref, bkt_ref, rb_ref, sink_ref, o_ref,
                  kpad, vtpad, bias_edge, bias_mid, s_even, s_odd, ot_scr):
    b = pl.program_id(0)
    i = pl.program_id(1)
    n_q_step = q_ref.shape[1] // WQ
    n_q_seq = seq // WQ
    n_blk_seq = seq // BLOCK
    edge = WINDOW

    assert WINDOW == BLOCK and WQ == 2 * BLOCK and TK_VALUE == 2 * BLOCK
    top_rc = (slice(0, edge), slice(0, BLOCK))
    mid_rc = (slice(edge, WK - edge), slice(0, WQ))
    bot_rc = (slice(WK - edge, WK), slice(BLOCK, WQ))

    @pl.when((b == 0) & (i == 0))
    def _():
        neg = jnp.full((edge, BLOCK), NEG_INF, F32)
        bias_edge[0, N_HEADS] = neg
        bias_edge[1, N_HEADS] = neg

        def table(rc, head):
            bucket = bkt_ref[rc]
            row = lax.broadcasted_iota(jnp.int32, bucket.shape, 0) + rc[0].start
            col = lax.broadcasted_iota(jnp.int32, bucket.shape, 1) + rc[1].start
            bias = jnp.zeros(bucket.shape, F32)
            for bk in range(N_BUCKETS):
                bias = jnp.where(bucket == bk, rb_ref[bk, head], bias)
            return jnp.where(jnp.abs(row - WINDOW - col) <= WINDOW, bias * LOG2E, NEG_INF)

        for head in range(N_HEADS):
            bias_edge[0, head] = table(top_rc, head)
            bias_mid[head] = table(mid_rc, head)
            bias_edge[1, head] = table(bot_rc, head)

    @pl.when(i == 0)
    def _():
        zk = jnp.zeros((WINDOW, LANES), BF16)
        for kv in range(N_KV):
            for half in range(2):
                kpad[kv, half, 0:WINDOW, :] = zk
                kpad[kv, half, WINDOW:WINDOW + seq, :] = k_ref[kv, half]
                kpad[kv, half, WINDOW + seq:2 * WINDOW + seq, :] = zk
        zv = jnp.zeros((N_KV, VT_ROWS, BLOCK), BF16)
        vtpad[0] = zv
        vtpad[pl.ds(1, n_blk_seq)] = vt_ref[...].reshape(n_blk_seq, N_KV, VT_ROWS, BLOCK)
        vtpad[n_blk_seq + 1] = zv

    group = N_HEADS // N_KV

    def put_scores(u, s_write):
        j, kv = u // N_KV, u % N_KV
        qb = i * n_q_step + j
        cms = []
        for g in range(group):
            head = kv * group + g
            q = q_ref[head // 2, pl.ds(pl.multiple_of(j * WQ, WQ), WQ), :]
            st = lax.dot_general(kpad[kv, g % 2, pl.ds(pl.multiple_of(qb * WQ, WQ), WK), :], q,
                                 NT, preferred_element_type=F32)
            top = st[top_rc] + bias_edge[0, jnp.where(qb == 0, N_HEADS, head)]
            mid = st[mid_rc] + bias_mid[head]
            bot = st[bot_rc] + bias_edge[1, jnp.where(qb == n_q_seq - 1, N_HEADS, head)]
            c0 = g * WQ
            s_write[top_rc[0], c0:c0 + BLOCK] = top
            s_write[mid_rc[0], c0:c0 + WQ] = mid
            s_write[bot_rc[0], c0 + BLOCK:c0 + WQ] = bot
            edge_max = jnp.concatenate([_fold8(top, jnp.max), _fold8(bot, jnp.max)], axis=1)
            cms.append(jnp.maximum(_fold8(mid, jnp.max), edge_max))
        return jnp.concatenate(cms, axis=1)

    def get_probs(s_read, row0, tk, m):
        zeros = jnp.zeros((BLOCK, BLOCK), F32)

        def half_rows(rows, lane0):
            parts = []
            for g in range(group):
                c = g * WQ + lane0
                live = jnp.exp2(s_read[rows, c:c + BLOCK] - m[:, c:c + BLOCK])
                parts += [live, zeros] if lane0 == 0 else [zeros, live]
            return jnp.concatenate(parts, axis=1)

        def full_rows(rows):
            return jnp.exp2(s_read[rows, :] - m)

        if row0 == 0:
            p = [half_rows(top_rc[0], 0), full_rows(slice(edge, tk))]
        else:
            p = [full_rows(slice(row0, WK - edge)), half_rows(bot_rc[0], BLOCK)]
        return jnp.concatenate(p, axis=0).astype(BF16)

    def value_chunk(u, row0, tk):
        j, kv = u // N_KV, u % N_KV
        blk0 = (i * n_q_step + j) * (WQ // BLOCK) + row0 // BLOCK
        return jnp.concatenate([vtpad[blk0 + t, kv] for t in range(tk // BLOCK)], axis=1)

    def emit(u, g, out):
        ot_scr[(u // N_KV) * N_HEADS + (u % N_KV) * group + g] = out

    _softmax_units(n_q_step * N_KV, group, WINDOW_ITER_UNITS, WK, WK, TK_VALUE, None, value_chunk,
                   (s_even, s_odd),
                   lambda u, g: sink_ref[(u % N_KV) * group + g] * LOG2E, emit,
                   put_scores=put_scores, get_probs=get_probs)
    for j in range(n_q_step):
        for pair in range(N_PAIRS):
            u = j * N_HEADS + 2 * pair
            both = jnp.concatenate([ot_scr[u], ot_scr[u + 1]], axis=0)
            o_ref[j * WQ:(j + 1) * WQ, pair * LANES:(pair + 1) * LANES] = both.T.astype(BF16)


def _wattn_call(q, k, vt3, batch, seq, bucket, rel_bias, sink):
    tq = TQ_WINDOW
    nq = seq // tq
    n_tok = batch * seq
    n_blk_seq = seq // BLOCK
    return pl.pallas_call(
        functools.partial(_wattn_kernel, seq),
        grid=(batch, nq),
        in_specs=[
            pl.BlockSpec((N_PAIRS, tq, LANES), lambda b, i: (0, b * nq + i, 0)),
            pl.BlockSpec((N_KV, 2, seq, LANES), lambda b, i: (0, 0, b, 0)),
            pl.BlockSpec((n_blk_seq, N_KV * VT_ROWS, BLOCK), lambda b, i: (b, 0, 0)),
            _const_spec((WK, WQ)),
            pl.BlockSpec(memory_space=pltpu.SMEM),
            pl.BlockSpec(memory_space=pltpu.SMEM),
        ],
        out_specs=pl.BlockSpec((tq, Q_WIDTH), lambda b, i: (b * nq + i, 0)),
        out_shape=jax.ShapeDtypeStruct((n_tok, Q_WIDTH), BF16),
        scratch_shapes=[
            pltpu.VMEM((N_KV, 2, seq + 2 * WINDOW, LANES), BF16),
            pltpu.VMEM((n_blk_seq + 2, N_KV, VT_ROWS, BLOCK), BF16),
            pltpu.VMEM((2, N_HEADS + 1, WINDOW, BLOCK), F32),
            pltpu.VMEM((N_HEADS, WK - 2 * WINDOW, WQ), F32),
            pltpu.VMEM((WK, N_HEADS // N_KV * WQ), F32),
            pltpu.VMEM((WK, N_HEADS // N_KV * WQ), F32),
            pltpu.VMEM((tq // WQ * N_HEADS, HEAD_DIM, WQ), F32),
        ],
        compiler_params=pltpu.CompilerParams(
            dimension_semantics=("arbitrary", "arbitrary"),
            vmem_limit_bytes=VMEM_LIMIT),
        name="wattn",
    )(q, k, vt3, bucket, rel_bias, sink)


def _mix_kernel(x_ref, ya_ref, yb_ref, gpre_ref, wg_ref, bg_ref, wa_ref, wb_ref, wo_ref, gpost_ref, o_ref):
    x = x_ref[...]
    h = _rmsnorm(x, gpre_ref[...]).astype(BF16)
    z = jnp.dot(h, wg_ref[...], preferred_element_type=F32) + bg_ref[...]
    gates = 1.0 / (1.0 + jnp.exp(-z))
    a = jnp.dot(ya_ref[...], wa_ref[...], preferred_element_type=F32)
    b = jnp.dot(yb_ref[...], wb_ref[...], preferred_element_type=F32)
    mix = gates[:, :D_MODEL] * a + gates[:, D_MODEL:] * b
    o = jnp.dot(mix.astype(BF16), wo_ref[...], preferred_element_type=F32)
    o_ref[...] = x + _rmsnorm(o, gpost_ref[...])


def _mix_call(x2d, ya, yb, g_pre, w_gate, b_gate, w_a, w_b, w_out, g_post):
    n_tok = x2d.shape[0]
    tm = TM_MIX
    return pl.pallas_call(
        _mix_kernel,
        grid=(n_tok // tm,),
        in_specs=[
            pl.BlockSpec((tm, D_MODEL), lambda i: (i, 0)),
            pl.BlockSpec((tm, Q_WIDTH), lambda i: (i, 0)),
            pl.BlockSpec((tm, Q_WIDTH), lambda i: (i, 0)),
            _const_spec((1, D_MODEL)),
            _const_spec((D_MODEL, 2 * D_MODEL)),
            _const_spec((1, 2 * D_MODEL)),
            _const_spec((Q_WIDTH, D_MODEL)),
            _const_spec((Q_WIDTH, D_MODEL)),
            _const_spec((D_MODEL, D_MODEL)),
            _const_spec((1, D_MODEL)),
        ],
        out_specs=pl.BlockSpec((tm, D_MODEL), lambda i: (i, 0)),
        out_shape=jax.ShapeDtypeStruct((n_tok, D_MODEL), F32),
        compiler_params=pltpu.CompilerParams(
            dimension_semantics=("arbitrary",), vmem_limit_bytes=VMEM_LIMIT),
        name="mix",
    )(x2d, ya, yb, g_pre, w_gate, b_gate, w_a, w_b, w_out, g_post)


def _ffn_kernel(x_ref, gpre_ref, wg_ref, wu_ref, wd_ref, gpost_ref, o_ref):
    x = x_ref[...]
    h = _rmsnorm(x, gpre_ref[...]).astype(BF16)
    g = jnp.dot(h, wg_ref[...], preferred_element_type=F32)
    u = jnp.dot(h, wu_ref[...], preferred_element_type=F32)
    act = (g / (1.0 + jnp.exp(-g))) * u
    f = jnp.dot(act.astype(BF16), wd_ref[...], preferred_element_type=F32)
    o_ref[...] = x + _rmsnorm(f, gpost_ref[...])


def _ffn_call(x2d, g_pre, w_g, w_u, w_d, g_post):
    n_tok = x2d.shape[0]
    tm = TM_FFN
    return pl.pallas_call(
        _ffn_kernel,
        grid=(n_tok // tm,),
        in_specs=[
            pl.BlockSpec((tm, D_MODEL), lambda i: (i, 0)),
            _const_spec((1, D_MODEL)),
            _const_spec((D_MODEL, D_FF)),
            _const_spec((D_MODEL, D_FF)),
            _const_spec((D_FF, D_MODEL)),
            _const_spec((1, D_MODEL)),
        ],
        out_specs=pl.BlockSpec((tm, D_MODEL), lambda i: (i, 0)),
        out_shape=jax.ShapeDtypeStruct((n_tok, D_MODEL), F32),
        compiler_params=pltpu.CompilerParams(
            dimension_semantics=("arbitrary",), vmem_limit_bytes=VMEM_LIMIT),
        name="ffn",
    )(x2d, g_pre, w_g, w_u, w_d, g_post)


def _rope_tables(seq):
    pos = np.arange(seq)
    rows = (pos // GRID_W).astype(np.float32)
    cols = (pos % GRID_W).astype(np.float32)
    inv_freq = (np.float32(1.0) / np.power(
        np.float32(ROPE_THETA), np.arange(0, ROPE_HALF, 2, dtype=np.float32) / np.float32(ROPE_HALF)))
    lane = np.arange(LANES)
    d = lane % HEAD_DIM
    use_col = (d // ROPE_HALF) == 1
    j = d % ROPE_HALF
    f_idx = j % ROPE_QUARTER
    coord = np.where(use_col[None, :], cols[:, None], rows[:, None])
    ang = (coord * inv_freq.astype(np.float32)[f_idx][None, :]).astype(np.float32).astype(np.float64)
    cos, sin = np.cos(ang), np.sin(ang)
    first = (j < ROPE_QUARTER)[None, :]
    sin_up = np.where(first, -sin, 0.0)
    sin_dn = np.where(first, 0.0, sin)
    return jnp.asarray(np.stack([cos, sin_up, sin_dn]).astype(np.float32))


def _layer(x, bucket, bd, p):
    batch, seq, _ = x.shape
    assert seq % TM_PRE == 0 and seq % (TQ_GLOBAL * GLOBAL_TILES_PER_STEP) == 0
    assert seq % TQ_WINDOW == 0 and seq // WQ >= 2
    x2d = x.reshape(batch * seq, D_MODEL)
    qa, ka, vat, qb, kb, vbt = _pre_call(x2d, seq, p["g_mix_pre"], p["w_in"], p["qk_gain"],
                                         _rope_tables(seq), bd)
    ya = _gattn_call(qa, ka, vat, batch, seq)
    yb = _wattn_call(qb, kb, vbt, batch, seq, bucket, p["rel_bias"], p["sink"])
    x1 = _mix_call(x2d, ya, yb, p["g_mix_pre"], p["w_gate"], p["b_gate"], p["w_a"], p["w_b"], p["w_out"],
                   p["g_mix_post"])
    y = _ffn_call(x1, p["g_ffn_pre"], p["w_ffn_gate"], p["w_ffn_up"], p["w_ffn_down"], p["g_ffn_post"])
    return y.reshape(batch, seq, D_MODEL)


def kernel(x_prompt, x_sample, norm_mix_pre, norm_mix_post, w_in, q_norm_a, k_norm_a, sink_b, rel_bias,
           w_branch_a, w_branch_b, w_gate, b_gate, w_out, norm_ffn_pre, norm_ffn_post, w_ffn_gate,
           w_ffn_up, w_ffn_down):
    depth = w_in.shape[0]
    bucket = jnp.asarray(_t5_bucket_table())
    blk = np.arange(LANES) // HEAD_DIM
    bd = jnp.asarray(np.tile(blk[:, None] == blk[None, :], (2, 1)), dtype=BF16)
    y_prompt, y_sample = x_prompt, x_sample
    for l in range(depth):
        p = {
            "g_mix_pre": norm_mix_pre[l].reshape(1, D_MODEL),
            "g_mix_post": norm_mix_post[l].reshape(1, D_MODEL),
            "w_in": w_in[l].astype(BF16),
            "qk_gain": jnp.stack([jnp.tile(q_norm_a[l], 2), jnp.tile(k_norm_a[l], 2)]).reshape(2, 1, LANES),
            "sink": sink_b[l],
            "rel_bias": rel_bias,
            "w_a": w_branch_a[l].astype(BF16),
            "w_b": w_branch_b[l].astype(BF16),
            "w_gate": w_gate[l].astype(BF16),
            "b_gate": b_gate[l].reshape(1, 2 * D_MODEL),
            "w_out": w_out[l].astype(BF16),
            "g_ffn_pre": norm_ffn_pre[l].reshape(1, D_MODEL),
            "g_ffn_post": norm_ffn_post[l].reshape(1, D_MODEL),
            "w_ffn_gate": w_ffn_gate[l].astype(BF16),
            "w_ffn_up": w_ffn_up[l].astype(BF16),
            "w_ffn_down": w_ffn_down[l].astype(BF16),
        }
        y_prompt = _layer(y_prompt, bucket, bd, p)
        y_sample = _layer(y_sample, bucket, bd, p)
    return (y_prompt, y_sample)
```

```python
import functools
import math

import numpy as np
import jax
import jax.numpy as jnp
from jax import lax
from jax.experimental import pallas as pl
from jax.experimental.pallas import tpu as pltpu

F32 = jnp.float32
BF16 = jnp.bfloat16

D_MODEL = 1024
HEAD_DIM = 64
N_HEADS = 8
N_KV = 2
Q_WIDTH = N_HEADS * HEAD_DIM
KV_WIDTH = N_KV * HEAD_DIM
IN_WIDTH = 2 * (Q_WIDTH + 2 * KV_WIDTH)
BLOCK = 128
WINDOW = 128
GRID_W = 64
ROPE_THETA = 10000.0
ROPE_HALF = HEAD_DIM // 2
ROPE_QUARTER = ROPE_HALF // 2
N_BUCKETS = 32
MAX_DISTANCE = 128
D_FF = 2816
EPS = 1e-6
NEG_INF = -1e30
LOG2E = 1.4426950408889634
Q_SCALE = LOG2E / math.sqrt(HEAD_DIM)

LANES = 128
SUBLANES = 8
N_PAIRS = N_HEADS // 2
PAIRS_PER_KV = N_PAIRS // N_KV


TM_PRE = 512
TM_MIX = 1024
TM_FFN = 512
ROW_GROUPS = 2
TQ_GLOBAL = 512
TK_SCORE = 512
TK_VALUE = 256
BF16_SUBLANES = 16
VT_ROWS = HEAD_DIM + BF16_SUBLANES
WQ = 256
WK = WQ + 2 * WINDOW
TQ_WINDOW = 2048
VMEM_LIMIT = 56 * 1024 * 1024
GLOBAL_ITER_UNITS = 4
WINDOW_ITER_UNITS = 4
GLOBAL_TILES_PER_STEP = 4

NT = (((1,), (1,)), ((), ()))


def _rmsnorm(x, gain):
    var = jnp.mean(x * x, axis=-1, keepdims=True)
    return x * lax.rsqrt(var + EPS) * gain


def _const_spec(shape):
    zeros = (0,) * len(shape)
    return pl.BlockSpec(shape, lambda *_: zeros, pipeline_mode=pl.Buffered(1))


def _two_stage(n_tiles, stage_a, stage_b, buf_even, buf_odd):
    i = pl.program_id(0)
    steady = (i > 0) & (i < n_tiles)

    @pl.when(i == 0)
    def _():
        stage_a(buf_even)

    @pl.when(steady & (i % 2 == 1))
    def _():
        stage_b(buf_even)
        stage_a(buf_odd)

    @pl.when(steady & (i % 2 == 0))
    def _():
        stage_b(buf_odd)
        stage_a(buf_even)

    @pl.when(i == n_tiles)
    def _():
        stage_b(buf_even if (n_tiles - 1) % 2 == 0 else buf_odd)


def _cur_tile(n_tiles):
    return lambda i: (jnp.minimum(i, n_tiles - 1), 0)


def _done_tile(i):
    return jnp.maximum(i - 1, 0)


def _pre_kernel(n_tiles, x_ref, g_ref, w_ref, qkg_ref, rope_ref, bd_ref,
                qa_ref, ka_ref, vat_ref, qb_ref, kb_ref, vbt_ref, proj_even, proj_odd):
    tm = x_ref.shape[0]

    def project(proj_ref):
        h = _rmsnorm(x_ref[...], g_ref[...]).astype(BF16)
        proj_ref[...] = jnp.dot(h, w_ref[...], preferred_element_type=F32)

    def finish(proj_ref):
        bd = bd_ref[...]
        cos, sin_up, sin_dn = rope_ref[0], rope_ref[1], rope_ref[2]
        lane = lax.broadcasted_iota(jnp.int32, (tm, LANES), 1)
        lo_half = lane < HEAD_DIM

        def slab(col):
            return proj_ref[:, col * LANES:(col + 1) * LANES]

        def qk_norm_rope(z, gain):
            sq = z * z
            hi = sq.astype(BF16)
            lo = (sq - hi.astype(F32)).astype(BF16)
            ssq = jnp.dot(jnp.concatenate([hi, lo], axis=1), bd, preferred_element_type=F32)
            zn = z * lax.rsqrt(ssq * (1.0 / HEAD_DIM) + EPS) * gain
            return (zn * cos
                    + pltpu.roll(zn, LANES - ROPE_QUARTER, 1) * sin_up
                    + pltpu.roll(zn, ROPE_QUARTER, 1) * sin_dn)

        def put_k(ref, z):
            swapped = pltpu.roll(z, HEAD_DIM, 1)
            ref[0, 0] = jnp.where(lo_half, z, 0.0).astype(BF16)
            ref[0, 1] = jnp.where(lo_half, 0.0, swapped).astype(BF16)
            ref[1, 0] = jnp.where(lo_half, swapped, 0.0).astype(BF16)
            ref[1, 1] = jnp.where(lo_half, 0.0, z).astype(BF16)

        def vt_with_ones(v):
            vt = v.T.astype(BF16)
            ones = jnp.ones((VT_ROWS - HEAD_DIM, tm), BF16)
            return jnp.concatenate([vt[:HEAD_DIM], ones, vt[HEAD_DIM:], ones], axis=0)

        for s in range(N_PAIRS):
            qa_ref[s] = (qk_norm_rope(slab(s), qkg_ref[0]) * Q_SCALE).astype(BF16)
        put_k(ka_ref, qk_norm_rope(slab(N_PAIRS), qkg_ref[1]))
        vat_ref[...] = vt_with_ones(slab(N_PAIRS + 1)).reshape(N_KV, VT_ROWS, tm)
        base = N_PAIRS + 2
        for s in range(N_PAIRS):
            qb_ref[s] = (slab(base + s) * Q_SCALE).astype(BF16)
        put_k(kb_ref, slab(base + N_PAIRS))
        vbt = vt_with_ones(slab(base + N_PAIRS + 1))
        for j in range(tm // BLOCK):
            vbt_ref[j] = vbt[:, j * BLOCK:(j + 1) * BLOCK]

    _two_stage(n_tiles, project, finish, proj_even, proj_odd)


def _pre_call(x2d, seq, g_pre, w_in, qk_gain, rope, bd):
    n_tok = x2d.shape[0]
    tm = TM_PRE
    n_tiles = n_tok // tm
    tiles_per_seq = seq // tm

    done = _done_tile
    q_spec = pl.BlockSpec((N_PAIRS, tm, LANES), lambda i: (0, done(i), 0))
    k_spec = pl.BlockSpec((N_KV, 2, tm, LANES), lambda i: (0, 0, done(i), 0))
    q_shape = jax.ShapeDtypeStruct((N_PAIRS, n_tok, LANES), BF16)
    k_shape = jax.ShapeDtypeStruct((N_KV, 2, n_tok, LANES), BF16)
    return pl.pallas_call(
        functools.partial(_pre_kernel, n_tiles),
        grid=(n_tiles + 1,),
        in_specs=[
            pl.BlockSpec((tm, D_MODEL), _cur_tile(n_tiles)),
            _const_spec((1, D_MODEL)),
            _const_spec((D_MODEL, IN_WIDTH)),
            _const_spec((2, 1, LANES)),
            pl.BlockSpec((3, tm, LANES), lambda i: (0, done(i) % tiles_per_seq, 0)),
            _const_spec((2 * LANES, LANES)),
        ],
        out_specs=[
            q_spec, k_spec,
            pl.BlockSpec((N_KV, VT_ROWS, tm), lambda i: (0, 0, done(i))),
            q_spec, k_spec,
            pl.BlockSpec((tm // BLOCK, N_KV * VT_ROWS, BLOCK), lambda i: (done(i), 0, 0)),
        ],
        out_shape=[
            q_shape, k_shape,
            jax.ShapeDtypeStruct((N_KV, VT_ROWS, n_tok), BF16),
            q_shape, k_shape,
            jax.ShapeDtypeStruct((n_tok // BLOCK, N_KV * VT_ROWS, BLOCK), BF16),
        ],
        scratch_shapes=[pltpu.VMEM((tm, IN_WIDTH), F32), pltpu.VMEM((tm, IN_WIDTH), F32)],
        compiler_params=pltpu.CompilerParams(
            dimension_semantics=("arbitrary",), vmem_limit_bytes=VMEM_LIMIT),
        name="pre",
    )(x2d, g_pre, w_in, qk_gain, rope, bd)


def _fold8(x, op):
    return op(x.reshape(x.shape[0] // SUBLANES, SUBLANES, x.shape[1]), axis=0)


def _softmax_units(n_units, group, iter_units, n_keys, tk_a, tk_b, score_chunk, value_chunk, s_scr, sink_of,
                   emit, put_scores=None, get_probs=None):
    assert tk_a % tk_b == 0 and n_keys % tk_a == 0
    n_a, b_per_a = n_keys // tk_a, tk_a // tk_b

    def sinks(u):
        vals = [sink_of(u, g) for g in range(group)]
        return None if vals[0] is None else vals

    def sweeps(ua, s_write, ub, m_b, s_read):
        mx = None
        acc = [None] * group
        for ca in range(n_a):
            if ua is not None:
                if put_scores is None:
                    st = jnp.concatenate([score_chunk(ua, g, ca * tk_a, tk_a) for g in range(group)], axis=1)
                    s_write[ca * tk_a:(ca + 1) * tk_a, :] = st
                    cm = _fold8(st, jnp.max)
                else:
                    assert n_a == 1
                    cm = put_scores(ua, s_write)
                mx = cm if mx is None else jnp.maximum(mx, cm)
            if ub is not None:
                for cb in range(ca * b_per_a, (ca + 1) * b_per_a):
                    if get_probs is None:
                        p = jnp.exp2(s_read[cb * tk_b:(cb + 1) * tk_b, :] - m_b).astype(BF16)
                    else:
                        p = get_probs(s_read, cb * tk_b, tk_b, m_b)
                    vt = value_chunk(ub, cb * tk_b, tk_b)
                    tq = p.shape[1] // group
                    for g in range(group):
                        pv = jnp.dot(vt, p[:, g * tq:(g + 1) * tq], preferred_element_type=F32)
                        acc[g] = pv if acc[g] is None else acc[g] + pv
        if ub is not None:
            sink = sinks(ub)
            for g in range(group):
                denom = acc[g][HEAD_DIM:HEAD_DIM + 1]
                if sink is not None:
                    tq = denom.shape[1]
                    denom = denom + jnp.exp2(sink[g] - m_b[:, g * tq:(g + 1) * tq])
                emit(ub, g, acc[g][:HEAD_DIM] / denom)
        if ua is None:
            return None
        m = jnp.max(mx, axis=0, keepdims=True)
        sink = sinks(ua)
        if sink is not None:
            tq = m.shape[1] // group
            m = jnp.concatenate([jnp.maximum(m[:, g * tq:(g + 1) * tq], sink[g]) for g in range(group)],
                                axis=1)
        return m

    def sub_steps(k0, count, m):
        for d in range(count):
            m = sweeps(k0 + d + 1, s_scr[(d + 1) % 2], k0 + d, m, s_scr[d % 2])
        return m

    assert iter_units % 2 == 0
    n_sub = n_units - 1
    trips = n_sub // iter_units + jnp.minimum(pl.program_id(0), 0)
    m = lax.fori_loop(0, trips, lambda t, m: sub_steps(iter_units * t, iter_units, m),
                      sweeps(0, s_scr[0], None, None, None))
    done = n_sub // iter_units * iter_units
    m = sub_steps(done, n_sub - done, m)
    sweeps(None, None, n_units - 1, m, s_scr[(n_units - 1) % 2])


def _gattn_kernel(q_ref, k_ref, vt_ref, o_ref, s_even, s_odd, ot_scr):
    seq = k_ref.shape[2]
    tq = TQ_GLOBAL
    n_tiles = q_ref.shape[1] // tq
    heads_per_kv = N_HEADS // N_KV

    def score_chunk(u, g, row0, tk):
        head = u % N_HEADS
        k = k_ref[head // heads_per_kv, head % 2, row0:row0 + tk, :]
        q = q_ref[head // 2, pl.ds(pl.multiple_of((u // N_HEADS) * tq, tq), tq), :]
        return lax.dot_general(k, q, NT, preferred_element_type=F32)

    def value_chunk(u, row0, tk):
        return vt_ref[(u % N_HEADS) // heads_per_kv, :, row0:row0 + tk]

    def emit(u, g, out):
        ot_scr[u] = out

    _softmax_units(n_tiles * N_HEADS, 1, GLOBAL_ITER_UNITS, seq, TK_SCORE, TK_VALUE, score_chunk, value_chunk,
                   (s_even, s_odd), lambda u, g: None, emit)
    for t in range(n_tiles):
        for pair in range(N_PAIRS):
            u = t * N_HEADS + 2 * pair
            both = jnp.concatenate([ot_scr[u], ot_scr[u + 1]], axis=0)
            o_ref[t * tq:(t + 1) * tq, pair * LANES:(pair + 1) * LANES] = both.T.astype(BF16)


def _gattn_call(q, k, vt, batch, seq):
    tq = TQ_GLOBAL
    rows = tq * GLOBAL_TILES_PER_STEP
    nq = seq // rows
    n_tok = batch * seq
    return pl.pallas_call(
        _gattn_kernel,
        grid=(batch, nq),
        in_specs=[
            pl.BlockSpec((N_PAIRS, rows, LANES), lambda b, i: (0, b * nq + i, 0)),
            pl.BlockSpec((N_KV, 2, seq, LANES), lambda b, i: (0, 0, b, 0)),
            pl.BlockSpec((N_KV, VT_ROWS, seq), lambda b, i: (0, 0, b)),
        ],
        out_specs=pl.BlockSpec((rows, Q_WIDTH), lambda b, i: (b * nq + i, 0)),
        out_shape=jax.ShapeDtypeStruct((n_tok, Q_WIDTH), BF16),
        scratch_shapes=[pltpu.VMEM((seq, tq), F32), pltpu.VMEM((seq, tq), F32),
                        pltpu.VMEM((GLOBAL_TILES_PER_STEP * N_HEADS, HEAD_DIM, tq), F32)],
        compiler_params=pltpu.CompilerParams(
            dimension_semantics=("arbitrary", "arbitrary"),
            vmem_limit_bytes=VMEM_LIMIT),
        name="gattn",
    )(q, k, vt)


def _t5_bucket_table():
    half = N_BUCKETS // 2
    max_exact = half // 2
    rel = np.arange(WK)[:, None] - WINDOW - np.arange(WQ)[None, :]
    n = np.abs(rel)
    assert MAX_DISTANCE // max_exact == 16 and half - max_exact == 8
    large = np.zeros_like(n)
    for kk in range(1, 2 * half):
        large += ((n * n) >= (2 ** kk) * max_exact * max_exact)
    large = np.minimum(max_exact + large, half - 1)
    bucket = np.where(rel > 0, half, 0) + np.where(n < max_exact, n, large)
    return bucket.astype(np.int32)


def _wattn_kernel(seq, q_ref, k_ref, vt_ref, bkt_ref, rb_ref, sink_ref, o_ref,
                  kpad, vtpad, bias_edge, bias_mid, s_even, s_odd, ot_scr):
    b = pl.program_id(0)
    i = pl.program_id(1)
    n_q_step = q_ref.shape[1] // WQ
    n_q_seq = seq // WQ
    n_blk_seq = seq // BLOCK
    edge = WINDOW

    assert WINDOW == BLOCK and WQ == 2 * BLOCK and TK_VALUE == 2 * BLOCK
    top_rc = (slice(0, edge), slice(0, BLOCK))
    mid_rc = (slice(edge, WK - edge), slice(0, WQ))
    bot_rc = (slice(WK - edge, WK), slice(BLOCK, WQ))

    @pl.when((b == 0) & (i == 0))
    def _():
        neg = jnp.full((edge, BLOCK), NEG_INF, F32)
        bias_edge[0, N_HEADS] = neg
        bias_edge[1, N_HEADS] = neg

        def table(rc, head):
            bucket = bkt_ref[rc]
            row = lax.broadcasted_iota(jnp.int32, bucket.shape, 0) + rc[0].start
            col = lax.broadcasted_iota(jnp.int32, bucket.shape, 1) + rc[1].start
            bias = jnp.zeros(bucket.shape, F32)
            for bk in range(N_BUCKETS):
                bias = jnp.where(bucket == bk, rb_ref[bk, head], bias)
            return jnp.where(jnp.abs(row - WINDOW - col) <= WINDOW, bias * LOG2E, NEG_INF)

        for head in range(N_HEADS):
            bias_edge[0, head] = table(top_rc, head)
            bias_mid[head] = table(mid_rc, head)
            bias_edge[1, head] = table(bot_rc, head)

    @pl.when(i == 0)
    def _():
        zk = jnp.zeros((WINDOW, LANES), BF16)
        for kv in range(N_KV):
            for half in range(2):
                kpad[kv, half, 0:WINDOW, :] = zk
                kpad[kv, half, WINDOW:WINDOW + seq, :] = k_ref[kv, half]
                kpad[kv, half, WINDOW + seq:2 * WINDOW + seq, :] = zk
        zv = jnp.zeros((N_KV, VT_ROWS, BLOCK), BF16)
        vtpad[0] = zv
        vtpad[pl.ds(1, n_blk_seq)] = vt_ref[...].reshape(n_blk_seq, N_KV, VT_ROWS, BLOCK)
        vtpad[n_blk_seq + 1] = zv

    group = N_HEADS // N_KV

    def put_scores(u, s_write):
        j, kv = u // N_KV, u % N_KV
        qb = i * n_q_step + j
        cms = []
        for g in range(group):
            head = kv * group + g
            q = q_ref[head // 2, pl.ds(pl.multiple_of(j * WQ, WQ), WQ), :]
            st = lax.dot_general(kpad[kv, g % 2, pl.ds(pl.multiple_of(qb * WQ, WQ), WK), :], q,
                                 NT, preferred_element_type=F32)
            top = st[top_rc] + bias_edge[0, jnp.where(qb == 0, N_HEADS, head)]
            mid = st[mid_rc] + bias_mid[head]
            bot = st[bot_rc] + bias_edge[1, jnp.where(qb == n_q_seq - 1, N_HEADS, head)]
            c0 = g * WQ
            s_write[top_rc[0], c0:c0 + BLOCK] = top
            s_write[mid_rc[0], c0:c0 + WQ] = mid
            s_write[bot_rc[0], c0 + BLOCK:c0 + WQ] = bot
            edge_max = jnp.concatenate([_fold8(top, jnp.max), _fold8(bot, jnp.max)], axis=1)
            cms.append(jnp.maximum(_fold8(mid, jnp.max), edge_max))
        return jnp.concatenate(cms, axis=1)

    def get_probs(s_read, row0, tk, m):
        zeros = jnp.zeros((BLOCK, BLOCK), F32)

        def half_rows(rows, lane0):
            parts = []
            for g in range(group):
                c = g * WQ + lane0
                live = jnp.exp2(s_read[rows, c:c + BLOCK] - m[:, c:c + BLOCK])
                parts += [live, zeros] if lane0 == 0 else [zeros, live]
            return jnp.concatenate(parts, axis=1)

        def full_rows(rows):
            return jnp.exp2(s_read[rows, :] - m)

        if row0 == 0:
            p = [half_rows(top_rc[0], 0), full_rows(slice(edge, tk))]
        else:
            p = [full_rows(slice(row0, WK - edge)), half_rows(bot_rc[0], BLOCK)]
        return jnp.concatenate(p, axis=0).astype(BF16)

    def value_chunk(u, row0, tk):
        j, kv = u // N_KV, u % N_KV
        blk0 = (i * n_q_step + j) * (WQ // BLOCK) + row0 // BLOCK
        return jnp.concatenate([vtpad[blk0 + t, kv] for t in range(tk // BLOCK)], axis=1)

    def emit(u, g, out):
        ot_scr[(u // N_KV) * N_HEADS + (u % N_KV) * group + g] = out

    _softmax_units(n_q_step * N_KV, group, WINDOW_ITER_UNITS, WK, WK, TK_VALUE, None, value_chunk,
                   (s_even, s_odd),
                   lambda u, g: sink_ref[(u % N_KV) * group + g] * LOG2E, emit,
                   put_scores=put_scores, get_probs=get_probs)
    for j in range(n_q_step):
        for pair in range(N_PAIRS):
            u = j * N_HEADS + 2 * pair
            both = jnp.concatenate([ot_scr[u], ot_scr[u + 1]], axis=0)
            o_ref[j * WQ:(j + 1) * WQ, pair * LANES:(pair + 1) * LANES] = both.T.astype(BF16)


def _wattn_call(q, k, vt3, batch, seq, bucket, rel_bias, sink):
    tq = TQ_WINDOW
    nq = seq // tq
    n_tok = batch * seq
    n_blk_seq = seq // BLOCK
    return pl.pallas_call(
        functools.partial(_wattn_kernel, seq),
        grid=(batch, nq),
        in_specs=[
            pl.BlockSpec((N_PAIRS, tq, LANES), lambda b, i: (0, b * nq + i, 0)),
            pl.BlockSpec((N_KV, 2, seq, LANES), lambda b, i: (0, 0, b, 0)),
            pl.BlockSpec((n_blk_seq, N_KV * VT_ROWS, BLOCK), lambda b, i: (b, 0, 0)),
            _const_spec((WK, WQ)),
            pl.BlockSpec(memory_space=pltpu.SMEM),
            pl.BlockSpec(memory_space=pltpu.SMEM),
        ],
        out_specs=pl.BlockSpec((tq, Q_WIDTH), lambda b, i: (b * nq + i, 0)),
        out_shape=jax.ShapeDtypeStruct((n_tok, Q_WIDTH), BF16),
        scratch_shapes=[
            pltpu.VMEM((N_KV, 2, seq + 2 * WINDOW, LANES), BF16),
            pltpu.VMEM((n_blk_seq + 2, N_KV, VT_ROWS, BLOCK), BF16),
            pltpu.VMEM((2, N_HEADS + 1, WINDOW, BLOCK), F32),
            pltpu.VMEM((N_HEADS, WK - 2 * WINDOW, WQ), F32),
            pltpu.VMEM((WK, N_HEADS // N_KV * WQ), F32),
            pltpu.VMEM((WK, N_HEADS // N_KV * WQ), F32),
            pltpu.VMEM((tq // WQ * N_HEADS, HEAD_DIM, WQ), F32),
        ],
        compiler_params=pltpu.CompilerParams(
            dimension_semantics=("arbitrary", "arbitrary"),
            vmem_limit_bytes=VMEM_LIMIT),
        name="wattn",
    )(q, k, vt3, bucket, rel_bias, sink)


def _mix_kernel(x_ref, ya_ref, yb_ref, gpre_ref, wg_ref, bg_ref, wa_ref, wb_ref, wo_ref, gpost_ref, o_ref):
    rows = x_ref.shape[0] // ROW_GROUPS
    for r in range(ROW_GROUPS):
        rs = slice(r * rows, (r + 1) * rows)
        x = x_ref[rs, :]
        h = _rmsnorm(x, gpre_ref[...]).astype(BF16)
        z = jnp.dot(h, wg_ref[...], preferred_element_type=F32) + bg_ref[...]
        gates = 1.0 / (1.0 + jnp.exp(-z))
        a = jnp.dot(ya_ref[rs, :], wa_ref[...], preferred_element_type=F32)
        b = jnp.dot(yb_ref[rs, :], wb_ref[...], preferred_element_type=F32)
        mix = gates[:, :D_MODEL] * a + gates[:, D_MODEL:] * b
        o = jnp.dot(mix.astype(BF16), wo_ref[...], preferred_element_type=F32)
        o_ref[rs, :] = x + _rmsnorm(o, gpost_ref[...])


def _mix_call(x2d, ya, yb, g_pre, w_gate, b_gate, w_a, w_b, w_out, g_post):
    n_tok = x2d.shape[0]
    tm = TM_MIX
    return pl.pallas_call(
        _mix_kernel,
        grid=(n_tok // tm,),
        in_specs=[
            pl.BlockSpec((tm, D_MODEL), lambda i: (i, 0)),
            pl.BlockSpec((tm, Q_WIDTH), lambda i: (i, 0)),
            pl.BlockSpec((tm, Q_WIDTH), lambda i: (i, 0)),
            _const_spec((1, D_MODEL)),
            _const_spec((D_MODEL, 2 * D_MODEL)),
            _const_spec((1, 2 * D_MODEL)),
            _const_spec((Q_WIDTH, D_MODEL)),
            _const_spec((Q_WIDTH, D_MODEL)),
            _const_spec((D_MODEL, D_MODEL)),
            _const_spec((1, D_MODEL)),
        ],
        out_specs=pl.BlockSpec((tm, D_MODEL), lambda i: (i, 0)),
        out_shape=jax.ShapeDtypeStruct((n_tok, D_MODEL), F32),
        compiler_params=pltpu.CompilerParams(
            dimension_semantics=("arbitrary",), vmem_limit_bytes=VMEM_LIMIT),
        name="mix",
    )(x2d, ya, yb, g_pre, w_gate, b_gate, w_a, w_b, w_out, g_post)


def _ffn_kernel(x_ref, gpre_ref, wg_ref, wu_ref, wd_ref, gpost_ref, o_ref):
    rows = x_ref.shape[0] // ROW_GROUPS
    for r in range(ROW_GROUPS):
        rs = slice(r * rows, (r + 1) * rows)
        x = x_ref[rs, :]
        h = _rmsnorm(x, gpre_ref[...]).astype(BF16)
        g = jnp.dot(h, wg_ref[...], preferred_element_type=F32)
        u = jnp.dot(h, wu_ref[...], preferred_element_type=F32)
        act = (g / (1.0 + jnp.exp(-g))) * u
        f = jnp.dot(act.astype(BF16), wd_ref[...], preferred_element_type=F32)
        o_ref[rs, :] = x + _rmsnorm(f, gpost_ref[...])


def _ffn_call(x2d, g_pre, w_g, w_u, w_d, g_post):
    n_tok = x2d.shape[0]
    tm = TM_FFN
    return pl.pallas_call(
        _ffn_kernel,
        grid=(n_tok // tm,),
        in_specs=[
            pl.BlockSpec((tm, D_MODEL), lambda i: (i, 0)),
            _const_spec((1, D_MODEL)),
            _const_spec((D_MODEL, D_FF)),
            _const_spec((D_MODEL, D_FF)),
            _const_spec((D_FF, D_MODEL)),
            _const_spec((1, D_MODEL)),
        ],
        out_specs=pl.BlockSpec((tm, D_MODEL), lambda i: (i, 0)),
        out_shape=jax.ShapeDtypeStruct((n_tok, D_MODEL), F32),
        compiler_params=pltpu.CompilerParams(
            dimension_semantics=("arbitrary",), vmem_limit_bytes=VMEM_LIMIT),
        name="ffn",
    )(x2d, g_pre, w_g, w_u, w_d, g_post)


def _rope_tables(seq):
    pos = np.arange(seq)
    rows = (pos // GRID_W).astype(np.float32)
    cols = (pos % GRID_W).astype(np.float32)
    inv_freq = (np.float32(1.0) / np.power(
        np.float32(ROPE_THETA), np.arange(0, ROPE_HALF, 2, dtype=np.float32) / np.float32(ROPE_HALF)))
    lane = np.arange(LANES)
    d = lane % HEAD_DIM
    use_col = (d // ROPE_HALF) == 1
    j = d % ROPE_HALF
    f_idx = j % ROPE_QUARTER
    coord = np.where(use_col[None, :], cols[:, None], rows[:, None])
    ang = (coord * inv_freq.astype(np.float32)[f_idx][None, :]).astype(np.float32).astype(np.float64)
    cos, sin = np.cos(ang), np.sin(ang)
    first = (j < ROPE_QUARTER)[None, :]
    sin_up = np.where(first, -sin, 0.0)
    sin_dn = np.where(first, 0.0, sin)
    return jnp.asarray(np.stack([cos, sin_up, sin_dn]).astype(np.float32))


def _layer(x, bucket, bd, p):
    batch, seq, _ = x.shape
    assert seq % TM_PRE == 0 and seq % (TQ_GLOBAL * GLOBAL_TILES_PER_STEP) == 0
    assert seq % TQ_WINDOW == 0 and seq // WQ >= 2
    x2d = x.reshape(batch * seq, D_MODEL)
    qa, ka, vat, qb, kb, vbt = _pre_call(x2d, seq, p["g_mix_pre"], p["w_in"], p["qk_gain"],
                                         _rope_tables(seq), bd)
    ya = _gattn_call(qa, ka, vat, batch, seq)
    yb = _wattn_call(qb, kb, vbt, batch, seq, bucket, p["rel_bias"], p["sink"])
    x1 = _mix_call(x2d, ya, yb, p["g_mix_pre"], p["w_gate"], p["b_gate"], p["w_a"], p["w_b"], p["w_out"],
                   p["g_mix_post"])
    y = _ffn_call(x1, p["g_ffn_pre"], p["w_ffn_gate"], p["w_ffn_up"], p["w_ffn_down"], p["g_ffn_post"])
    return y.reshape(batch, seq, D_MODEL)


def kernel(x_prompt, x_sample, norm_mix_pre, norm_mix_post, w_in, q_norm_a, k_norm_a, sink_b, rel_bias,
           w_branch_a, w_branch_b, w_gate, b_gate, w_out, norm_ffn_pre, norm_ffn_post, w_ffn_gate,
           w_ffn_up, w_ffn_down):
    depth = w_in.shape[0]
    bucket = jnp.asarray(_t5_bucket_table())
    blk = np.arange(LANES) // HEAD_DIM
    bd = jnp.asarray(np.tile(blk[:, None] == blk[None, :], (2, 1)), dtype=BF16)
    y_prompt, y_sample = x_prompt, x_sample
    for l in range(depth):
        p = {
            "g_mix_pre": norm_mix_pre[l].reshape(1, D_MODEL),
            "g_mix_post": norm_mix_post[l].reshape(1, D_MODEL),
            "w_in": w_in[l].astype(BF16),
            "qk_gain": jnp.stack([jnp.tile(q_norm_a[l], 2), jnp.tile(k_norm_a[l], 2)]).reshape(2, 1, LANES),
            "sink": sink_b[l],
            "rel_bias": rel_bias,
            "w_a": w_branch_a[l].astype(BF16),
            "w_b": w_branch_b[l].astype(BF16),
            "w_gate": w_gate[l].astype(BF16),
            "b_gate": b_gate[l].reshape(1, 2 * D_MODEL),
            "w_out": w_out[l].astype(BF16),
            "g_ffn_pre": norm_ffn_pre[l].reshape(1, D_MODEL),
            "g_ffn_post": norm_ffn_post[l].reshape(1, D_MODEL),
            "w_ffn_gate": w_ffn_gate[l].astype(BF16),
            "w_ffn_up": w_ffn_up[l].astype(BF16),
            "w_ffn_down": w_ffn_down[l].astype(BF16),
        }
        y_prompt = _layer(y_prompt, bucket, bd, p)
        y_sample = _layer(y_sample, bucket, bd, p)
    return (y_prompt, y_sample)
```

```python
import functools
import math

import numpy as np
import jax
import jax.numpy as jnp
from jax import lax
from jax.experimental import pallas as pl
from jax.experimental.pallas import tpu as pltpu

F32 = jnp.float32
BF16 = jnp.bfloat16

D_MODEL = 1024
HEAD_DIM = 64
N_HEADS = 8
N_KV = 2
Q_WIDTH = N_HEADS * HEAD_DIM
KV_WIDTH = N_KV * HEAD_DIM
IN_WIDTH = 2 * (Q_WIDTH + 2 * KV_WIDTH)
BLOCK = 128
WINDOW = 128
GRID_W = 64
ROPE_THETA = 10000.0
ROPE_HALF = HEAD_DIM // 2
ROPE_QUARTER = ROPE_HALF // 2
N_BUCKETS = 32
MAX_DISTANCE = 128
D_FF = 2816
EPS = 1e-6
NEG_INF = -1e30
LOG2E = 1.4426950408889634
Q_SCALE = LOG2E / math.sqrt(HEAD_DIM)

LANES = 128
SUBLANES = 8
N_PAIRS = N_HEADS // 2
PAIRS_PER_KV = N_PAIRS // N_KV


TM_PRE = 512
TM_MIX = 1024
TM_FFN = 512
FFN_ROW_GROUPS = 2
TQ_GLOBAL = 512
TK_SCORE = 512
TK_VALUE = 256
BF16_SUBLANES = 16
VT_ROWS = HEAD_DIM + BF16_SUBLANES
WQ = 256
WK = WQ + 2 * WINDOW
TQ_WINDOW = 2048
VMEM_LIMIT = 56 * 1024 * 1024
GLOBAL_ITER_UNITS = 4
WINDOW_ITER_UNITS = 4
GLOBAL_TILES_PER_STEP = 4

NT = (((1,), (1,)), ((), ()))


def _rmsnorm(x, gain):
    var = jnp.mean(x * x, axis=-1, keepdims=True)
    return x * lax.rsqrt(var + EPS) * gain


def _const_spec(shape):
    zeros = (0,) * len(shape)
    return pl.BlockSpec(shape, lambda *_: zeros, pipeline_mode=pl.Buffered(1))


def _two_stage(n_tiles, stage_a, stage_b, buf_even, buf_odd):
    i = pl.program_id(0)
    steady = (i > 0) & (i < n_tiles)

    @pl.when(i == 0)
    def _():
        stage_a(buf_even)

    @pl.when(steady & (i % 2 == 1))
    def _():
        stage_b(buf_even)
        stage_a(buf_odd)

    @pl.when(steady & (i % 2 == 0))
    def _():
        stage_b(buf_odd)
        stage_a(buf_even)

    @pl.when(i == n_tiles)
    def _():
        stage_b(buf_even if (n_tiles - 1) % 2 == 0 else buf_odd)


def _cur_tile(n_tiles):
    return lambda i: (jnp.minimum(i, n_tiles - 1), 0)


def _done_tile(i):
    return jnp.maximum(i - 1, 0)


def _pre_kernel(n_tiles, x_ref, g_ref, w_ref, qkg_ref, rope_ref, bd_ref,
                qa_ref, ka_ref, vat_ref, qb_ref, kb_ref, vbt_ref, proj_even, proj_odd):
    tm = x_ref.shape[0]

    def project(proj_ref):
        h = _rmsnorm(x_ref[...], g_ref[...]).astype(BF16)
        proj_ref[...] = jnp.dot(h, w_ref[...], preferred_element_type=F32)

    def finish(proj_ref):
        bd = bd_ref[...]
        cos, sin_up, sin_dn = rope_ref[0], rope_ref[1], rope_ref[2]
        lane = lax.broadcasted_iota(jnp.int32, (tm, LANES), 1)
        lo_half = lane < HEAD_DIM

        def slab(col):
            return proj_ref[:, col * LANES:(col + 1) * LANES]

        def qk_norm_rope(z, gain):
            sq = z * z
            hi = sq.astype(BF16)
            lo = (sq - hi.astype(F32)).astype(BF16)
            ssq = jnp.dot(jnp.concatenate([hi, lo], axis=1), bd, preferred_element_type=F32)
            zn = z * lax.rsqrt(ssq * (1.0 / HEAD_DIM) + EPS) * gain
            return (zn * cos
                    + pltpu.roll(zn, LANES - ROPE_QUARTER, 1) * sin_up
                    + pltpu.roll(zn, ROPE_QUARTER, 1) * sin_dn)

        def put_k(ref, z):
            swapped = pltpu.roll(z, HEAD_DIM, 1)
            ref[0, 0] = jnp.where(lo_half, z, 0.0).astype(BF16)
            ref[0, 1] = jnp.where(lo_half, 0.0, swapped).astype(BF16)
            ref[1, 0] = jnp.where(lo_half, swapped, 0.0).astype(BF16)
            ref[1, 1] = jnp.where(lo_half, 0.0, z).astype(BF16)

        def vt_with_ones(v):
            vt = v.T.astype(BF16)
            ones = jnp.ones((VT_ROWS - HEAD_DIM, tm), BF16)
            return jnp.concatenate([vt[:HEAD_DIM], ones, vt[HEAD_DIM:], ones], axis=0)

        for s in range(N_PAIRS):
            qa_ref[s] = (qk_norm_rope(slab(s), qkg_ref[0]) * Q_SCALE).astype(BF16)
        put_k(ka_ref, qk_norm_rope(slab(N_PAIRS), qkg_ref[1]))
        vat_ref[...] = vt_with_ones(slab(N_PAIRS + 1)).reshape(N_KV, VT_ROWS, tm)
        base = N_PAIRS + 2
        for s in range(N_PAIRS):
            qb_ref[s] = (slab(base + s) * Q_SCALE).astype(BF16)
        put_k(kb_ref, slab(base + N_PAIRS))
        vbt = vt_with_ones(slab(base + N_PAIRS + 1))
        for j in range(tm // BLOCK):
            vbt_ref[j] = vbt[:, j * BLOCK:(j + 1) * BLOCK]

    _two_stage(n_tiles, project, finish, proj_even, proj_odd)


def _pre_call(x2d, seq, g_pre, w_in, qk_gain, rope, bd):
    n_tok = x2d.shape[0]
    tm = TM_PRE
    n_tiles = n_tok // tm
    tiles_per_seq = seq // tm

    done = _done_tile
    q_spec = pl.BlockSpec((N_PAIRS, tm, LANES), lambda i: (0, done(i), 0))
    k_spec = pl.BlockSpec((N_KV, 2, tm, LANES), lambda i: (0, 0, done(i), 0))
    q_shape = jax.ShapeDtypeStruct((N_PAIRS, n_tok, LANES), BF16)
    k_shape = jax.ShapeDtypeStruct((N_KV, 2, n_tok, LANES), BF16)
    return pl.pallas_call(
        functools.partial(_pre_kernel, n_tiles),
        grid=(n_tiles + 1,),
        in_specs=[
            pl.BlockSpec((tm, D_MODEL), _cur_tile(n_tiles)),
            _const_spec((1, D_MODEL)),
            _const_spec((D_MODEL, IN_WIDTH)),
            _const_spec((2, 1, LANES)),
            pl.BlockSpec((3, tm, LANES), lambda i: (0, done(i) % tiles_per_seq, 0)),
            _const_spec((2 * LANES, LANES)),
        ],
        out_specs=[
            q_spec, k_spec,
            pl.BlockSpec((N_KV, VT_ROWS, tm), lambda i: (0, 0, done(i))),
            q_spec, k_spec,
            pl.BlockSpec((tm // BLOCK, N_KV * VT_ROWS, BLOCK), lambda i: (done(i), 0, 0)),
        ],
        out_shape=[
            q_shape, k_shape,
            jax.ShapeDtypeStruct((N_KV, VT_ROWS, n_tok), BF16),
            q_shape, k_shape,
            jax.ShapeDtypeStruct((n_tok // BLOCK, N_KV * VT_ROWS, BLOCK), BF16),
        ],
        scratch_shapes=[pltpu.VMEM((tm, IN_WIDTH), F32), pltpu.VMEM((tm, IN_WIDTH), F32)],
        compiler_params=pltpu.CompilerParams(
            dimension_semantics=("arbitrary",), vmem_limit_bytes=VMEM_LIMIT),
        name="pre",
    )(x2d, g_pre, w_in, qk_gain, rope, bd)


def _fold8(x, op):
    return op(x.reshape(x.shape[0] // SUBLANES, SUBLANES, x.shape[1]), axis=0)


def _softmax_units(n_units, group, iter_units, n_keys, tk_a, tk_b, score_chunk, value_chunk, s_scr, sink_of,
                   emit, put_scores=None, get_probs=None):
    assert tk_a % tk_b == 0 and n_keys % tk_a == 0
    n_a, b_per_a = n_keys // tk_a, tk_a // tk_b

    def sinks(u):
        vals = [sink_of(u, g) for g in range(group)]
        return None if vals[0] is None else vals

    def sweeps(ua, s_write, ub, m_b, s_read):
        mx = None
        acc = [None] * group
        for ca in range(n_a):
            if ua is not None:
                if put_scores is None:
                    st = jnp.concatenate([score_chunk(ua, g, ca * tk_a, tk_a) for g in range(group)], axis=1)
                    s_write[ca * tk_a:(ca + 1) * tk_a, :] = st
                    cm = _fold8(st, jnp.max)
                else:
                    assert n_a == 1
                    cm = put_scores(ua, s_write)
                mx = cm if mx is None else jnp.maximum(mx, cm)
            if ub is not None:
                for cb in range(ca * b_per_a, (ca + 1) * b_per_a):
                    if get_probs is None:
                        p = jnp.exp2(s_read[cb * tk_b:(cb + 1) * tk_b, :] - m_b).astype(BF16)
                    else:
                        p = get_probs(s_read, cb * tk_b, tk_b, m_b)
                    vt = value_chunk(ub, cb * tk_b, tk_b)
                    tq = p.shape[1] // group
                    for g in range(group):
                        pv = jnp.dot(vt, p[:, g * tq:(g + 1) * tq], preferred_element_type=F32)
                        acc[g] = pv if acc[g] is None else acc[g] + pv
        if ub is not None:
            sink = sinks(ub)
            for g in range(group):
                denom = acc[g][HEAD_DIM:HEAD_DIM + 1]
                if sink is not None:
                    tq = denom.shape[1]
                    denom = denom + jnp.exp2(sink[g] - m_b[:, g * tq:(g + 1) * tq])
                emit(ub, g, acc[g][:HEAD_DIM] / denom)
        if ua is None:
            return None
        m = jnp.max(mx, axis=0, keepdims=True)
        sink = sinks(ua)
        if sink is not None:
            tq = m.shape[1] // group
            m = jnp.concatenate([jnp.maximum(m[:, g * tq:(g + 1) * tq], sink[g]) for g in range(group)],
                                axis=1)
        return m

    def sub_steps(k0, count, m):
        for d in range(count):
            m = sweeps(k0 + d + 1, s_scr[(d + 1) % 2], k0 + d, m, s_scr[d % 2])
        return m

    assert iter_units % 2 == 0
    n_sub = n_units - 1
    trips = n_sub // iter_units + jnp.minimum(pl.program_id(0), 0)
    m = lax.fori_loop(0, trips, lambda t, m: sub_steps(iter_units * t, iter_units, m),
                      sweeps(0, s_scr[0], None, None, None))
    done = n_sub // iter_units * iter_units
    m = sub_steps(done, n_sub - done, m)
    sweeps(None, None, n_units - 1, m, s_scr[(n_units - 1) % 2])


def _gattn_kernel(q_ref, k_ref, vt_ref, o_ref, s_even, s_odd, ot_scr):
    seq = k_ref.shape[2]
    tq = TQ_GLOBAL
    n_tiles = q_ref.shape[1] // tq
    heads_per_kv = N_HEADS // N_KV

    def score_chunk(u, g, row0, tk):
        head = u % N_HEADS
        k = k_ref[head // heads_per_kv, head % 2, row0:row0 + tk, :]
        q = q_ref[head // 2, pl.ds(pl.multiple_of((u // N_HEADS) * tq, tq), tq), :]
        return lax.dot_general(k, q, NT, preferred_element_type=F32)

    def value_chunk(u, row0, tk):
        return vt_ref[(u % N_HEADS) // heads_per_kv, :, row0:row0 + tk]

    def emit(u, g, out):
        ot_scr[u] = out

    _softmax_units(n_tiles * N_HEADS, 1, GLOBAL_ITER_UNITS, seq, TK_SCORE, TK_VALUE, score_chunk, value_chunk,
                   (s_even, s_odd), lambda u, g: None, emit)
    for t in range(n_tiles):
        for pair in range(N_PAIRS):
            u = t * N_HEADS + 2 * pair
            both = jnp.concatenate([ot_scr[u], ot_scr[u + 1]], axis=0)
            o_ref[t * tq:(t + 1) * tq, pair * LANES:(pair + 1) * LANES] = both.T.astype(BF16)


def _gattn_call(q, k, vt, batch, seq):
    tq = TQ_GLOBAL
    rows = tq * GLOBAL_TILES_PER_STEP
    nq = seq // rows
    n_tok = batch * seq
    return pl.pallas_call(
        _gattn_kernel,
        grid=(batch, nq),
        in_specs=[
            pl.BlockSpec((N_PAIRS, rows, LANES), lambda b, i: (0, b * nq + i, 0)),
            pl.BlockSpec((N_KV, 2, seq, LANES), lambda b, i: (0, 0, b, 0)),
            pl.BlockSpec((N_KV, VT_ROWS, seq), lambda b, i: (0, 0, b)),
        ],
        out_specs=pl.BlockSpec((rows, Q_WIDTH), lambda b, i: (b * nq + i, 0)),
        out_shape=jax.ShapeDtypeStruct((n_tok, Q_WIDTH), BF16),
        scratch_shapes=[pltpu.VMEM((seq, tq), F32), pltpu.VMEM((seq, tq), F32),
                        pltpu.VMEM((GLOBAL_TILES_PER_STEP * N_HEADS, HEAD_DIM, tq), F32)],
        compiler_params=pltpu.CompilerParams(
            dimension_semantics=("arbitrary", "arbitrary"),
            vmem_limit_bytes=VMEM_LIMIT),
        name="gattn",
    )(q, k, vt)


def _t5_bucket_table():
    half = N_BUCKETS // 2
    max_exact = half // 2
    rel = np.arange(WK)[:, None] - WINDOW - np.arange(WQ)[None, :]
    n = np.abs(rel)
    assert MAX_DISTANCE // max_exact == 16 and half - max_exact == 8
    large = np.zeros_like(n)
    for kk in range(1, 2 * half):
        large += ((n * n) >= (2 ** kk) * max_exact * max_exact)
    large = np.minimum(max_exact + large, half - 1)
    bucket = np.where(rel > 0, half, 0) + np.where(n < max_exact, n, large)
    return bucket.astype(np.int32)


def _wattn_kernel(seq, q_ref, k_ref, vt_ref, bkt_ref, rb_ref, sink_ref, o_ref,
                  kpad, vtpad, bias_edge, bias_mid, s_even, s_odd, ot_scr):
    b = pl.program_id(0)
    i = pl.program_id(1)
    n_q_step = q_ref.shape[1] // WQ
    n_q_seq = seq // WQ
    n_blk_seq = seq // BLOCK
    edge = WINDOW

    assert WINDOW == BLOCK and WQ == 2 * BLOCK and TK_VALUE == 2 * BLOCK
    top_rc = (slice(0, edge), slice(0, BLOCK))
    mid_rc = (slice(edge, WK - edge), slice(0, WQ))
    bot_rc = (slice(WK - edge, WK), slice(BLOCK, WQ))

    @pl.when((b == 0) & (i == 0))
    def _():
        neg = jnp.full((edge, BLOCK), NEG_INF, F32)
        bias_edge[0, N_HEADS] = neg
        bias_edge[1, N_HEADS] = neg

        def table(rc, head):
            bucket = bkt_ref[rc]
            row = lax.broadcasted_iota(jnp.int32, bucket.shape, 0) + rc[0].start
            col = lax.broadcasted_iota(jnp.int32, bucket.shape, 1) + rc[1].start
            bias = jnp.zeros(bucket.shape, F32)
            for bk in range(N_BUCKETS):
                bias = jnp.where(bucket == bk, rb_ref[bk, head], bias)
            return jnp.where(jnp.abs(row - WINDOW - col) <= WINDOW, bias * LOG2E, NEG_INF)

        for head in range(N_HEADS):
            bias_edge[0, head] = table(top_rc, head)
            bias_mid[head] = table(mid_rc, head)
            bias_edge[1, head] = table(bot_rc, head)

    @pl.when(i == 0)
    def _():
        zk = jnp.zeros((WINDOW, LANES), BF16)
        for kv in range(N_KV):
            for half in range(2):
                kpad[kv, half, 0:WINDOW, :] = zk
                kpad[kv, half, WINDOW:WINDOW + seq, :] = k_ref[kv, half]
                kpad[kv, half, WINDOW + seq:2 * WINDOW + seq, :] = zk
        zv = jnp.zeros((N_KV, VT_ROWS, BLOCK), BF16)
        vtpad[0] = zv
        vtpad[pl.ds(1, n_blk_seq)] = vt_ref[...].reshape(n_blk_seq, N_KV, VT_ROWS, BLOCK)
        vtpad[n_blk_seq + 1] = zv

    group = N_HEADS // N_KV

    def put_scores(u, s_write):
        j, kv = u // N_KV, u % N_KV
        qb = i * n_q_step + j
        cms = []
        for g in range(group):
            head = kv * group + g
            q = q_ref[head // 2, pl.ds(pl.multiple_of(j * WQ, WQ), WQ), :]
            st = lax.dot_general(kpad[kv, g % 2, pl.ds(pl.multiple_of(qb * WQ, WQ), WK), :], q,
                                 NT, preferred_element_type=F32)
            top = st[top_rc] + bias_edge[0, jnp.where(qb == 0, N_HEADS, head)]
            mid = st[mid_rc] + bias_mid[head]
            bot = st[bot_rc] + bias_edge[1, jnp.where(qb == n_q_seq - 1, N_HEADS, head)]
            c0 = g * WQ
            s_write[top_rc[0], c0:c0 + BLOCK] = top
            s_write[mid_rc[0], c0:c0 + WQ] = mid
            s_write[bot_rc[0], c0 + BLOCK:c0 + WQ] = bot
            edge_max = jnp.concatenate([_fold8(top, jnp.max), _fold8(bot, jnp.max)], axis=1)
            cms.append(jnp.maximum(_fold8(mid, jnp.max), edge_max))
        return jnp.concatenate(cms, axis=1)

    def get_probs(s_read, row0, tk, m):
        zeros = jnp.zeros((BLOCK, BLOCK), F32)

        def half_rows(rows, lane0):
            parts = []
            for g in range(group):
                c = g * WQ + lane0
                live = jnp.exp2(s_read[rows, c:c + BLOCK] - m[:, c:c + BLOCK])
                parts += [live, zeros] if lane0 == 0 else [zeros, live]
            return jnp.concatenate(parts, axis=1)

        def full_rows(rows):
            return jnp.exp2(s_read[rows, :] - m)

        if row0 == 0:
            p = [half_rows(top_rc[0], 0), full_rows(slice(edge, tk))]
        else:
            p = [full_rows(slice(row0, WK - edge)), half_rows(bot_rc[0], BLOCK)]
        return jnp.concatenate(p, axis=0).astype(BF16)

    def value_chunk(u, row0, tk):
        j, kv = u // N_KV, u % N_KV
        blk0 = (i * n_q_step + j) * (WQ // BLOCK) + row0 // BLOCK
        return jnp.concatenate([vtpad[blk0 + t, kv] for t in range(tk // BLOCK)], axis=1)

    def emit(u, g, out):
        ot_scr[(u // N_KV) * N_HEADS + (u % N_KV) * group + g] = out

    _softmax_units(n_q_step * N_KV, group, WINDOW_ITER_UNITS, WK, WK, TK_VALUE, None, value_chunk,
                   (s_even, s_odd),
                   lambda u, g: sink_ref[(u % N_KV) * group + g] * LOG2E, emit,
                   put_scores=put_scores, get_probs=get_probs)
    for j in range(n_q_step):
        for pair in range(N_PAIRS):
            u = j * N_HEADS + 2 * pair
            both = jnp.concatenate([ot_scr[u], ot_scr[u + 1]], axis=0)
            o_ref[j * WQ:(j + 1) * WQ, pair * LANES:(pair + 1) * LANES] = both.T.astype(BF16)


def _wattn_call(q, k, vt3, batch, seq, bucket, rel_bias, sink):
    tq = TQ_WINDOW
    nq = seq // tq
    n_tok = batch * seq
    n_blk_seq = seq // BLOCK
    return pl.pallas_call(
        functools.partial(_wattn_kernel, seq),
        grid=(batch, nq),
        in_specs=[
            pl.BlockSpec((N_PAIRS, tq, LANES), lambda b, i: (0, b * nq + i, 0)),
            pl.BlockSpec((N_KV, 2, seq, LANES), lambda b, i: (0, 0, b, 0)),
            pl.BlockSpec((n_blk_seq, N_KV * VT_ROWS, BLOCK), lambda b, i: (b, 0, 0)),
            _const_spec((WK, WQ)),
            pl.BlockSpec(memory_space=pltpu.SMEM),
            pl.BlockSpec(memory_space=pltpu.SMEM),
        ],
        out_specs=pl.BlockSpec((tq, Q_WIDTH), lambda b, i: (b * nq + i, 0)),
        out_shape=jax.ShapeDtypeStruct((n_tok, Q_WIDTH), BF16),
        scratch_shapes=[
            pltpu.VMEM((N_KV, 2, seq + 2 * WINDOW, LANES), BF16),
            pltpu.VMEM((n_blk_seq + 2, N_KV, VT_ROWS, BLOCK), BF16),
            pltpu.VMEM((2, N_HEADS + 1, WINDOW, BLOCK), F32),
            pltpu.VMEM((N_HEADS, WK - 2 * WINDOW, WQ), F32),
            pltpu.VMEM((WK, N_HEADS // N_KV * WQ), F32),
            pltpu.VMEM((WK, N_HEADS // N_KV * WQ), F32),
            pltpu.VMEM((tq // WQ * N_HEADS, HEAD_DIM, WQ), F32),
        ],
        compiler_params=pltpu.CompilerParams(
            dimension_semantics=("arbitrary", "arbitrary"),
            vmem_limit_bytes=VMEM_LIMIT),
        name="wattn",
    )(q, k, vt3, bucket, rel_bias, sink)


def _mix_kernel(x_ref, ya_ref, yb_ref, gpre_ref, wg_ref, bg_ref, wa_ref, wb_ref, wo_ref, gpost_ref, o_ref):
    x = x_ref[...]
    h = _rmsnorm(x, gpre_ref[...]).astype(BF16)
    z = jnp.dot(h, wg_ref[...], preferred_element_type=F32) + bg_ref[...]
    gates = 1.0 / (1.0 + jnp.exp(-z))
    a = jnp.dot(ya_ref[...], wa_ref[...], preferred_element_type=F32)
    b = jnp.dot(yb_ref[...], wb_ref[...], preferred_element_type=F32)
    mix = gates[:, :D_MODEL] * a + gates[:, D_MODEL:] * b
    o = jnp.dot(mix.astype(BF16), wo_ref[...], preferred_element_type=F32)
    o_ref[...] = x + _rmsnorm(o, gpost_ref[...])


def _mix_call(x2d, ya, yb, g_pre, w_gate, b_gate, w_a, w_b, w_out, g_post):
    n_tok = x2d.shape[0]
    tm = TM_MIX
    return pl.pallas_call(
        _mix_kernel,
        grid=(n_tok // tm,),
        in_specs=[
            pl.BlockSpec((tm, D_MODEL), lambda i: (i, 0)),
            pl.BlockSpec((tm, Q_WIDTH), lambda i: (i, 0)),
            pl.BlockSpec((tm, Q_WIDTH), lambda i: (i, 0)),
            _const_spec((1, D_MODEL)),
            _const_spec((D_MODEL, 2 * D_MODEL)),
            _const_spec((1, 2 * D_MODEL)),
            _const_spec((Q_WIDTH, D_MODEL)),
            _const_spec((Q_WIDTH, D_MODEL)),
            _const_spec((D_MODEL, D_MODEL)),
            _const_spec((1, D_MODEL)),
        ],
        out_specs=pl.BlockSpec((tm, D_MODEL), lambda i: (i, 0)),
        out_shape=jax.ShapeDtypeStruct((n_tok, D_MODEL), F32),
        compiler_params=pltpu.CompilerParams(
            dimension_semantics=("arbitrary",), vmem_limit_bytes=VMEM_LIMIT),
        name="mix",
    )(x2d, ya, yb, g_pre, w_gate, b_gate, w_a, w_b, w_out, g_post)


def _ffn_kernel(x_ref, gpre_ref, wg_ref, wu_ref, wd_ref, gpost_ref, o_ref):
    rows = x_ref.shape[0] // FFN_ROW_GROUPS
    for r in range(FFN_ROW_GROUPS):
        rs = slice(r * rows, (r + 1) * rows)
        x = x_ref[rs, :]
        h = _rmsnorm(x, gpre_ref[...]).astype(BF16)
        g = jnp.dot(h, wg_ref[...], preferred_element_type=F32)
        u = jnp.dot(h, wu_ref[...], preferred_element_type=F32)
        act = (g / (1.0 + jnp.exp(-g))) * u
        f = jnp.dot(act.astype(BF16), wd_ref[...], preferred_element_type=F32)
        o_ref[rs, :] = x + _rmsnorm(f, gpost_ref[...])


def _ffn_call(x2d, g_pre, w_g, w_u, w_d, g_post):
    n_tok = x2d.shape[0]
    tm = TM_FFN
    return pl.pallas_call(
        _ffn_kernel,
        grid=(n_tok // tm,),
        in_specs=[
            pl.BlockSpec((tm, D_MODEL), lambda i: (i, 0)),
            _const_spec((1, D_MODEL)),
            _const_spec((D_MODEL, D_FF)),
            _const_spec((D_MODEL, D_FF)),
            _const_spec((D_FF, D_MODEL)),
            _const_spec((1, D_MODEL)),
        ],
        out_specs=pl.BlockSpec((tm, D_MODEL), lambda i: (i, 0)),
        out_shape=jax.ShapeDtypeStruct((n_tok, D_MODEL), F32),
        compiler_params=pltpu.CompilerParams(
            dimension_semantics=("arbitrary",), vmem_limit_bytes=VMEM_LIMIT),
        name="ffn",
    )(x2d, g_pre, w_g, w_u, w_d, g_post)


def _rope_tables(seq):
    pos = np.arange(seq)
    rows = (pos // GRID_W).astype(np.float32)
    cols = (pos % GRID_W).astype(np.float32)
    inv_freq = (np.float32(1.0) / np.power(
        np.float32(ROPE_THETA), np.arange(0, ROPE_HALF, 2, dtype=np.float32) / np.float32(ROPE_HALF)))
    lane = np.arange(LANES)
    d = lane % HEAD_DIM
    use_col = (d // ROPE_HALF) == 1
    j = d % ROPE_HALF
    f_idx = j % ROPE_QUARTER
    coord = np.where(use_col[None, :], cols[:, None], rows[:, None])
    ang = (coord * inv_freq.astype(np.float32)[f_idx][None, :]).astype(np.float32).astype(np.float64)
    cos, sin = np.cos(ang), np.sin(ang)
    first = (j < ROPE_QUARTER)[None, :]
    sin_up = np.where(first, -sin, 0.0)
    sin_dn = np.where(first, 0.0, sin)
    return jnp.asarray(np.stack([cos, sin_up, sin_dn]).astype(np.float32))


def _layer(x, bucket, bd, p):
    batch, seq, _ = x.shape
    assert seq % TM_PRE == 0 and seq % (TQ_GLOBAL * GLOBAL_TILES_PER_STEP) == 0
    assert seq % TQ_WINDOW == 0 and seq // WQ >= 2
    x2d = x.reshape(batch * seq, D_MODEL)
    qa, ka, vat, qb, kb, vbt = _pre_call(x2d, seq, p["g_mix_pre"], p["w_in"], p["qk_gain"],
                                         _rope_tables(seq), bd)
    ya = _gattn_call(qa, ka, vat, batch, seq)
    yb = _wattn_call(qb, kb, vbt, batch, seq, bucket, p["rel_bias"], p["sink"])
    x1 = _mix_call(x2d, ya, yb, p["g_mix_pre"], p["w_gate"], p["b_gate"], p["w_a"], p["w_b"], p["w_out"],
                   p["g_mix_post"])
    y = _ffn_call(x1, p["g_ffn_pre"], p["w_ffn_gate"], p["w_ffn_up"], p["w_ffn_down"], p["g_ffn_post"])
    return y.reshape(batch, seq, D_MODEL)


def kernel(x_prompt, x_sample, norm_mix_pre, norm_mix_post, w_in, q_norm_a, k_norm_a, sink_b, rel_bias,
           w_branch_a, w_branch_b, w_gate, b_gate, w_out, norm_ffn_pre, norm_ffn_post, w_ffn_gate,
           w_ffn_up, w_ffn_down):
    depth = w_in.shape[0]
    bucket = jnp.asarray(_t5_bucket_table())
    blk = np.arange(LANES) // HEAD_DIM
    bd = jnp.asarray(np.tile(blk[:, None] == blk[None, :], (2, 1)), dtype=BF16)
    y_prompt, y_sample = x_prompt, x_sample
    for l in range(depth):
        p = {
            "g_mix_pre": norm_mix_pre[l].reshape(1, D_MODEL),
            "g_mix_post": norm_mix_post[l].reshape(1, D_MODEL),
            "w_in": w_in[l].astype(BF16),
            "qk_gain": jnp.stack([jnp.tile(q_norm_a[l], 2), jnp.tile(k_norm_a[l], 2)]).reshape(2, 1, LANES),
            "sink": sink_b[l],
            "rel_bias": rel_bias,
            "w_a": w_branch_a[l].astype(BF16),
            "w_b": w_branch_b[l].astype(BF16),
            "w_gate": w_gate[l].astype(BF16),
            "b_gate": b_gate[l].reshape(1, 2 * D_MODEL),
            "w_out": w_out[l].astype(BF16),
            "g_ffn_pre": norm_ffn_pre[l].reshape(1, D_MODEL),
            "g_ffn_post": norm_ffn_post[l].reshape(1, D_MODEL),
            "w_ffn_gate": w_ffn_gate[l].astype(BF16),
            "w_ffn_up": w_ffn_up[l].astype(BF16),
            "w_ffn_down": w_ffn_down[l].astype(BF16),
        }
        y_prompt = _layer(y_prompt, bucket, bd, p)
        y_sample = _layer(y_sample, bucket, bd, p)
    return (y_prompt, y_sample)
```

```python
import functools
import math

import numpy as np
import jax
import jax.numpy as jnp
from jax import lax
from jax.experimental import pallas as pl
from jax.experimental.pallas import tpu as pltpu

F32 = jnp.float32
BF16 = jnp.bfloat16

D_MODEL = 1024
HEAD_DIM = 64
N_HEADS = 8
N_KV = 2
Q_WIDTH = N_HEADS * HEAD_DIM
KV_WIDTH = N_KV * HEAD_DIM
IN_WIDTH = 2 * (Q_WIDTH + 2 * KV_WIDTH)
BLOCK = 128
WINDOW = 128
GRID_W = 64
ROPE_THETA = 10000.0
ROPE_HALF = HEAD_DIM // 2
ROPE_QUARTER = ROPE_HALF // 2
N_BUCKETS = 32
MAX_DISTANCE = 128
D_FF = 2816
EPS = 1e-6
NEG_INF = -1e30
LOG2E = 1.4426950408889634
Q_SCALE = LOG2E / math.sqrt(HEAD_DIM)

LANES = 128
SUBLANES = 8
N_PAIRS = N_HEADS // 2
PAIRS_PER_KV = N_PAIRS // N_KV


TM_PRE = 512
TM_MIX = 1024
TM_FFN = 1024
FFN_ROW_GROUPS = 4
TQ_GLOBAL = 512
TK_SCORE = 512
TK_VALUE = 256
BF16_SUBLANES = 16
VT_ROWS = HEAD_DIM + BF16_SUBLANES
WQ = 256
WK = WQ + 2 * WINDOW
TQ_WINDOW = 2048
VMEM_LIMIT = 56 * 1024 * 1024
GLOBAL_ITER_UNITS = 4
WINDOW_ITER_UNITS = 4
GLOBAL_TILES_PER_STEP = 4

NT = (((1,), (1,)), ((), ()))


def _rmsnorm(x, gain):
    var = jnp.mean(x * x, axis=-1, keepdims=True)
    return x * lax.rsqrt(var + EPS) * gain


def _const_spec(shape):
    zeros = (0,) * len(shape)
    return pl.BlockSpec(shape, lambda *_: zeros, pipeline_mode=pl.Buffered(1))


def _two_stage(n_tiles, stage_a, stage_b, buf_even, buf_odd):
    i = pl.program_id(0)
    steady = (i > 0) & (i < n_tiles)

    @pl.when(i == 0)
    def _():
        stage_a(buf_even)

    @pl.when(steady & (i % 2 == 1))
    def _():
        stage_b(buf_even)
        stage_a(buf_odd)

    @pl.when(steady & (i % 2 == 0))
    def _():
        stage_b(buf_odd)
        stage_a(buf_even)

    @pl.when(i == n_tiles)
    def _():
        stage_b(buf_even if (n_tiles - 1) % 2 == 0 else buf_odd)


def _cur_tile(n_tiles):
    return lambda i: (jnp.minimum(i, n_tiles - 1), 0)


def _done_tile(i):
    return jnp.maximum(i - 1, 0)


def _pre_kernel(n_tiles, x_ref, g_ref, w_ref, qkg_ref, rope_ref, bd_ref,
                qa_ref, ka_ref, vat_ref, qb_ref, kb_ref, vbt_ref, proj_even, proj_odd):
    tm = x_ref.shape[0]

    def project(proj_ref):
        h = _rmsnorm(x_ref[...], g_ref[...]).astype(BF16)
        proj_ref[...] = jnp.dot(h, w_ref[...], preferred_element_type=F32)

    def finish(proj_ref):
        bd = bd_ref[...]
        cos, sin_up, sin_dn = rope_ref[0], rope_ref[1], rope_ref[2]
        lane = lax.broadcasted_iota(jnp.int32, (tm, LANES), 1)
        lo_half = lane < HEAD_DIM

        def slab(col):
            return proj_ref[:, col * LANES:(col + 1) * LANES]

        def qk_norm_rope(z, gain):
            sq = z * z
            hi = sq.astype(BF16)
            lo = (sq - hi.astype(F32)).astype(BF16)
            ssq = jnp.dot(jnp.concatenate([hi, lo], axis=1), bd, preferred_element_type=F32)
            zn = z * lax.rsqrt(ssq * (1.0 / HEAD_DIM) + EPS) * gain
            return (zn * cos
                    + pltpu.roll(zn, LANES - ROPE_QUARTER, 1) * sin_up
                    + pltpu.roll(zn, ROPE_QUARTER, 1) * sin_dn)

        def put_k(ref, z):
            swapped = pltpu.roll(z, HEAD_DIM, 1)
            ref[0, 0] = jnp.where(lo_half, z, 0.0).astype(BF16)
            ref[0, 1] = jnp.where(lo_half, 0.0, swapped).astype(BF16)
            ref[1, 0] = jnp.where(lo_half, swapped, 0.0).astype(BF16)
            ref[1, 1] = jnp.where(lo_half, 0.0, z).astype(BF16)

        def vt_with_ones(v):
            vt = v.T.astype(BF16)
            ones = jnp.ones((VT_ROWS - HEAD_DIM, tm), BF16)
            return jnp.concatenate([vt[:HEAD_DIM], ones, vt[HEAD_DIM:], ones], axis=0)

        for s in range(N_PAIRS):
            qa_ref[s] = (qk_norm_rope(slab(s), qkg_ref[0]) * Q_SCALE).astype(BF16)
        put_k(ka_ref, qk_norm_rope(slab(N_PAIRS), qkg_ref[1]))
        vat_ref[...] = vt_with_ones(slab(N_PAIRS + 1)).reshape(N_KV, VT_ROWS, tm)
        base = N_PAIRS + 2
        for s in range(N_PAIRS):
            qb_ref[s] = (slab(base + s) * Q_SCALE).astype(BF16)
        put_k(kb_ref, slab(base + N_PAIRS))
        vbt = vt_with_ones(slab(base + N_PAIRS + 1))
        for j in range(tm // BLOCK):
            vbt_ref[j] = vbt[:, j * BLOCK:(j + 1) * BLOCK]

    _two_stage(n_tiles, project, finish, proj_even, proj_odd)


def _pre_call(x2d, seq, g_pre, w_in, qk_gain, rope, bd):
    n_tok = x2d.shape[0]
    tm = TM_PRE
    n_tiles = n_tok // tm
    tiles_per_seq = seq // tm

    done = _done_tile
    q_spec = pl.BlockSpec((N_PAIRS, tm, LANES), lambda i: (0, done(i), 0))
    k_spec = pl.BlockSpec((N_KV, 2, tm, LANES), lambda i: (0, 0, done(i), 0))
    q_shape = jax.ShapeDtypeStruct((N_PAIRS, n_tok, LANES), BF16)
    k_shape = jax.ShapeDtypeStruct((N_KV, 2, n_tok, LANES), BF16)
    return pl.pallas_call(
        functools.partial(_pre_kernel, n_tiles),
        grid=(n_tiles + 1,),
        in_specs=[
            pl.BlockSpec((tm, D_MODEL), _cur_tile(n_tiles)),
            _const_spec((1, D_MODEL)),
            _const_spec((D_MODEL, IN_WIDTH)),
            _const_spec((2, 1, LANES)),
            pl.BlockSpec((3, tm, LANES), lambda i: (0, done(i) % tiles_per_seq, 0)),
            _const_spec((2 * LANES, LANES)),
        ],
        out_specs=[
            q_spec, k_spec,
            pl.BlockSpec((N_KV, VT_ROWS, tm), lambda i: (0, 0, done(i))),
            q_spec, k_spec,
            pl.BlockSpec((tm // BLOCK, N_KV * VT_ROWS, BLOCK), lambda i: (done(i), 0, 0)),
        ],
        out_shape=[
            q_shape, k_shape,
            jax.ShapeDtypeStruct((N_KV, VT_ROWS, n_tok), BF16),
            q_shape, k_shape,
            jax.ShapeDtypeStruct((n_tok // BLOCK, N_KV * VT_ROWS, BLOCK), BF16),
        ],
        scratch_shapes=[pltpu.VMEM((tm, IN_WIDTH), F32), pltpu.VMEM((tm, IN_WIDTH), F32)],
        compiler_params=pltpu.CompilerParams(
            dimension_semantics=("arbitrary",), vmem_limit_bytes=VMEM_LIMIT),
        name="pre",
    )(x2d, g_pre, w_in, qk_gain, rope, bd)


def _fold8(x, op):
    return op(x.reshape(x.shape[0] // SUBLANES, SUBLANES, x.shape[1]), axis=0)


def _softmax_units(n_units, group, iter_units, n_keys, tk_a, tk_b, score_chunk, value_chunk, s_scr, sink_of,
                   emit, put_scores=None, get_probs=None):
    assert tk_a % tk_b == 0 and n_keys % tk_a == 0
    n_a, b_per_a = n_keys // tk_a, tk_a // tk_b

    def sinks(u):
        vals = [sink_of(u, g) for g in range(group)]
        return None if vals[0] is None else vals

    def sweeps(ua, s_write, ub, m_b, s_read):
        mx = None
        acc = [None] * group
        for ca in range(n_a):
            if ua is not None:
                if put_scores is None:
                    st = jnp.concatenate([score_chunk(ua, g, ca * tk_a, tk_a) for g in range(group)], axis=1)
                    s_write[ca * tk_a:(ca + 1) * tk_a, :] = st
                    cm = _fold8(st, jnp.max)
                else:
                    assert n_a == 1
                    cm = put_scores(ua, s_write)
                mx = cm if mx is None else jnp.maximum(mx, cm)
            if ub is not None:
                for cb in range(ca * b_per_a, (ca + 1) * b_per_a):
                    if get_probs is None:
                        p = jnp.exp2(s_read[cb * tk_b:(cb + 1) * tk_b, :] - m_b).astype(BF16)
                    else:
                        p = get_probs(s_read, cb * tk_b, tk_b, m_b)
                    vt = value_chunk(ub, cb * tk_b, tk_b)
                    tq = p.shape[1] // group
                    for g in range(group):
                        pv = jnp.dot(vt, p[:, g * tq:(g + 1) * tq], preferred_element_type=F32)
                        acc[g] = pv if acc[g] is None else acc[g] + pv
        if ub is not None:
            sink = sinks(ub)
            for g in range(group):
                denom = acc[g][HEAD_DIM:HEAD_DIM + 1]
                if sink is not None:
                    tq = denom.shape[1]
                    denom = denom + jnp.exp2(sink[g] - m_b[:, g * tq:(g + 1) * tq])
                emit(ub, g, acc[g][:HEAD_DIM] / denom)
        if ua is None:
            return None
        m = jnp.max(mx, axis=0, keepdims=True)
        sink = sinks(ua)
        if sink is not None:
            tq = m.shape[1] // group
            m = jnp.concatenate([jnp.maximum(m[:, g * tq:(g + 1) * tq], sink[g]) for g in range(group)],
                                axis=1)
        return m

    def sub_steps(k0, count, m):
        for d in range(count):
            m = sweeps(k0 + d + 1, s_scr[(d + 1) % 2], k0 + d, m, s_scr[d % 2])
        return m

    assert iter_units % 2 == 0
    n_sub = n_units - 1
    trips = n_sub // iter_units + jnp.minimum(pl.program_id(0), 0)
    m = lax.fori_loop(0, trips, lambda t, m: sub_steps(iter_units * t, iter_units, m),
                      sweeps(0, s_scr[0], None, None, None))
    done = n_sub // iter_units * iter_units
    m = sub_steps(done, n_sub - done, m)
    sweeps(None, None, n_units - 1, m, s_scr[(n_units - 1) % 2])


def _gattn_kernel(q_ref, k_ref, vt_ref, o_ref, s_even, s_odd, ot_scr):
    seq = k_ref.shape[2]
    tq = TQ_GLOBAL
    n_tiles = q_ref.shape[1] // tq
    heads_per_kv = N_HEADS // N_KV

    def score_chunk(u, g, row0, tk):
        head = u % N_HEADS
        k = k_ref[head // heads_per_kv, head % 2, row0:row0 + tk, :]
        q = q_ref[head // 2, pl.ds(pl.multiple_of((u // N_HEADS) * tq, tq), tq), :]
        return lax.dot_general(k, q, NT, preferred_element_type=F32)

    def value_chunk(u, row0, tk):
        return vt_ref[(u % N_HEADS) // heads_per_kv, :, row0:row0 + tk]

    def emit(u, g, out):
        ot_scr[u] = out

    _softmax_units(n_tiles * N_HEADS, 1, GLOBAL_ITER_UNITS, seq, TK_SCORE, TK_VALUE, score_chunk, value_chunk,
                   (s_even, s_odd), lambda u, g: None, emit)
    for t in range(n_tiles):
        for pair in range(N_PAIRS):
            u = t * N_HEADS + 2 * pair
            both = jnp.concatenate([ot_scr[u], ot_scr[u + 1]], axis=0)
            o_ref[t * tq:(t + 1) * tq, pair * LANES:(pair + 1) * LANES] = both.T.astype(BF16)


def _gattn_call(q, k, vt, batch, seq):
    tq = TQ_GLOBAL
    rows = tq * GLOBAL_TILES_PER_STEP
    nq = seq // rows
    n_tok = batch * seq
    return pl.pallas_call(
        _gattn_kernel,
        grid=(batch, nq),
        in_specs=[
            pl.BlockSpec((N_PAIRS, rows, LANES), lambda b, i: (0, b * nq + i, 0)),
            pl.BlockSpec((N_KV, 2, seq, LANES), lambda b, i: (0, 0, b, 0)),
            pl.BlockSpec((N_KV, VT_ROWS, seq), lambda b, i: (0, 0, b)),
        ],
        out_specs=pl.BlockSpec((rows, Q_WIDTH), lambda b, i: (b * nq + i, 0)),
        out_shape=jax.ShapeDtypeStruct((n_tok, Q_WIDTH), BF16),
        scratch_shapes=[pltpu.VMEM((seq, tq), F32), pltpu.VMEM((seq, tq), F32),
                        pltpu.VMEM((GLOBAL_TILES_PER_STEP * N_HEADS, HEAD_DIM, tq), F32)],
        compiler_params=pltpu.CompilerParams(
            dimension_semantics=("arbitrary", "arbitrary"),
            vmem_limit_bytes=VMEM_LIMIT),
        name="gattn",
    )(q, k, vt)


def _t5_bucket_table():
    half = N_BUCKETS // 2
    max_exact = half // 2
    rel = np.arange(WK)[:, None] - WINDOW - np.arange(WQ)[None, :]
    n = np.abs(rel)
    assert MAX_DISTANCE // max_exact == 16 and half - max_exact == 8
    large = np.zeros_like(n)
    for kk in range(1, 2 * half):
        large += ((n * n) >= (2 ** kk) * max_exact * max_exact)
    large = np.minimum(max_exact + large, half - 1)
    bucket = np.where(rel > 0, half, 0) + np.where(n < max_exact, n, large)
    return bucket.astype(np.int32)


def _wattn_kernel(seq, q_ref, k_ref, vt_ref, bkt_ref, rb_ref, sink_ref, o_ref,
                  kpad, vtpad, bias_edge, bias_mid, s_even, s_odd, ot_scr):
    b = pl.program_id(0)
    i = pl.program_id(1)
    n_q_step = q_ref.shape[1] // WQ
    n_q_seq = seq // WQ
    n_blk_seq = seq // BLOCK
    edge = WINDOW

    assert WINDOW == BLOCK and WQ == 2 * BLOCK and TK_VALUE == 2 * BLOCK
    top_rc = (slice(0, edge), slice(0, BLOCK))
    mid_rc = (slice(edge, WK - edge), slice(0, WQ))
    bot_rc = (slice(WK - edge, WK), slice(BLOCK, WQ))

    @pl.when((b == 0) & (i == 0))
    def _():
        neg = jnp.full((edge, BLOCK), NEG_INF, F32)
        bias_edge[0, N_HEADS] = neg
        bias_edge[1, N_HEADS] = neg

        def table(rc, head):
            bucket = bkt_ref[rc]
            row = lax.broadcasted_iota(jnp.int32, bucket.shape, 0) + rc[0].start
            col = lax.broadcasted_iota(jnp.int32, bucket.shape, 1) + rc[1].start
            bias = jnp.zeros(bucket.shape, F32)
            for bk in range(N_BUCKETS):
                bias = jnp.where(bucket == bk, rb_ref[bk, head], bias)
            return jnp.where(jnp.abs(row - WINDOW - col) <= WINDOW, bias * LOG2E, NEG_INF)

        for head in range(N_HEADS):
            bias_edge[0, head] = table(top_rc, head)
            bias_mid[head] = table(mid_rc, head)
            bias_edge[1, head] = table(bot_rc, head)

    @pl.when(i == 0)
    def _():
        zk = jnp.zeros((WINDOW, LANES), BF16)
        for kv in range(N_KV):
            for half in range(2):
                kpad[kv, half, 0:WINDOW, :] = zk
                kpad[kv, half, WINDOW:WINDOW + seq, :] = k_ref[kv, half]
                kpad[kv, half, WINDOW + seq:2 * WINDOW + seq, :] = zk
        zv = jnp.zeros((N_KV, VT_ROWS, BLOCK), BF16)
        vtpad[0] = zv
        vtpad[pl.ds(1, n_blk_seq)] = vt_ref[...].reshape(n_blk_seq, N_KV, VT_ROWS, BLOCK)
        vtpad[n_blk_seq + 1] = zv

    group = N_HEADS // N_KV

    def put_scores(u, s_write):
        j, kv = u // N_KV, u % N_KV
        qb = i * n_q_step + j
        cms = []
        for g in range(group):
            head = kv * group + g
            q = q_ref[head // 2, pl.ds(pl.multiple_of(j * WQ, WQ), WQ), :]
            st = lax.dot_general(kpad[kv, g % 2, pl.ds(pl.multiple_of(qb * WQ, WQ), WK), :], q,
                                 NT, preferred_element_type=F32)
            top = st[top_rc] + bias_edge[0, jnp.where(qb == 0, N_HEADS, head)]
            mid = st[mid_rc] + bias_mid[head]
            bot = st[bot_rc] + bias_edge[1, jnp.where(qb == n_q_seq - 1, N_HEADS, head)]
            c0 = g * WQ
            s_write[top_rc[0], c0:c0 + BLOCK] = top
            s_write[mid_rc[0], c0:c0 + WQ] = mid
            s_write[bot_rc[0], c0 + BLOCK:c0 + WQ] = bot
            edge_max = jnp.concatenate([_fold8(top, jnp.max), _fold8(bot, jnp.max)], axis=1)
            cms.append(jnp.maximum(_fold8(mid, jnp.max), edge_max))
        return jnp.concatenate(cms, axis=1)

    def get_probs(s_read, row0, tk, m):
        zeros = jnp.zeros((BLOCK, BLOCK), F32)

        def half_rows(rows, lane0):
            parts = []
            for g in range(group):
                c = g * WQ + lane0
                live = jnp.exp2(s_read[rows, c:c + BLOCK] - m[:, c:c + BLOCK])
                parts += [live, zeros] if lane0 == 0 else [zeros, live]
            return jnp.concatenate(parts, axis=1)

        def full_rows(rows):
            return jnp.exp2(s_read[rows, :] - m)

        if row0 == 0:
            p = [half_rows(top_rc[0], 0), full_rows(slice(edge, tk))]
        else:
            p = [full_rows(slice(row0, WK - edge)), half_rows(bot_rc[0], BLOCK)]
        return jnp.concatenate(p, axis=0).astype(BF16)

    def value_chunk(u, row0, tk):
        j, kv = u // N_KV, u % N_KV
        blk0 = (i * n_q_step + j) * (WQ // BLOCK) + row0 // BLOCK
        return jnp.concatenate([vtpad[blk0 + t, kv] for t in range(tk // BLOCK)], axis=1)

    def emit(u, g, out):
        ot_scr[(u // N_KV) * N_HEADS + (u % N_KV) * group + g] = out

    _softmax_units(n_q_step * N_KV, group, WINDOW_ITER_UNITS, WK, WK, TK_VALUE, None, value_chunk,
                   (s_even, s_odd),
                   lambda u, g: sink_ref[(u % N_KV) * group + g] * LOG2E, emit,
                   put_scores=put_scores, get_probs=get_probs)
    for j in range(n_q_step):
        for pair in range(N_PAIRS):
            u = j * N_HEADS + 2 * pair
            both = jnp.concatenate([ot_scr[u], ot_scr[u + 1]], axis=0)
            o_ref[j * WQ:(j + 1) * WQ, pair * LANES:(pair + 1) * LANES] = both.T.astype(BF16)


def _wattn_call(q, k, vt3, batch, seq, bucket, rel_bias, sink):
    tq = TQ_WINDOW
    nq = seq // tq
    n_tok = batch * seq
    n_blk_seq = seq // BLOCK
    return pl.pallas_call(
        functools.partial(_wattn_kernel, seq),
        grid=(batch, nq),
        in_specs=[
            pl.BlockSpec((N_PAIRS, tq, LANES), lambda b, i: (0, b * nq + i, 0)),
            pl.BlockSpec((N_KV, 2, seq, LANES), lambda b, i: (0, 0, b, 0)),
            pl.BlockSpec((n_blk_seq, N_KV * VT_ROWS, BLOCK), lambda b, i: (b, 0, 0)),
            _const_spec((WK, WQ)),
            pl.BlockSpec(memory_space=pltpu.SMEM),
            pl.BlockSpec(memory_space=pltpu.SMEM),
        ],
        out_specs=pl.BlockSpec((tq, Q_WIDTH), lambda b, i: (b * nq + i, 0)),
        out_shape=jax.ShapeDtypeStruct((n_tok, Q_WIDTH), BF16),
        scratch_shapes=[
            pltpu.VMEM((N_KV, 2, seq + 2 * WINDOW, LANES), BF16),
            pltpu.VMEM((n_blk_seq + 2, N_KV, VT_ROWS, BLOCK), BF16),
            pltpu.VMEM((2, N_HEADS + 1, WINDOW, BLOCK), F32),
            pltpu.VMEM((N_HEADS, WK - 2 * WINDOW, WQ), F32),
            pltpu.VMEM((WK, N_HEADS // N_KV * WQ), F32),
            pltpu.VMEM((WK, N_HEADS // N_KV * WQ), F32),
            pltpu.VMEM((tq // WQ * N_HEADS, HEAD_DIM, WQ), F32),
        ],
        compiler_params=pltpu.CompilerParams(
            dimension_semantics=("arbitrary", "arbitrary"),
            vmem_limit_bytes=VMEM_LIMIT),
        name="wattn",
    )(q, k, vt3, bucket, rel_bias, sink)


def _mix_kernel(x_ref, ya_ref, yb_ref, gpre_ref, wg_ref, bg_ref, wa_ref, wb_ref, wo_ref, gpost_ref, o_ref):
    x = x_ref[...]
    h = _rmsnorm(x, gpre_ref[...]).astype(BF16)
    z = jnp.dot(h, wg_ref[...], preferred_element_type=F32) + bg_ref[...]
    gates = 1.0 / (1.0 + jnp.exp(-z))
    a = jnp.dot(ya_ref[...], wa_ref[...], preferred_element_type=F32)
    b = jnp.dot(yb_ref[...], wb_ref[...], preferred_element_type=F32)
    mix = gates[:, :D_MODEL] * a + gates[:, D_MODEL:] * b
    o = jnp.dot(mix.astype(BF16), wo_ref[...], preferred_element_type=F32)
    o_ref[...] = x + _rmsnorm(o, gpost_ref[...])


def _mix_call(x2d, ya, yb, g_pre, w_gate, b_gate, w_a, w_b, w_out, g_post):
    n_tok = x2d.shape[0]
    tm = TM_MIX
    return pl.pallas_call(
        _mix_kernel,
        grid=(n_tok // tm,),
        in_specs=[
            pl.BlockSpec((tm, D_MODEL), lambda i: (i, 0)),
            pl.BlockSpec((tm, Q_WIDTH), lambda i: (i, 0)),
            pl.BlockSpec((tm, Q_WIDTH), lambda i: (i, 0)),
            _const_spec((1, D_MODEL)),
            _const_spec((D_MODEL, 2 * D_MODEL)),
            _const_spec((1, 2 * D_MODEL)),
            _const_spec((Q_WIDTH, D_MODEL)),
            _const_spec((Q_WIDTH, D_MODEL)),
            _const_spec((D_MODEL, D_MODEL)),
            _const_spec((1, D_MODEL)),
        ],
        out_specs=pl.BlockSpec((tm, D_MODEL), lambda i: (i, 0)),
        out_shape=jax.ShapeDtypeStruct((n_tok, D_MODEL), F32),
        compiler_params=pltpu.CompilerParams(
            dimension_semantics=("arbitrary",), vmem_limit_bytes=VMEM_LIMIT),
        name="mix",
    )(x2d, ya, yb, g_pre, w_gate, b_gate, w_a, w_b, w_out, g_post)


def _ffn_kernel(x_ref, gpre_ref, wg_ref, wu_ref, wd_ref, gpost_ref, o_ref):
    rows = x_ref.shape[0] // FFN_ROW_GROUPS
    for r in range(FFN_ROW_GROUPS):
        rs = slice(r * rows, (r + 1) * rows)
        x = x_ref[rs, :]
        h = _rmsnorm(x, gpre_ref[...]).astype(BF16)
        g = jnp.dot(h, wg_ref[...], preferred_element_type=F32)
        u = jnp.dot(h, wu_ref[...], preferred_element_type=F32)
        act = (g / (1.0 + jnp.exp(-g))) * u
        f = jnp.dot(act.astype(BF16), wd_ref[...], preferred_element_type=F32)
        o_ref[rs, :] = x + _rmsnorm(f, gpost_ref[...])


def _ffn_call(x2d, g_pre, w_g, w_u, w_d, g_post):
    n_tok = x2d.shape[0]
    tm = TM_FFN
    return pl.pallas_call(
        _ffn_kernel,
        grid=(n_tok // tm,),
        in_specs=[
            pl.BlockSpec((tm, D_MODEL), lambda i: (i, 0)),
            _const_spec((1, D_MODEL)),
            _const_spec((D_MODEL, D_FF)),
            _const_spec((D_MODEL, D_FF)),
            _const_spec((D_FF, D_MODEL)),
            _const_spec((1, D_MODEL)),
        ],
        out_specs=pl.BlockSpec((tm, D_MODEL), lambda i: (i, 0)),
        out_shape=jax.ShapeDtypeStruct((n_tok, D_MODEL), F32),
        compiler_params=pltpu.CompilerParams(
            dimension_semantics=("arbitrary",), vmem_limit_bytes=VMEM_LIMIT),
        name="ffn",
    )(x2d, g_pre, w_g, w_u, w_d, g_post)


def _rope_tables(seq):
    pos = np.arange(seq)
    rows = (pos // GRID_W).astype(np.float32)
    cols = (pos % GRID_W).astype(np.float32)
    inv_freq = (np.float32(1.0) / np.power(
        np.float32(ROPE_THETA), np.arange(0, ROPE_HALF, 2, dtype=np.float32) / np.float32(ROPE_HALF)))
    lane = np.arange(LANES)
    d = lane % HEAD_DIM
    use_col = (d // ROPE_HALF) == 1
    j = d % ROPE_HALF
    f_idx = j % ROPE_QUARTER
    coord = np.where(use_col[None, :], cols[:, None], rows[:, None])
    ang = (coord * inv_freq.astype(np.float32)[f_idx][None, :]).astype(np.float32).astype(np.float64)
    cos, sin = np.cos(ang), np.sin(ang)
    first = (j < ROPE_QUARTER)[None, :]
    sin_up = np.where(first, -sin, 0.0)
    sin_dn = np.where(first, 0.0, sin)
    return jnp.asarray(np.stack([cos, sin_up, sin_dn]).astype(np.float32))


def _layer(x, bucket, bd, p):
    batch, seq, _ = x.shape
    assert seq % TM_PRE == 0 and seq % (TQ_GLOBAL * GLOBAL_TILES_PER_STEP) == 0
    assert seq % TQ_WINDOW == 0 and seq // WQ >= 2
    x2d = x.reshape(batch * seq, D_MODEL)
    qa, ka, vat, qb, kb, vbt = _pre_call(x2d, seq, p["g_mix_pre"], p["w_in"], p["qk_gain"],
                                         _rope_tables(seq), bd)
    ya = _gattn_call(qa, ka, vat, batch, seq)
    yb = _wattn_call(qb, kb, vbt, batch, seq, bucket, p["rel_bias"], p["sink"])
    x1 = _mix_call(x2d, ya, yb, p["g_mix_pre"], p["w_gate"], p["b_gate"], p["w_a"], p["w_b"], p["w_out"],
                   p["g_mix_post"])
    y = _ffn_call(x1, p["g_ffn_pre"], p["w_ffn_gate"], p["w_ffn_up"], p["w_ffn_down"], p["g_ffn_post"])
    return y.reshape(batch, seq, D_MODEL)


def kernel(x_prompt, x_sample, norm_mix_pre, norm_mix_post, w_in, q_norm_a, k_norm_a, sink_b, rel_bias,
           w_branch_a, w_branch_b, w_gate, b_gate, w_out, norm_ffn_pre, norm_ffn_post, w_ffn_gate,
           w_ffn_up, w_ffn_down):
    depth = w_in.shape[0]
    bucket = jnp.asarray(_t5_bucket_table())
    blk = np.arange(LANES) // HEAD_DIM
    bd = jnp.asarray(np.tile(blk[:, None] == blk[None, :], (2, 1)), dtype=BF16)
    y_prompt, y_sample = x_prompt, x_sample
    for l in range(depth):
        p = {
            "g_mix_pre": norm_mix_pre[l].reshape(1, D_MODEL),
            "g_mix_post": norm_mix_post[l].reshape(1, D_MODEL),
            "w_in": w_in[l].astype(BF16),
            "qk_gain": jnp.stack([jnp.tile(q_norm_a[l], 2), jnp.tile(k_norm_a[l], 2)]).reshape(2, 1, LANES),
            "sink": sink_b[l],
            "rel_bias": rel_bias,
            "w_a": w_branch_a[l].astype(BF16),
            "w_b": w_branch_b[l].astype(BF16),
            "w_gate": w_gate[l].astype(BF16),
            "b_gate": b_gate[l].reshape(1, 2 * D_MODEL),
            "w_out": w_out[l].astype(BF16),
            "g_ffn_pre": norm_ffn_pre[l].reshape(1, D_MODEL),
            "g_ffn_post": norm_ffn_post[l].reshape(1, D_MODEL),
            "w_ffn_gate": w_ffn_gate[l].astype(BF16),
            "w_ffn_up": w_ffn_up[l].astype(BF16),
            "w_ffn_down": w_ffn_down[l].astype(BF16),
        }
        y_prompt = _layer(y_prompt, bucket, bd, p)
        y_sample = _layer(y_sample, bucket, bd, p)
    return (y_prompt, y_sample)
```

```python
import functools
import math

import numpy as np
import jax
import jax.numpy as jnp
from jax import lax
from jax.experimental import pallas as pl
from jax.experimental.pallas import tpu as pltpu

F32 = jnp.float32
BF16 = jnp.bfloat16

D_MODEL = 1024
HEAD_DIM = 64
N_HEADS = 8
N_KV = 2
Q_WIDTH = N_HEADS * HEAD_DIM
KV_WIDTH = N_KV * HEAD_DIM
IN_WIDTH = 2 * (Q_WIDTH + 2 * KV_WIDTH)
BLOCK = 128
WINDOW = 128
GRID_W = 64
ROPE_THETA = 10000.0
ROPE_HALF = HEAD_DIM // 2
ROPE_QUARTER = ROPE_HALF // 2
N_BUCKETS = 32
MAX_DISTANCE = 128
D_FF = 2816
EPS = 1e-6
NEG_INF = -1e30
LOG2E = 1.4426950408889634
Q_SCALE = LOG2E / math.sqrt(HEAD_DIM)

LANES = 128
SUBLANES = 8
N_PAIRS = N_HEADS // 2
PAIRS_PER_KV = N_PAIRS // N_KV


TM_PRE = 512
TM_MIX = 1024
MIX_HEAD_GROUPS = 2
MIX_TAIL_GROUPS = 4
TM_FFN = 1024
FFN_ROW_GROUPS = 4
TQ_GLOBAL = 512
TK_SCORE = 512
TK_VALUE = 256
BF16_SUBLANES = 16
VT_ROWS = HEAD_DIM + BF16_SUBLANES
WQ = 256
WK = WQ + 2 * WINDOW
TQ_WINDOW = 2048
VMEM_LIMIT = 56 * 1024 * 1024
GLOBAL_ITER_UNITS = 4
WINDOW_ITER_UNITS = 4
GLOBAL_TILES_PER_STEP = 4

NT = (((1,), (1,)), ((), ()))


def _rmsnorm(x, gain):
    var = jnp.mean(x * x, axis=-1, keepdims=True)
    return x * lax.rsqrt(var + EPS) * gain


def _const_spec(shape):
    zeros = (0,) * len(shape)
    return pl.BlockSpec(shape, lambda *_: zeros, pipeline_mode=pl.Buffered(1))


def _two_stage(n_tiles, stage_a, stage_b, buf_even, buf_odd):
    i = pl.program_id(0)
    steady = (i > 0) & (i < n_tiles)

    @pl.when(i == 0)
    def _():
        stage_a(buf_even)

    @pl.when(steady & (i % 2 == 1))
    def _():
        stage_b(buf_even)
        stage_a(buf_odd)

    @pl.when(steady & (i % 2 == 0))
    def _():
        stage_b(buf_odd)
        stage_a(buf_even)

    @pl.when(i == n_tiles)
    def _():
        stage_b(buf_even if (n_tiles - 1) % 2 == 0 else buf_odd)


def _cur_tile(n_tiles):
    return lambda i: (jnp.minimum(i, n_tiles - 1), 0)


def _done_tile(i):
    return jnp.maximum(i - 1, 0)


def _pre_kernel(n_tiles, x_ref, g_ref, w_ref, qkg_ref, rope_ref, bd_ref,
                qa_ref, ka_ref, vat_ref, qb_ref, kb_ref, vbt_ref, proj_even, proj_odd):
    tm = x_ref.shape[0]

    def project(proj_ref):
        h = _rmsnorm(x_ref[...], g_ref[...]).astype(BF16)
        proj_ref[...] = jnp.dot(h, w_ref[...], preferred_element_type=F32)

    def finish(proj_ref):
        bd = bd_ref[...]
        cos, sin_up, sin_dn = rope_ref[0], rope_ref[1], rope_ref[2]
        lane = lax.broadcasted_iota(jnp.int32, (tm, LANES), 1)
        lo_half = lane < HEAD_DIM

        def slab(col):
            return proj_ref[:, col * LANES:(col + 1) * LANES]

        def qk_norm_rope(z, gain):
            sq = z * z
            hi = sq.astype(BF16)
            lo = (sq - hi.astype(F32)).astype(BF16)
            ssq = jnp.dot(jnp.concatenate([hi, lo], axis=1), bd, preferred_element_type=F32)
            zn = z * lax.rsqrt(ssq * (1.0 / HEAD_DIM) + EPS) * gain
            return (zn * cos
                    + pltpu.roll(zn, LANES - ROPE_QUARTER, 1) * sin_up
                    + pltpu.roll(zn, ROPE_QUARTER, 1) * sin_dn)

        def put_k(ref, z):
            swapped = pltpu.roll(z, HEAD_DIM, 1)
            ref[0, 0] = jnp.where(lo_half, z, 0.0).astype(BF16)
            ref[0, 1] = jnp.where(lo_half, 0.0, swapped).astype(BF16)
            ref[1, 0] = jnp.where(lo_half, swapped, 0.0).astype(BF16)
            ref[1, 1] = jnp.where(lo_half, 0.0, z).astype(BF16)

        def vt_with_ones(v):
            vt = v.T.astype(BF16)
            ones = jnp.ones((VT_ROWS - HEAD_DIM, tm), BF16)
            return jnp.concatenate([vt[:HEAD_DIM], ones, vt[HEAD_DIM:], ones], axis=0)

        for s in range(N_PAIRS):
            qa_ref[s] = (qk_norm_rope(slab(s), qkg_ref[0]) * Q_SCALE).astype(BF16)
        put_k(ka_ref, qk_norm_rope(slab(N_PAIRS), qkg_ref[1]))
        vat_ref[...] = vt_with_ones(slab(N_PAIRS + 1)).reshape(N_KV, VT_ROWS, tm)
        base = N_PAIRS + 2
        for s in range(N_PAIRS):
            qb_ref[s] = (slab(base + s) * Q_SCALE).astype(BF16)
        put_k(kb_ref, slab(base + N_PAIRS))
        vbt = vt_with_ones(slab(base + N_PAIRS + 1))
        for j in range(tm // BLOCK):
            vbt_ref[j] = vbt[:, j * BLOCK:(j + 1) * BLOCK]

    _two_stage(n_tiles, project, finish, proj_even, proj_odd)


def _pre_call(x2d, seq, g_pre, w_in, qk_gain, rope, bd):
    n_tok = x2d.shape[0]
    tm = TM_PRE
    n_tiles = n_tok // tm
    tiles_per_seq = seq // tm

    done = _done_tile
    q_spec = pl.BlockSpec((N_PAIRS, tm, LANES), lambda i: (0, done(i), 0))
    k_spec = pl.BlockSpec((N_KV, 2, tm, LANES), lambda i: (0, 0, done(i), 0))
    q_shape = jax.ShapeDtypeStruct((N_PAIRS, n_tok, LANES), BF16)
    k_shape = jax.ShapeDtypeStruct((N_KV, 2, n_tok, LANES), BF16)
    return pl.pallas_call(
        functools.partial(_pre_kernel, n_tiles),
        grid=(n_tiles + 1,),
        in_specs=[
            pl.BlockSpec((tm, D_MODEL), _cur_tile(n_tiles)),
            _const_spec((1, D_MODEL)),
            _const_spec((D_MODEL, IN_WIDTH)),
            _const_spec((2, 1, LANES)),
            pl.BlockSpec((3, tm, LANES), lambda i: (0, done(i) % tiles_per_seq, 0)),
            _const_spec((2 * LANES, LANES)),
        ],
        out_specs=[
            q_spec, k_spec,
            pl.BlockSpec((N_KV, VT_ROWS, tm), lambda i: (0, 0, done(i))),
            q_spec, k_spec,
            pl.BlockSpec((tm // BLOCK, N_KV * VT_ROWS, BLOCK), lambda i: (done(i), 0, 0)),
        ],
        out_shape=[
            q_shape, k_shape,
            jax.ShapeDtypeStruct((N_KV, VT_ROWS, n_tok), BF16),
            q_shape, k_shape,
            jax.ShapeDtypeStruct((n_tok // BLOCK, N_KV * VT_ROWS, BLOCK), BF16),
        ],
        scratch_shapes=[pltpu.VMEM((tm, IN_WIDTH), F32), pltpu.VMEM((tm, IN_WIDTH), F32)],
        compiler_params=pltpu.CompilerParams(
            dimension_semantics=("arbitrary",), vmem_limit_bytes=VMEM_LIMIT),
        name="pre",
    )(x2d, g_pre, w_in, qk_gain, rope, bd)


def _fold8(x, op):
    return op(x.reshape(x.shape[0] // SUBLANES, SUBLANES, x.shape[1]), axis=0)


def _softmax_units(n_units, group, iter_units, n_keys, tk_a, tk_b, score_chunk, value_chunk, s_scr, sink_of,
                   emit, put_scores=None, get_probs=None):
    assert tk_a % tk_b == 0 and n_keys % tk_a == 0
    n_a, b_per_a = n_keys // tk_a, tk_a // tk_b

    def sinks(u):
        vals = [sink_of(u, g) for g in range(group)]
        return None if vals[0] is None else vals

    def sweeps(ua, s_write, ub, m_b, s_read):
        mx = None
        acc = [None] * group
        for ca in range(n_a):
            if ua is not None:
                if put_scores is None:
                    st = jnp.concatenate([score_chunk(ua, g, ca * tk_a, tk_a) for g in range(group)], axis=1)
                    s_write[ca * tk_a:(ca + 1) * tk_a, :] = st
                    cm = _fold8(st, jnp.max)
                else:
                    assert n_a == 1
                    cm = put_scores(ua, s_write)
                mx = cm if mx is None else jnp.maximum(mx, cm)
            if ub is not None:
                for cb in range(ca * b_per_a, (ca + 1) * b_per_a):
                    if get_probs is None:
                        p = jnp.exp2(s_read[cb * tk_b:(cb + 1) * tk_b, :] - m_b).astype(BF16)
                    else:
                        p = get_probs(s_read, cb * tk_b, tk_b, m_b)
                    vt = value_chunk(ub, cb * tk_b, tk_b)
                    tq = p.shape[1] // group
                    for g in range(group):
                        pv = jnp.dot(vt, p[:, g * tq:(g + 1) * tq], preferred_element_type=F32)
                        acc[g] = pv if acc[g] is None else acc[g] + pv
        if ub is not None:
            sink = sinks(ub)
            for g in range(group):
                denom = acc[g][HEAD_DIM:HEAD_DIM + 1]
                if sink is not None:
                    tq = denom.shape[1]
                    denom = denom + jnp.exp2(sink[g] - m_b[:, g * tq:(g + 1) * tq])
                emit(ub, g, acc[g][:HEAD_DIM] / denom)
        if ua is None:
            return None
        m = jnp.max(mx, axis=0, keepdims=True)
        sink = sinks(ua)
        if sink is not None:
            tq = m.shape[1] // group
            m = jnp.concatenate([jnp.maximum(m[:, g * tq:(g + 1) * tq], sink[g]) for g in range(group)],
                                axis=1)
        return m

    def sub_steps(k0, count, m):
        for d in range(count):
            m = sweeps(k0 + d + 1, s_scr[(d + 1) % 2], k0 + d, m, s_scr[d % 2])
        return m

    assert iter_units % 2 == 0
    n_sub = n_units - 1
    trips = n_sub // iter_units + jnp.minimum(pl.program_id(0), 0)
    m = lax.fori_loop(0, trips, lambda t, m: sub_steps(iter_units * t, iter_units, m),
                      sweeps(0, s_scr[0], None, None, None))
    done = n_sub // iter_units * iter_units
    m = sub_steps(done, n_sub - done, m)
    sweeps(None, None, n_units - 1, m, s_scr[(n_units - 1) % 2])


def _gattn_kernel(q_ref, k_ref, vt_ref, o_ref, s_even, s_odd, ot_scr):
    seq = k_ref.shape[2]
    tq = TQ_GLOBAL
    n_tiles = q_ref.shape[1] // tq
    heads_per_kv = N_HEADS // N_KV

    def score_chunk(u, g, row0, tk):
        head = u % N_HEADS
        k = k_ref[head // heads_per_kv, head % 2, row0:row0 + tk, :]
        q = q_ref[head // 2, pl.ds(pl.multiple_of((u // N_HEADS) * tq, tq), tq), :]
        return lax.dot_general(k, q, NT, preferred_element_type=F32)

    def value_chunk(u, row0, tk):
        return vt_ref[(u % N_HEADS) // heads_per_kv, :, row0:row0 + tk]

    def emit(u, g, out):
        ot_scr[u] = out

    _softmax_units(n_tiles * N_HEADS, 1, GLOBAL_ITER_UNITS, seq, TK_SCORE, TK_VALUE, score_chunk, value_chunk,
                   (s_even, s_odd), lambda u, g: None, emit)
    for t in range(n_tiles):
        for pair in range(N_PAIRS):
            u = t * N_HEADS + 2 * pair
            both = jnp.concatenate([ot_scr[u], ot_scr[u + 1]], axis=0)
            o_ref[t * tq:(t + 1) * tq, pair * LANES:(pair + 1) * LANES] = both.T.astype(BF16)


def _gattn_call(q, k, vt, batch, seq):
    tq = TQ_GLOBAL
    rows = tq * GLOBAL_TILES_PER_STEP
    nq = seq // rows
    n_tok = batch * seq
    return pl.pallas_call(
        _gattn_kernel,
        grid=(batch, nq),
        in_specs=[
            pl.BlockSpec((N_PAIRS, rows, LANES), lambda b, i: (0, b * nq + i, 0)),
            pl.BlockSpec((N_KV, 2, seq, LANES), lambda b, i: (0, 0, b, 0)),
            pl.BlockSpec((N_KV, VT_ROWS, seq), lambda b, i: (0, 0, b)),
        ],
        out_specs=pl.BlockSpec((rows, Q_WIDTH), lambda b, i: (b * nq + i, 0)),
        out_shape=jax.ShapeDtypeStruct((n_tok, Q_WIDTH), BF16),
        scratch_shapes=[pltpu.VMEM((seq, tq), F32), pltpu.VMEM((seq, tq), F32),
                        pltpu.VMEM((GLOBAL_TILES_PER_STEP * N_HEADS, HEAD_DIM, tq), F32)],
        compiler_params=pltpu.CompilerParams(
            dimension_semantics=("arbitrary", "arbitrary"),
            vmem_limit_bytes=VMEM_LIMIT),
        name="gattn",
    )(q, k, vt)


def _t5_bucket_table():
    half = N_BUCKETS // 2
    max_exact = half // 2
    rel = np.arange(WK)[:, None] - WINDOW - np.arange(WQ)[None, :]
    n = np.abs(rel)
    assert MAX_DISTANCE // max_exact == 16 and half - max_exact == 8
    large = np.zeros_like(n)
    for kk in range(1, 2 * half):
        large += ((n * n) >= (2 ** kk) * max_exact * max_exact)
    large = np.minimum(max_exact + large, half - 1)
    bucket = np.where(rel > 0, half, 0) + np.where(n < max_exact, n, large)
    return bucket.astype(np.int32)


def _wattn_kernel(seq, q_ref, k_ref, vt_ref, bkt_ref, rb_ref, sink_ref, o_ref,
                  kpad, vtpad, bias_edge, bias_mid, s_even, s_odd, ot_scr):
    b = pl.program_id(0)
    i = pl.program_id(1)
    n_q_step = q_ref.shape[1] // WQ
    n_q_seq = seq // WQ
    n_blk_seq = seq // BLOCK
    edge = WINDOW

    assert WINDOW == BLOCK and WQ == 2 * BLOCK and TK_VALUE == 2 * BLOCK
    top_rc = (slice(0, edge), slice(0, BLOCK))
    mid_rc = (slice(edge, WK - edge), slice(0, WQ))
    bot_rc = (slice(WK - edge, WK), slice(BLOCK, WQ))

    @pl.when((b == 0) & (i == 0))
    def _():
        neg = jnp.full((edge, BLOCK), NEG_INF, F32)
        bias_edge[0, N_HEADS] = neg
        bias_edge[1, N_HEADS] = neg

        def table(rc, head):
            bucket = bkt_ref[rc]
            row = lax.broadcasted_iota(jnp.int32, bucket.shape, 0) + rc[0].start
            col = lax.broadcasted_iota(jnp.int32, bucket.shape, 1) + rc[1].start
            bias = jnp.zeros(bucket.shape, F32)
            for bk in range(N_BUCKETS):
                bias = jnp.where(bucket == bk, rb_ref[bk, head], bias)
            return jnp.where(jnp.abs(row - WINDOW - col) <= WINDOW, bias * LOG2E, NEG_INF)

        for head in range(N_HEADS):
            bias_edge[0, head] = table(top_rc, head)
            bias_mid[head] = table(mid_rc, head)
            bias_edge[1, head] = table(bot_rc, head)

    @pl.when(i == 0)
    def _():
        zk = jnp.zeros((WINDOW, LANES), BF16)
        for kv in range(N_KV):
            for half in range(2):
                kpad[kv, half, 0:WINDOW, :] = zk
                kpad[kv, half, WINDOW:WINDOW + seq, :] = k_ref[kv, half]
                kpad[kv, half, WINDOW + seq:2 * WINDOW + seq, :] = zk
        zv = jnp.zeros((N_KV, VT_ROWS, BLOCK), BF16)
        vtpad[0] = zv
        vtpad[pl.ds(1, n_blk_seq)] = vt_ref[...].reshape(n_blk_seq, N_KV, VT_ROWS, BLOCK)
        vtpad[n_blk_seq + 1] = zv

    group = N_HEADS // N_KV

    def put_scores(u, s_write):
        j, kv = u // N_KV, u % N_KV
        qb = i * n_q_step + j
        cms = []
        for g in range(group):
            head = kv * group + g
            q = q_ref[head // 2, pl.ds(pl.multiple_of(j * WQ, WQ), WQ), :]
            st = lax.dot_general(kpad[kv, g % 2, pl.ds(pl.multiple_of(qb * WQ, WQ), WK), :], q,
                                 NT, preferred_element_type=F32)
            top = st[top_rc] + bias_edge[0, jnp.where(qb == 0, N_HEADS, head)]
            mid = st[mid_rc] + bias_mid[head]
            bot = st[bot_rc] + bias_edge[1, jnp.where(qb == n_q_seq - 1, N_HEADS, head)]
            c0 = g * WQ
            s_write[top_rc[0], c0:c0 + BLOCK] = top
            s_write[mid_rc[0], c0:c0 + WQ] = mid
            s_write[bot_rc[0], c0 + BLOCK:c0 + WQ] = bot
            edge_max = jnp.concatenate([_fold8(top, jnp.max), _fold8(bot, jnp.max)], axis=1)
            cms.append(jnp.maximum(_fold8(mid, jnp.max), edge_max))
        return jnp.concatenate(cms, axis=1)

    def get_probs(s_read, row0, tk, m):
        zeros = jnp.zeros((BLOCK, BLOCK), F32)

        def half_rows(rows, lane0):
            parts = []
            for g in range(group):
                c = g * WQ + lane0
                live = jnp.exp2(s_read[rows, c:c + BLOCK] - m[:, c:c + BLOCK])
                parts += [live, zeros] if lane0 == 0 else [zeros, live]
            return jnp.concatenate(parts, axis=1)

        def full_rows(rows):
            return jnp.exp2(s_read[rows, :] - m)

        if row0 == 0:
            p = [half_rows(top_rc[0], 0), full_rows(slice(edge, tk))]
        else:
            p = [full_rows(slice(row0, WK - edge)), half_rows(bot_rc[0], BLOCK)]
        return jnp.concatenate(p, axis=0).astype(BF16)

    def value_chunk(u, row0, tk):
        j, kv = u // N_KV, u % N_KV
        blk0 = (i * n_q_step + j) * (WQ // BLOCK) + row0 // BLOCK
        return jnp.concatenate([vtpad[blk0 + t, kv] for t in range(tk // BLOCK)], axis=1)

    def emit(u, g, out):
        ot_scr[(u // N_KV) * N_HEADS + (u % N_KV) * group + g] = out

    _softmax_units(n_q_step * N_KV, group, WINDOW_ITER_UNITS, WK, WK, TK_VALUE, None, value_chunk,
                   (s_even, s_odd),
                   lambda u, g: sink_ref[(u % N_KV) * group + g] * LOG2E, emit,
                   put_scores=put_scores, get_probs=get_probs)
    for j in range(n_q_step):
        for pair in range(N_PAIRS):
            u = j * N_HEADS + 2 * pair
            both = jnp.concatenate([ot_scr[u], ot_scr[u + 1]], axis=0)
            o_ref[j * WQ:(j + 1) * WQ, pair * LANES:(pair + 1) * LANES] = both.T.astype(BF16)


def _wattn_call(q, k, vt3, batch, seq, bucket, rel_bias, sink):
    tq = TQ_WINDOW
    nq = seq // tq
    n_tok = batch * seq
    n_blk_seq = seq // BLOCK
    return pl.pallas_call(
        functools.partial(_wattn_kernel, seq),
        grid=(batch, nq),
        in_specs=[
            pl.BlockSpec((N_PAIRS, tq, LANES), lambda b, i: (0, b * nq + i, 0)),
            pl.BlockSpec((N_KV, 2, seq, LANES), lambda b, i: (0, 0, b, 0)),
            pl.BlockSpec((n_blk_seq, N_KV * VT_ROWS, BLOCK), lambda b, i: (b, 0, 0)),
            _const_spec((WK, WQ)),
            pl.BlockSpec(memory_space=pltpu.SMEM),
            pl.BlockSpec(memory_space=pltpu.SMEM),
        ],
        out_specs=pl.BlockSpec((tq, Q_WIDTH), lambda b, i: (b * nq + i, 0)),
        out_shape=jax.ShapeDtypeStruct((n_tok, Q_WIDTH), BF16),
        scratch_shapes=[
            pltpu.VMEM((N_KV, 2, seq + 2 * WINDOW, LANES), BF16),
            pltpu.VMEM((n_blk_seq + 2, N_KV, VT_ROWS, BLOCK), BF16),
            pltpu.VMEM((2, N_HEADS + 1, WINDOW, BLOCK), F32),
            pltpu.VMEM((N_HEADS, WK - 2 * WINDOW, WQ), F32),
            pltpu.VMEM((WK, N_HEADS // N_KV * WQ), F32),
            pltpu.VMEM((WK, N_HEADS // N_KV * WQ), F32),
            pltpu.VMEM((tq // WQ * N_HEADS, HEAD_DIM, WQ), F32),
        ],
        compiler_params=pltpu.CompilerParams(
            dimension_semantics=("arbitrary", "arbitrary"),
            vmem_limit_bytes=VMEM_LIMIT),
        name="wattn",
    )(q, k, vt3, bucket, rel_bias, sink)


def _mix_kernel(x_ref, ya_ref, yb_ref, gpre_ref, wg_ref, bg_ref, wa_ref, wb_ref, wo_ref, gpost_ref, o_ref):
    tm = x_ref.shape[0]

    def row_groups(n):
        return [slice(r * (tm // n), (r + 1) * (tm // n)) for r in range(n)]

    z = jnp.concatenate(
        [jnp.dot(_rmsnorm(x_ref[rs, :], gpre_ref[...]).astype(BF16), wg_ref[...], preferred_element_type=F32)
         for rs in row_groups(MIX_HEAD_GROUPS)], axis=0) + bg_ref[...]
    gates = 1.0 / (1.0 + jnp.exp(-z))
    a = jnp.dot(ya_ref[...], wa_ref[...], preferred_element_type=F32)
    b = jnp.dot(yb_ref[...], wb_ref[...], preferred_element_type=F32)
    mix = (gates[:, :D_MODEL] * a + gates[:, D_MODEL:] * b).astype(BF16)
    for rs in row_groups(MIX_TAIL_GROUPS):
        o = jnp.dot(mix[rs, :], wo_ref[...], preferred_element_type=F32)
        o_ref[rs, :] = x_ref[rs, :] + _rmsnorm(o, gpost_ref[...])


def _mix_call(x2d, ya, yb, g_pre, w_gate, b_gate, w_a, w_b, w_out, g_post):
    n_tok = x2d.shape[0]
    tm = TM_MIX
    return pl.pallas_call(
        _mix_kernel,
        grid=(n_tok // tm,),
        in_specs=[
            pl.BlockSpec((tm, D_MODEL), lambda i: (i, 0)),
            pl.BlockSpec((tm, Q_WIDTH), lambda i: (i, 0)),
            pl.BlockSpec((tm, Q_WIDTH), lambda i: (i, 0)),
            _const_spec((1, D_MODEL)),
            _const_spec((D_MODEL, 2 * D_MODEL)),
            _const_spec((1, 2 * D_MODEL)),
            _const_spec((Q_WIDTH, D_MODEL)),
            _const_spec((Q_WIDTH, D_MODEL)),
            _const_spec((D_MODEL, D_MODEL)),
            _const_spec((1, D_MODEL)),
        ],
        out_specs=pl.BlockSpec((tm, D_MODEL), lambda i: (i, 0)),
        out_shape=jax.ShapeDtypeStruct((n_tok, D_MODEL), F32),
        compiler_params=pltpu.CompilerParams(
            dimension_semantics=("arbitrary",), vmem_limit_bytes=VMEM_LIMIT),
        name="mix",
    )(x2d, ya, yb, g_pre, w_gate, b_gate, w_a, w_b, w_out, g_post)


def _ffn_kernel(x_ref, gpre_ref, wg_ref, wu_ref, wd_ref, gpost_ref, o_ref):
    rows = x_ref.shape[0] // FFN_ROW_GROUPS
    for r in range(FFN_ROW_GROUPS):
        rs = slice(r * rows, (r + 1) * rows)
        x = x_ref[rs, :]
        h = _rmsnorm(x, gpre_ref[...]).astype(BF16)
        g = jnp.dot(h, wg_ref[...], preferred_element_type=F32)
        u = jnp.dot(h, wu_ref[...], preferred_element_type=F32)
        act = (g / (1.0 + jnp.exp(-g))) * u
        f = jnp.dot(act.astype(BF16), wd_ref[...], preferred_element_type=F32)
        o_ref[rs, :] = x + _rmsnorm(f, gpost_ref[...])


def _ffn_call(x2d, g_pre, w_g, w_u, w_d, g_post):
    n_tok = x2d.shape[0]
    tm = TM_FFN
    return pl.pallas_call(
        _ffn_kernel,
        grid=(n_tok // tm,),
        in_specs=[
            pl.BlockSpec((tm, D_MODEL), lambda i: (i, 0)),
            _const_spec((1, D_MODEL)),
            _const_spec((D_MODEL, D_FF)),
            _const_spec((D_MODEL, D_FF)),
            _const_spec((D_FF, D_MODEL)),
            _const_spec((1, D_MODEL)),
        ],
        out_specs=pl.BlockSpec((tm, D_MODEL), lambda i: (i, 0)),
        out_shape=jax.ShapeDtypeStruct((n_tok, D_MODEL), F32),
        compiler_params=pltpu.CompilerParams(
            dimension_semantics=("arbitrary",), vmem_limit_bytes=VMEM_LIMIT),
        name="ffn",
    )(x2d, g_pre, w_g, w_u, w_d, g_post)


def _rope_tables(seq):
    pos = np.arange(seq)
    rows = (pos // GRID_W).astype(np.float32)
    cols = (pos % GRID_W).astype(np.float32)
    inv_freq = (np.float32(1.0) / np.power(
        np.float32(ROPE_THETA), np.arange(0, ROPE_HALF, 2, dtype=np.float32) / np.float32(ROPE_HALF)))
    lane = np.arange(LANES)
    d = lane % HEAD_DIM
    use_col = (d // ROPE_HALF) == 1
    j = d % ROPE_HALF
    f_idx = j % ROPE_QUARTER
    coord = np.where(use_col[None, :], cols[:, None], rows[:, None])
    ang = (coord * inv_freq.astype(np.float32)[f_idx][None, :]).astype(np.float32).astype(np.float64)
    cos, sin = np.cos(ang), np.sin(ang)
    first = (j < ROPE_QUARTER)[None, :]
    sin_up = np.where(first, -sin, 0.0)
    sin_dn = np.where(first, 0.0, sin)
    return jnp.asarray(np.stack([cos, sin_up, sin_dn]).astype(np.float32))


def _layer(x, bucket, bd, p):
    batch, seq, _ = x.shape
    assert seq % TM_PRE == 0 and seq % (TQ_GLOBAL * GLOBAL_TILES_PER_STEP) == 0
    assert seq % TQ_WINDOW == 0 and seq // WQ >= 2
    x2d = x.reshape(batch * seq, D_MODEL)
    qa, ka, vat, qb, kb, vbt = _pre_call(x2d, seq, p["g_mix_pre"], p["w_in"], p["qk_gain"],
                                         _rope_tables(seq), bd)
    ya = _gattn_call(qa, ka, vat, batch, seq)
    yb = _wattn_call(qb, kb, vbt, batch, seq, bucket, p["rel_bias"], p["sink"])
    x1 = _mix_call(x2d, ya, yb, p["g_mix_pre"], p["w_gate"], p["b_gate"], p["w_a"], p["w_b"], p["w_out"],
                   p["g_mix_post"])
    y = _ffn_call(x1, p["g_ffn_pre"], p["w_ffn_gate"], p["w_ffn_up"], p["w_ffn_down"], p["g_ffn_post"])
    return y.reshape(batch, seq, D_MODEL)


def kernel(x_prompt, x_sample, norm_mix_pre, norm_mix_post, w_in, q_norm_a, k_norm_a, sink_b, rel_bias,
           w_branch_a, w_branch_b, w_gate, b_gate, w_out, norm_ffn_pre, norm_ffn_post, w_ffn_gate,
           w_ffn_up, w_ffn_down):
    depth = w_in.shape[0]
    bucket = jnp.asarray(_t5_bucket_table())
    blk = np.arange(LANES) // HEAD_DIM
    bd = jnp.asarray(np.tile(blk[:, None] == blk[None, :], (2, 1)), dtype=BF16)
    y_prompt, y_sample = x_prompt, x_sample
    for l in range(depth):
        p = {
            "g_mix_pre": norm_mix_pre[l].reshape(1, D_MODEL),
            "g_mix_post": norm_mix_post[l].reshape(1, D_MODEL),
            "w_in": w_in[l].astype(BF16),
            "qk_gain": jnp.stack([jnp.tile(q_norm_a[l], 2), jnp.tile(k_norm_a[l], 2)]).reshape(2, 1, LANES),
            "sink": sink_b[l],
            "rel_bias": rel_bias,
            "w_a": w_branch_a[l].astype(BF16),
            "w_b": w_branch_b[l].astype(BF16),
            "w_gate": w_gate[l].astype(BF16),
            "b_gate": b_gate[l].reshape(1, 2 * D_MODEL),
            "w_out": w_out[l].astype(BF16),
            "g_ffn_pre": norm_ffn_pre[l].reshape(1, D_MODEL),
            "g_ffn_post": norm_ffn_post[l].reshape(1, D_MODEL),
            "w_ffn_gate": w_ffn_gate[l].astype(BF16),
            "w_ffn_up": w_ffn_up[l].astype(BF16),
            "w_ffn_down": w_ffn_down[l].astype(BF16),
        }
        y_prompt = _layer(y_prompt, bucket, bd, p)
        y_sample = _layer(y_sample, bucket, bd, p)
    return (y_prompt, y_sample)
```

```python
import functools
import math

import numpy as np
import jax
import jax.numpy as jnp
from jax import lax
from jax.experimental import pallas as pl
from jax.experimental.pallas import tpu as pltpu

F32 = jnp.float32
BF16 = jnp.bfloat16

D_MODEL = 1024
HEAD_DIM = 64
N_HEADS = 8
N_KV = 2
Q_WIDTH = N_HEADS * HEAD_DIM
KV_WIDTH = N_KV * HEAD_DIM
IN_WIDTH = 2 * (Q_WIDTH + 2 * KV_WIDTH)
BLOCK = 128
WINDOW = 128
GRID_W = 64
ROPE_THETA = 10000.0
ROPE_HALF = HEAD_DIM // 2
ROPE_QUARTER = ROPE_HALF // 2
N_BUCKETS = 32
MAX_DISTANCE = 128
D_FF = 2816
EPS = 1e-6
NEG_INF = -1e30
LOG2E = 1.4426950408889634
Q_SCALE = LOG2E / math.sqrt(HEAD_DIM)

LANES = 128
SUBLANES = 8
N_PAIRS = N_HEADS // 2
PAIRS_PER_KV = N_PAIRS // N_KV


TM_PRE = 512
TM_MIX = 1024
TM_FFN = 1024
FFN_ROW_GROUPS = 2
TQ_GLOBAL = 512
TK_SCORE = 512
TK_VALUE = 256
BF16_SUBLANES = 16
VT_ROWS = HEAD_DIM + BF16_SUBLANES
WQ = 256
WK = WQ + 2 * WINDOW
TQ_WINDOW = 2048
VMEM_LIMIT = 56 * 1024 * 1024
GLOBAL_ITER_UNITS = 4
WINDOW_ITER_UNITS = 4
GLOBAL_TILES_PER_STEP = 4

NT = (((1,), (1,)), ((), ()))


def _rmsnorm(x, gain):
    var = jnp.mean(x * x, axis=-1, keepdims=True)
    return x * lax.rsqrt(var + EPS) * gain


def _const_spec(shape):
    zeros = (0,) * len(shape)
    return pl.BlockSpec(shape, lambda *_: zeros, pipeline_mode=pl.Buffered(1))


def _two_stage(n_tiles, stage_a, stage_b, buf_even, buf_odd):
    i = pl.program_id(0)
    steady = (i > 0) & (i < n_tiles)

    @pl.when(i == 0)
    def _():
        stage_a(buf_even)

    @pl.when(steady & (i % 2 == 1))
    def _():
        stage_b(buf_even)
        stage_a(buf_odd)

    @pl.when(steady & (i % 2 == 0))
    def _():
        stage_b(buf_odd)
        stage_a(buf_even)

    @pl.when(i == n_tiles)
    def _():
        stage_b(buf_even if (n_tiles - 1) % 2 == 0 else buf_odd)


def _cur_tile(n_tiles):
    return lambda i: (jnp.minimum(i, n_tiles - 1), 0)


def _done_tile(i):
    return jnp.maximum(i - 1, 0)


def _pre_kernel(n_tiles, x_ref, g_ref, w_ref, qkg_ref, rope_ref, bd_ref,
                qa_ref, ka_ref, vat_ref, qb_ref, kb_ref, vbt_ref, proj_even, proj_odd):
    tm = x_ref.shape[0]

    def project(proj_ref):
        h = _rmsnorm(x_ref[...], g_ref[...]).astype(BF16)
        proj_ref[...] = jnp.dot(h, w_ref[...], preferred_element_type=F32)

    def finish(proj_ref):
        bd = bd_ref[...]
        cos, sin_up, sin_dn = rope_ref[0], rope_ref[1], rope_ref[2]
        lane = lax.broadcasted_iota(jnp.int32, (tm, LANES), 1)
        lo_half = lane < HEAD_DIM

        def slab(col):
            return proj_ref[:, col * LANES:(col + 1) * LANES]

        def qk_norm_rope(z, gain):
            sq = z * z
            hi = sq.astype(BF16)
            lo = (sq - hi.astype(F32)).astype(BF16)
            ssq = jnp.dot(jnp.concatenate([hi, lo], axis=1), bd, preferred_element_type=F32)
            zn = z * lax.rsqrt(ssq * (1.0 / HEAD_DIM) + EPS) * gain
            return (zn * cos
                    + pltpu.roll(zn, LANES - ROPE_QUARTER, 1) * sin_up
                    + pltpu.roll(zn, ROPE_QUARTER, 1) * sin_dn)

        def put_k(ref, z):
            swapped = pltpu.roll(z, HEAD_DIM, 1)
            ref[0, 0] = jnp.where(lo_half, z, 0.0).astype(BF16)
            ref[0, 1] = jnp.where(lo_half, 0.0, swapped).astype(BF16)
            ref[1, 0] = jnp.where(lo_half, swapped, 0.0).astype(BF16)
            ref[1, 1] = jnp.where(lo_half, 0.0, z).astype(BF16)

        def vt_with_ones(v):
            vt = v.T.astype(BF16)
            ones = jnp.ones((VT_ROWS - HEAD_DIM, tm), BF16)
            return jnp.concatenate([vt[:HEAD_DIM], ones, vt[HEAD_DIM:], ones], axis=0)

        for s in range(N_PAIRS):
            qa_ref[s] = (qk_norm_rope(slab(s), qkg_ref[0]) * Q_SCALE).astype(BF16)
        put_k(ka_ref, qk_norm_rope(slab(N_PAIRS), qkg_ref[1]))
        vat_ref[...] = vt_with_ones(slab(N_PAIRS + 1)).reshape(N_KV, VT_ROWS, tm)
        base = N_PAIRS + 2
        for s in range(N_PAIRS):
            qb_ref[s] = (slab(base + s) * Q_SCALE).astype(BF16)
        put_k(kb_ref, slab(base + N_PAIRS))
        vbt = vt_with_ones(slab(base + N_PAIRS + 1))
        for j in range(tm // BLOCK):
            vbt_ref[j] = vbt[:, j * BLOCK:(j + 1) * BLOCK]

    _two_stage(n_tiles, project, finish, proj_even, proj_odd)


def _pre_call(x2d, seq, g_pre, w_in, qk_gain, rope, bd):
    n_tok = x2d.shape[0]
    tm = TM_PRE
    n_tiles = n_tok // tm
    tiles_per_seq = seq // tm

    done = _done_tile
    q_spec = pl.BlockSpec((N_PAIRS, tm, LANES), lambda i: (0, done(i), 0))
    k_spec = pl.BlockSpec((N_KV, 2, tm, LANES), lambda i: (0, 0, done(i), 0))
    q_shape = jax.ShapeDtypeStruct((N_PAIRS, n_tok, LANES), BF16)
    k_shape = jax.ShapeDtypeStruct((N_KV, 2, n_tok, LANES), BF16)
    return pl.pallas_call(
        functools.partial(_pre_kernel, n_tiles),
        grid=(n_tiles + 1,),
        in_specs=[
            pl.BlockSpec((tm, D_MODEL), _cur_tile(n_tiles)),
            _const_spec((1, D_MODEL)),
            _const_spec((D_MODEL, IN_WIDTH)),
            _const_spec((2, 1, LANES)),
            pl.BlockSpec((3, tm, LANES), lambda i: (0, done(i) % tiles_per_seq, 0)),
            _const_spec((2 * LANES, LANES)),
        ],
        out_specs=[
            q_spec, k_spec,
            pl.BlockSpec((N_KV, VT_ROWS, tm), lambda i: (0, 0, done(i))),
            q_spec, k_spec,
            pl.BlockSpec((tm // BLOCK, N_KV * VT_ROWS, BLOCK), lambda i: (done(i), 0, 0)),
        ],
        out_shape=[
            q_shape, k_shape,
            jax.ShapeDtypeStruct((N_KV, VT_ROWS, n_tok), BF16),
            q_shape, k_shape,
            jax.ShapeDtypeStruct((n_tok // BLOCK, N_KV * VT_ROWS, BLOCK), BF16),
        ],
        scratch_shapes=[pltpu.VMEM((tm, IN_WIDTH), F32), pltpu.VMEM((tm, IN_WIDTH), F32)],
        compiler_params=pltpu.CompilerParams(
            dimension_semantics=("arbitrary",), vmem_limit_bytes=VMEM_LIMIT),
        name="pre",
    )(x2d, g_pre, w_in, qk_gain, rope, bd)


def _fold8(x, op):
    return op(x.reshape(x.shape[0] // SUBLANES, SUBLANES, x.shape[1]), axis=0)


def _softmax_units(n_units, group, iter_units, n_keys, tk_a, tk_b, score_chunk, value_chunk, s_scr, sink_of,
                   emit, put_scores=None, get_probs=None):
    assert tk_a % tk_b == 0 and n_keys % tk_a == 0
    n_a, b_per_a = n_keys // tk_a, tk_a // tk_b

    def sinks(u):
        vals = [sink_of(u, g) for g in range(group)]
        return None if vals[0] is None else vals

    def sweeps(ua, s_write, ub, m_b, s_read):
        mx = None
        acc = [None] * group
        for ca in range(n_a):
            if ua is not None:
                if put_scores is None:
                    st = jnp.concatenate([score_chunk(ua, g, ca * tk_a, tk_a) for g in range(group)], axis=1)
                    s_write[ca * tk_a:(ca + 1) * tk_a, :] = st
                    cm = _fold8(st, jnp.max)
                else:
                    assert n_a == 1
                    cm = put_scores(ua, s_write)
                mx = cm if mx is None else jnp.maximum(mx, cm)
            if ub is not None:
                for cb in range(ca * b_per_a, (ca + 1) * b_per_a):
                    if get_probs is None:
                        p = jnp.exp2(s_read[cb * tk_b:(cb + 1) * tk_b, :] - m_b).astype(BF16)
                    else:
                        p = get_probs(s_read, cb * tk_b, tk_b, m_b)
                    vt = value_chunk(ub, cb * tk_b, tk_b)
                    tq = p.shape[1] // group
                    for g in range(group):
                        pv = jnp.dot(vt, p[:, g * tq:(g + 1) * tq], preferred_element_type=F32)
                        acc[g] = pv if acc[g] is None else acc[g] + pv
        if ub is not None:
            sink = sinks(ub)
            for g in range(group):
                denom = acc[g][HEAD_DIM:HEAD_DIM + 1]
                if sink is not None:
                    tq = denom.shape[1]
                    denom = denom + jnp.exp2(sink[g] - m_b[:, g * tq:(g + 1) * tq])
                emit(ub, g, acc[g][:HEAD_DIM] / denom)
        if ua is None:
            return None
        m = jnp.max(mx, axis=0, keepdims=True)
        sink = sinks(ua)
        if sink is not None:
            tq = m.shape[1] // group
            m = jnp.concatenate([jnp.maximum(m[:, g * tq:(g + 1) * tq], sink[g]) for g in range(group)],
                                axis=1)
        return m

    def sub_steps(k0, count, m):
        for d in range(count):
            m = sweeps(k0 + d + 1, s_scr[(d + 1) % 2], k0 + d, m, s_scr[d % 2])
        return m

    assert iter_units % 2 == 0
    n_sub = n_units - 1
    trips = n_sub // iter_units + jnp.minimum(pl.program_id(0), 0)
    m = lax.fori_loop(0, trips, lambda t, m: sub_steps(iter_units * t, iter_units, m),
                      sweeps(0, s_scr[0], None, None, None))
    done = n_sub // iter_units * iter_units
    m = sub_steps(done, n_sub - done, m)
    sweeps(None, None, n_units - 1, m, s_scr[(n_units - 1) % 2])


def _gattn_kernel(q_ref, k_ref, vt_ref, o_ref, s_even, s_odd, ot_scr):
    seq = k_ref.shape[2]
    tq = TQ_GLOBAL
    n_tiles = q_ref.shape[1] // tq
    heads_per_kv = N_HEADS // N_KV

    def score_chunk(u, g, row0, tk):
        head = u % N_HEADS
        k = k_ref[head // heads_per_kv, head % 2, row0:row0 + tk, :]
        q = q_ref[head // 2, pl.ds(pl.multiple_of((u // N_HEADS) * tq, tq), tq), :]
        return lax.dot_general(k, q, NT, preferred_element_type=F32)

    def value_chunk(u, row0, tk):
        return vt_ref[(u % N_HEADS) // heads_per_kv, :, row0:row0 + tk]

    def emit(u, g, out):
        ot_scr[u] = out

    _softmax_units(n_tiles * N_HEADS, 1, GLOBAL_ITER_UNITS, seq, TK_SCORE, TK_VALUE, score_chunk, value_chunk,
                   (s_even, s_odd), lambda u, g: None, emit)
    for t in range(n_tiles):
        for pair in range(N_PAIRS):
            u = t * N_HEADS + 2 * pair
            both = jnp.concatenate([ot_scr[u], ot_scr[u + 1]], axis=0)
            o_ref[t * tq:(t + 1) * tq, pair * LANES:(pair + 1) * LANES] = both.T.astype(BF16)


def _gattn_call(q, k, vt, batch, seq):
    tq = TQ_GLOBAL
    rows = tq * GLOBAL_TILES_PER_STEP
    nq = seq // rows
    n_tok = batch * seq
    return pl.pallas_call(
        _gattn_kernel,
        grid=(batch, nq),
        in_specs=[
            pl.BlockSpec((N_PAIRS, rows, LANES), lambda b, i: (0, b * nq + i, 0)),
            pl.BlockSpec((N_KV, 2, seq, LANES), lambda b, i: (0, 0, b, 0)),
            pl.BlockSpec((N_KV, VT_ROWS, seq), lambda b, i: (0, 0, b)),
        ],
        out_specs=pl.BlockSpec((rows, Q_WIDTH), lambda b, i: (b * nq + i, 0)),
        out_shape=jax.ShapeDtypeStruct((n_tok, Q_WIDTH), BF16),
        scratch_shapes=[pltpu.VMEM((seq, tq), F32), pltpu.VMEM((seq, tq), F32),
                        pltpu.VMEM((GLOBAL_TILES_PER_STEP * N_HEADS, HEAD_DIM, tq), F32)],
        compiler_params=pltpu.CompilerParams(
            dimension_semantics=("arbitrary", "arbitrary"),
            vmem_limit_bytes=VMEM_LIMIT),
        name="gattn",
    )(q, k, vt)


def _t5_bucket_table():
    half = N_BUCKETS // 2
    max_exact = half // 2
    rel = np.arange(WK)[:, None] - WINDOW - np.arange(WQ)[None, :]
    n = np.abs(rel)
    assert MAX_DISTANCE // max_exact == 16 and half - max_exact == 8
    large = np.zeros_like(n)
    for kk in range(1, 2 * half):
        large += ((n * n) >= (2 ** kk) * max_exact * max_exact)
    large = np.minimum(max_exact + large, half - 1)
    bucket = np.where(rel > 0, half, 0) + np.where(n < max_exact, n, large)
    return bucket.astype(np.int32)


def _wattn_kernel(seq, q_ref, k_ref, vt_ref, bkt_ref, rb_ref, sink_ref, o_ref,
                  kpad, vtpad, bias_edge, bias_mid, s_even, s_odd, ot_scr):
    b = pl.program_id(0)
    i = pl.program_id(1)
    n_q_step = q_ref.shape[1] // WQ
    n_q_seq = seq // WQ
    n_blk_seq = seq // BLOCK
    edge = WINDOW

    assert WINDOW == BLOCK and WQ == 2 * BLOCK and TK_VALUE == 2 * BLOCK
    top_rc = (slice(0, edge), slice(0, BLOCK))
    mid_rc = (slice(edge, WK - edge), slice(0, WQ))
    bot_rc = (slice(WK - edge, WK), slice(BLOCK, WQ))

    @pl.when((b == 0) & (i == 0))
    def _():
        neg = jnp.full((edge, BLOCK), NEG_INF, F32)
        bias_edge[0, N_HEADS] = neg
        bias_edge[1, N_HEADS] = neg

        def table(rc, head):
            bucket = bkt_ref[rc]
            row = lax.broadcasted_iota(jnp.int32, bucket.shape, 0) + rc[0].start
            col = lax.broadcasted_iota(jnp.int32, bucket.shape, 1) + rc[1].start
            bias = jnp.zeros(bucket.shape, F32)
            for bk in range(N_BUCKETS):
                bias = jnp.where(bucket == bk, rb_ref[bk, head], bias)
            return jnp.where(jnp.abs(row - WINDOW - col) <= WINDOW, bias * LOG2E, NEG_INF)

        for head in range(N_HEADS):
            bias_edge[0, head] = table(top_rc, head)
            bias_mid[head] = table(mid_rc, head)
            bias_edge[1, head] = table(bot_rc, head)

    @pl.when(i == 0)
    def _():
        zk = jnp.zeros((WINDOW, LANES), BF16)
        for kv in range(N_KV):
            for half in range(2):
                kpad[kv, half, 0:WINDOW, :] = zk
                kpad[kv, half, WINDOW:WINDOW + seq, :] = k_ref[kv, half]
                kpad[kv, half, WINDOW + seq:2 * WINDOW + seq, :] = zk
        zv = jnp.zeros((N_KV, VT_ROWS, BLOCK), BF16)
        vtpad[0] = zv
        vtpad[pl.ds(1, n_blk_seq)] = vt_ref[...].reshape(n_blk_seq, N_KV, VT_ROWS, BLOCK)
        vtpad[n_blk_seq + 1] = zv

    group = N_HEADS // N_KV

    def put_scores(u, s_write):
        j, kv = u // N_KV, u % N_KV
        qb = i * n_q_step + j
        cms = []
        for g in range(group):
            head = kv * group + g
            q = q_ref[head // 2, pl.ds(pl.multiple_of(j * WQ, WQ), WQ), :]
            st = lax.dot_general(kpad[kv, g % 2, pl.ds(pl.multiple_of(qb * WQ, WQ), WK), :], q,
                                 NT, preferred_element_type=F32)
            top = st[top_rc] + bias_edge[0, jnp.where(qb == 0, N_HEADS, head)]
            mid = st[mid_rc] + bias_mid[head]
            bot = st[bot_rc] + bias_edge[1, jnp.where(qb == n_q_seq - 1, N_HEADS, head)]
            c0 = g * WQ
            s_write[top_rc[0], c0:c0 + BLOCK] = top
            s_write[mid_rc[0], c0:c0 + WQ] = mid
            s_write[bot_rc[0], c0 + BLOCK:c0 + WQ] = bot
            edge_max = jnp.concatenate([_fold8(top, jnp.max), _fold8(bot, jnp.max)], axis=1)
            cms.append(jnp.maximum(_fold8(mid, jnp.max), edge_max))
        return jnp.concatenate(cms, axis=1)

    def get_probs(s_read, row0, tk, m):
        zeros = jnp.zeros((BLOCK, BLOCK), F32)

        def half_rows(rows, lane0):
            parts = []
            for g in range(group):
                c = g * WQ + lane0
                live = jnp.exp2(s_read[rows, c:c + BLOCK] - m[:, c:c + BLOCK])
                parts += [live, zeros] if lane0 == 0 else [zeros, live]
            return jnp.concatenate(parts, axis=1)

        def full_rows(rows):
            return jnp.exp2(s_read[rows, :] - m)

        if row0 == 0:
            p = [half_rows(top_rc[0], 0), full_rows(slice(edge, tk))]
        else:
            p = [full_rows(slice(row0, WK - edge)), half_rows(bot_rc[0], BLOCK)]
        return jnp.concatenate(p, axis=0).astype(BF16)

    def value_chunk(u, row0, tk):
        j, kv = u // N_KV, u % N_KV
        blk0 = (i * n_q_step + j) * (WQ // BLOCK) + row0 // BLOCK
        return jnp.concatenate([vtpad[blk0 + t, kv] for t in range(tk // BLOCK)], axis=1)

    def emit(u, g, out):
        ot_scr[(u // N_KV) * N_HEADS + (u % N_KV) * group + g] = out

    _softmax_units(n_q_step * N_KV, group, WINDOW_ITER_UNITS, WK, WK, TK_VALUE, None, value_chunk,
                   (s_even, s_odd),
                   lambda u, g: sink_ref[(u % N_KV) * group + g] * LOG2E, emit,
                   put_scores=put_scores, get_probs=get_probs)
    for j in range(n_q_step):
        for pair in range(N_PAIRS):
            u = j * N_HEADS + 2 * pair
            both = jnp.concatenate([ot_scr[u], ot_scr[u + 1]], axis=0)
            o_ref[j * WQ:(j + 1) * WQ, pair * LANES:(pair + 1) * LANES] = both.T.astype(BF16)


def _wattn_call(q, k, vt3, batch, seq, bucket, rel_bias, sink):
    tq = TQ_WINDOW
    nq = seq // tq
    n_tok = batch * seq
    n_blk_seq = seq // BLOCK
    return pl.pallas_call(
        functools.partial(_wattn_kernel, seq),
        grid=(batch, nq),
        in_specs=[
            pl.BlockSpec((N_PAIRS, tq, LANES), lambda b, i: (0, b * nq + i, 0)),
            pl.BlockSpec((N_KV, 2, seq, LANES), lambda b, i: (0, 0, b, 0)),
            pl.BlockSpec((n_blk_seq, N_KV * VT_ROWS, BLOCK), lambda b, i: (b, 0, 0)),
            _const_spec((WK, WQ)),
            pl.BlockSpec(memory_space=pltpu.SMEM),
            pl.BlockSpec(memory_space=pltpu.SMEM),
        ],
        out_specs=pl.BlockSpec((tq, Q_WIDTH), lambda b, i: (b * nq + i, 0)),
        out_shape=jax.ShapeDtypeStruct((n_tok, Q_WIDTH), BF16),
        scratch_shapes=[
            pltpu.VMEM((N_KV, 2, seq + 2 * WINDOW, LANES), BF16),
            pltpu.VMEM((n_blk_seq + 2, N_KV, VT_ROWS, BLOCK), BF16),
            pltpu.VMEM((2, N_HEADS + 1, WINDOW, BLOCK), F32),
            pltpu.VMEM((N_HEADS, WK - 2 * WINDOW, WQ), F32),
            pltpu.VMEM((WK, N_HEADS // N_KV * WQ), F32),
            pltpu.VMEM((WK, N_HEADS // N_KV * WQ), F32),
            pltpu.VMEM((tq // WQ * N_HEADS, HEAD_DIM, WQ), F32),
        ],
        compiler_params=pltpu.CompilerParams(
            dimension_semantics=("arbitrary", "arbitrary"),
            vmem_limit_bytes=VMEM_LIMIT),
        name="wattn",
    )(q, k, vt3, bucket, rel_bias, sink)


def _mix_kernel(x_ref, ya_ref, yb_ref, gpre_ref, wg_ref, bg_ref, wa_ref, wb_ref, wo_ref, gpost_ref, o_ref):
    x = x_ref[...]
    h = _rmsnorm(x, gpre_ref[...]).astype(BF16)
    z = jnp.dot(h, wg_ref[...], preferred_element_type=F32) + bg_ref[...]
    gates = 1.0 / (1.0 + jnp.exp(-z))
    a = jnp.dot(ya_ref[...], wa_ref[...], preferred_element_type=F32)
    b = jnp.dot(yb_ref[...], wb_ref[...], preferred_element_type=F32)
    mix = gates[:, :D_MODEL] * a + gates[:, D_MODEL:] * b
    o = jnp.dot(mix.astype(BF16), wo_ref[...], preferred_element_type=F32)
    o_ref[...] = x + _rmsnorm(o, gpost_ref[...])


def _mix_call(x2d, ya, yb, g_pre, w_gate, b_gate, w_a, w_b, w_out, g_post):
    n_tok = x2d.shape[0]
    tm = TM_MIX
    return pl.pallas_call(
        _mix_kernel,
        grid=(n_tok // tm,),
        in_specs=[
            pl.BlockSpec((tm, D_MODEL), lambda i: (i, 0)),
            pl.BlockSpec((tm, Q_WIDTH), lambda i: (i, 0)),
            pl.BlockSpec((tm, Q_WIDTH), lambda i: (i, 0)),
            _const_spec((1, D_MODEL)),
            _const_spec((D_MODEL, 2 * D_MODEL)),
            _const_spec((1, 2 * D_MODEL)),
            _const_spec((Q_WIDTH, D_MODEL)),
            _const_spec((Q_WIDTH, D_MODEL)),
            _const_spec((D_MODEL, D_MODEL)),
            _const_spec((1, D_MODEL)),
        ],
        out_specs=pl.BlockSpec((tm, D_MODEL), lambda i: (i, 0)),
        out_shape=jax.ShapeDtypeStruct((n_tok, D_MODEL), F32),
        compiler_params=pltpu.CompilerParams(
            dimension_semantics=("arbitrary",), vmem_limit_bytes=VMEM_LIMIT),
        name="mix",
    )(x2d, ya, yb, g_pre, w_gate, b_gate, w_a, w_b, w_out, g_post)


def _ffn_kernel(x_ref, gpre_ref, wg_ref, wu_ref, wd_ref, gpost_ref, o_ref):
    rows = x_ref.shape[0] // FFN_ROW_GROUPS
    for r in range(FFN_ROW_GROUPS):
        rs = slice(r * rows, (r + 1) * rows)
        x = x_ref[rs, :]
        h = _rmsnorm(x, gpre_ref[...]).astype(BF16)
        g = jnp.dot(h, wg_ref[...], preferred_element_type=F32)
        u = jnp.dot(h, wu_ref[...], preferred_element_type=F32)
        act = (g / (1.0 + jnp.exp(-g))) * u
        f = jnp.dot(act.astype(BF16), wd_ref[...], preferred_element_type=F32)
        o_ref[rs, :] = x + _rmsnorm(f, gpost_ref[...])


def _ffn_call(x2d, g_pre, w_g, w_u, w_d, g_post):
    n_tok = x2d.shape[0]
    tm = TM_FFN
    return pl.pallas_call(
        _ffn_kernel,
        grid=(n_tok // tm,),
        in_specs=[
            pl.BlockSpec((tm, D_MODEL), lambda i: (i, 0)),
            _const_spec((1, D_MODEL)),
            _const_spec((D_MODEL, D_FF)),
            _const_spec((D_MODEL, D_FF)),
            _const_spec((D_FF, D_MODEL)),
            _const_spec((1, D_MODEL)),
        ],
        out_specs=pl.BlockSpec((tm, D_MODEL), lambda i: (i, 0)),
        out_shape=jax.ShapeDtypeStruct((n_tok, D_MODEL), F32),
        compiler_params=pltpu.CompilerParams(
            dimension_semantics=("arbitrary",), vmem_limit_bytes=VMEM_LIMIT),
        name="ffn",
    )(x2d, g_pre, w_g, w_u, w_d, g_post)


def _rope_tables(seq):
    pos = np.arange(seq)
    rows = (pos // GRID_W).astype(np.float32)
    cols = (pos % GRID_W).astype(np.float32)
    inv_freq = (np.float32(1.0) / np.power(
        np.float32(ROPE_THETA), np.arange(0, ROPE_HALF, 2, dtype=np.float32) / np.float32(ROPE_HALF)))
    lane = np.arange(LANES)
    d = lane % HEAD_DIM
    use_col = (d // ROPE_HALF) == 1
    j = d % ROPE_HALF
    f_idx = j % ROPE_QUARTER
    coord = np.where(use_col[None, :], cols[:, None], rows[:, None])
    ang = (coord * inv_freq.astype(np.float32)[f_idx][None, :]).astype(np.float32).astype(np.float64)
    cos, sin = np.cos(ang), np.sin(ang)
    first = (j < ROPE_QUARTER)[None, :]
    sin_up = np.where(first, -sin, 0.0)
    sin_dn = np.where(first, 0.0, sin)
    return jnp.asarray(np.stack([cos, sin_up, sin_dn]).astype(np.float32))


def _layer(x, bucket, bd, p):
    batch, seq, _ = x.shape
    assert seq % TM_PRE == 0 and seq % (TQ_GLOBAL * GLOBAL_TILES_PER_STEP) == 0
    assert seq % TQ_WINDOW == 0 and seq // WQ >= 2
    x2d = x.reshape(batch * seq, D_MODEL)
    qa, ka, vat, qb, kb, vbt = _pre_call(x2d, seq, p["g_mix_pre"], p["w_in"], p["qk_gain"],
                                         _rope_tables(seq), bd)
    ya = _gattn_call(qa, ka, vat, batch, seq)
    yb = _wattn_call(qb, kb, vbt, batch, seq, bucket, p["rel_bias"], p["sink"])
    x1 = _mix_call(x2d, ya, yb, p["g_mix_pre"], p["w_gate"], p["b_gate"], p["w_a"], p["w_b"], p["w_out"],
                   p["g_mix_post"])
    y = _ffn_call(x1, p["g_ffn_pre"], p["w_ffn_gate"], p["w_ffn_up"], p["w_ffn_down"], p["g_ffn_post"])
    return y.reshape(batch, seq, D_MODEL)


def kernel(x_prompt, x_sample, norm_mix_pre, norm_mix_post, w_in, q_norm_a, k_norm_a, sink_b, rel_bias,
           w_branch_a, w_branch_b, w_gate, b_gate, w_out, norm_ffn_pre, norm_ffn_post, w_ffn_gate,
           w_ffn_up, w_ffn_down):
    depth = w_in.shape[0]
    bucket = jnp.asarray(_t5_bucket_table())
    blk = np.arange(LANES) // HEAD_DIM
    bd = jnp.asarray(np.tile(blk[:, None] == blk[None, :], (2, 1)), dtype=BF16)
    y_prompt, y_sample = x_prompt, x_sample
    for l in range(depth):
        p = {
            "g_mix_pre": norm_mix_pre[l].reshape(1, D_MODEL),
            "g_mix_post": norm_mix_post[l].reshape(1, D_MODEL),
            "w_in": w_in[l].astype(BF16),
            "qk_gain": jnp.stack([jnp.tile(q_norm_a[l], 2), jnp.tile(k_norm_a[l], 2)]).reshape(2, 1, LANES),
            "sink": sink_b[l],
            "rel_bias": rel_bias,
            "w_a": w_branch_a[l].astype(BF16),
            "w_b": w_branch_b[l].astype(BF16),
            "w_gate": w_gate[l].astype(BF16),
            "b_gate": b_gate[l].reshape(1, 2 * D_MODEL),
            "w_out": w_out[l].astype(BF16),
            "g_ffn_pre": norm_ffn_pre[l].reshape(1, D_MODEL),
            "g_ffn_post": norm_ffn_post[l].reshape(1, D_MODEL),
            "w_ffn_gate": w_ffn_gate[l].astype(BF16),
            "w_ffn_up": w_ffn_up[l].astype(BF16),
            "w_ffn_down": w_ffn_down[l].astype(BF16),
        }
        y_prompt = _layer(y_prompt, bucket, bd, p)
        y_sample = _layer(y_sample, bucket, bd, p)
    return (y_prompt, y_sample)
```

```python
import functools
import math

import numpy as np
import jax
import jax.numpy as jnp
from jax import lax
from jax.experimental import pallas as pl
from jax.experimental.pallas import tpu as pltpu

F32 = jnp.float32
BF16 = jnp.bfloat16

D_MODEL = 1024
HEAD_DIM = 64
N_HEADS = 8
N_KV = 2
Q_WIDTH = N_HEADS * HEAD_DIM
KV_WIDTH = N_KV * HEAD_DIM
IN_WIDTH = 2 * (Q_WIDTH + 2 * KV_WIDTH)
BLOCK = 128
WINDOW = 128
GRID_W = 64
ROPE_THETA = 10000.0
ROPE_HALF = HEAD_DIM // 2
ROPE_QUARTER = ROPE_HALF // 2
N_BUCKETS = 32
MAX_DISTANCE = 128
D_FF = 2816
EPS = 1e-6
NEG_INF = -1e30
LOG2E = 1.4426950408889634
Q_SCALE = LOG2E / math.sqrt(HEAD_DIM)

LANES = 128
SUBLANES = 8
N_PAIRS = N_HEADS // 2
PAIRS_PER_KV = N_PAIRS // N_KV


TM_PRE = 512
TM_MIX = 1024
TM_FFN = 1024
FFN_ROW_GROUPS = 4
TQ_GLOBAL = 512
TK_SCORE = 512
TK_VALUE = 256
BF16_SUBLANES = 16
VT_ROWS = HEAD_DIM + BF16_SUBLANES
WQ = 256
WK = WQ + 2 * WINDOW
TQ_WINDOW = 2048
VMEM_LIMIT = 56 * 1024 * 1024
GLOBAL_ITER_UNITS = 4
WINDOW_ITER_UNITS = 4
GLOBAL_TILES_PER_STEP = 4

NT = (((1,), (1,)), ((), ()))


def _rmsnorm(x, gain):
    var = jnp.mean(x * x, axis=-1, keepdims=True)
    return x * lax.rsqrt(var + EPS) * gain


def _const_spec(shape):
    zeros = (0,) * len(shape)
    return pl.BlockSpec(shape, lambda *_: zeros, pipeline_mode=pl.Buffered(1))


def _two_stage(n_tiles, stage_a, stage_b, buf_even, buf_odd):
    i = pl.program_id(0)
    steady = (i > 0) & (i < n_tiles)

    @pl.when(i == 0)
    def _():
        stage_a(buf_even)

    @pl.when(steady & (i % 2 == 1))
    def _():
        stage_b(buf_even)
        stage_a(buf_odd)

    @pl.when(steady & (i % 2 == 0))
    def _():
        stage_b(buf_odd)
        stage_a(buf_even)

    @pl.when(i == n_tiles)
    def _():
        stage_b(buf_even if (n_tiles - 1) % 2 == 0 else buf_odd)


def _cur_tile(n_tiles):
    return lambda i: (jnp.minimum(i, n_tiles - 1), 0)


def _done_tile(i):
    return jnp.maximum(i - 1, 0)


def _pre_kernel(n_tiles, x_ref, g_ref, w_ref, qkg_ref, rope_ref, bd_ref,
                qa_ref, ka_ref, vat_ref, qb_ref, kb_ref, vbt_ref, proj_even, proj_odd):
    tm = x_ref.shape[0]

    def project(proj_ref):
        h = _rmsnorm(x_ref[...], g_ref[...]).astype(BF16)
        proj_ref[...] = jnp.dot(h, w_ref[...], preferred_element_type=F32)

    def finish(proj_ref):
        bd = bd_ref[...]
        cos, sin_up, sin_dn = rope_ref[0], rope_ref[1], rope_ref[2]
        lane = lax.broadcasted_iota(jnp.int32, (tm, LANES), 1)
        lo_half = lane < HEAD_DIM

        def slab(col):
            return proj_ref[:, col * LANES:(col + 1) * LANES]

        def qk_norm_rope(z, gain):
            sq = z * z
            hi = sq.astype(BF16)
            lo = (sq - hi.astype(F32)).astype(BF16)
            ssq = jnp.dot(jnp.concatenate([hi, lo], axis=1), bd, preferred_element_type=F32)
            zn = z * lax.rsqrt(ssq * (1.0 / HEAD_DIM) + EPS) * gain
            return (zn * cos
                    + pltpu.roll(zn, LANES - ROPE_QUARTER, 1) * sin_up
                    + pltpu.roll(zn, ROPE_QUARTER, 1) * sin_dn)

        def put_k(ref, z):
            swapped = pltpu.roll(z, HEAD_DIM, 1)
            ref[0, 0] = jnp.where(lo_half, z, 0.0).astype(BF16)
            ref[0, 1] = jnp.where(lo_half, 0.0, swapped).astype(BF16)
            ref[1, 0] = jnp.where(lo_half, swapped, 0.0).astype(BF16)
            ref[1, 1] = jnp.where(lo_half, 0.0, z).astype(BF16)

        def vt_with_ones(v):
            vt = v.T.astype(BF16)
            ones = jnp.ones((VT_ROWS - HEAD_DIM, tm), BF16)
            return jnp.concatenate([vt[:HEAD_DIM], ones, vt[HEAD_DIM:], ones], axis=0)

        for s in range(N_PAIRS):
            qa_ref[s] = (qk_norm_rope(slab(s), qkg_ref[0]) * Q_SCALE).astype(BF16)
        put_k(ka_ref, qk_norm_rope(slab(N_PAIRS), qkg_ref[1]))
        vat_ref[...] = vt_with_ones(slab(N_PAIRS + 1)).reshape(N_KV, VT_ROWS, tm)
        base = N_PAIRS + 2
        for s in range(N_PAIRS):
            qb_ref[s] = (slab(base + s) * Q_SCALE).astype(BF16)
        put_k(kb_ref, slab(base + N_PAIRS))
        vbt = vt_with_ones(slab(base + N_PAIRS + 1))
        for j in range(tm // BLOCK):
            vbt_ref[j] = vbt[:, j * BLOCK:(j + 1) * BLOCK]

    _two_stage(n_tiles, project, finish, proj_even, proj_odd)


def _pre_call(x2d, seq, g_pre, w_in, qk_gain, rope, bd):
    n_tok = x2d.shape[0]
    tm = TM_PRE
    n_tiles = n_tok // tm
    tiles_per_seq = seq // tm

    done = _done_tile
    q_spec = pl.BlockSpec((N_PAIRS, tm, LANES), lambda i: (0, done(i), 0))
    k_spec = pl.BlockSpec((N_KV, 2, tm, LANES), lambda i: (0, 0, done(i), 0))
    q_shape = jax.ShapeDtypeStruct((N_PAIRS, n_tok, LANES), BF16)
    k_shape = jax.ShapeDtypeStruct((N_KV, 2, n_tok, LANES), BF16)
    return pl.pallas_call(
        functools.partial(_pre_kernel, n_tiles),
        grid=(n_tiles + 1,),
        in_specs=[
            pl.BlockSpec((tm, D_MODEL), _cur_tile(n_tiles)),
            _const_spec((1, D_MODEL)),
            _const_spec((D_MODEL, IN_WIDTH)),
            _const_spec((2, 1, LANES)),
            pl.BlockSpec((3, tm, LANES), lambda i: (0, done(i) % tiles_per_seq, 0)),
            _const_spec((2 * LANES, LANES)),
        ],
        out_specs=[
            q_spec, k_spec,
            pl.BlockSpec((N_KV, VT_ROWS, tm), lambda i: (0, 0, done(i))),
            q_spec, k_spec,
            pl.BlockSpec((tm // BLOCK, N_KV * VT_ROWS, BLOCK), lambda i: (done(i), 0, 0)),
        ],
        out_shape=[
            q_shape, k_shape,
            jax.ShapeDtypeStruct((N_KV, VT_ROWS, n_tok), BF16),
            q_shape, k_shape,
            jax.ShapeDtypeStruct((n_tok // BLOCK, N_KV * VT_ROWS, BLOCK), BF16),
        ],
        scratch_shapes=[pltpu.VMEM((tm, IN_WIDTH), F32), pltpu.VMEM((tm, IN_WIDTH), F32)],
        compiler_params=pltpu.CompilerParams(
            dimension_semantics=("arbitrary",), vmem_limit_bytes=VMEM_LIMIT),
        name="pre",
    )(x2d, g_pre, w_in, qk_gain, rope, bd)


def _fold8(x, op):
    return op(x.reshape(x.shape[0] // SUBLANES, SUBLANES, x.shape[1]), axis=0)


def _softmax_units(n_units, group, iter_units, n_keys, tk_a, tk_b, score_chunk, value_chunk, s_scr, sink_of,
                   emit, put_scores=None, get_probs=None):
    assert tk_a % tk_b == 0 and n_keys % tk_a == 0
    n_a, b_per_a = n_keys // tk_a, tk_a // tk_b

    def sinks(u):
        vals = [sink_of(u, g) for g in range(group)]
        return None if vals[0] is None else vals

    def sweeps(ua, s_write, ub, m_b, s_read):
        mx = None
        acc = [None] * group
        for ca in range(n_a):
            if ua is not None:
                if put_scores is None:
                    st = jnp.concatenate([score_chunk(ua, g, ca * tk_a, tk_a) for g in range(group)], axis=1)
                    s_write[ca * tk_a:(ca + 1) * tk_a, :] = st
                    cm = _fold8(st, jnp.max)
                else:
                    assert n_a == 1
                    cm = put_scores(ua, s_write)
                mx = cm if mx is None else jnp.maximum(mx, cm)
            if ub is not None:
                for cb in range(ca * b_per_a, (ca + 1) * b_per_a):
                    if get_probs is None:
                        p = jnp.exp2(s_read[cb * tk_b:(cb + 1) * tk_b, :] - m_b).astype(BF16)
                    else:
                        p = get_probs(s_read, cb * tk_b, tk_b, m_b)
                    vt = value_chunk(ub, cb * tk_b, tk_b)
                    tq = p.shape[1] // group
                    for g in range(group):
                        pv = jnp.dot(vt, p[:, g * tq:(g + 1) * tq], preferred_element_type=F32)
                        acc[g] = pv if acc[g] is None else acc[g] + pv
        if ub is not None:
            sink = sinks(ub)
            for g in range(group):
                denom = acc[g][HEAD_DIM:HEAD_DIM + 1]
                if sink is not None:
                    tq = denom.shape[1]
                    denom = denom + jnp.exp2(sink[g] - m_b[:, g * tq:(g + 1) * tq])
                emit(ub, g, acc[g][:HEAD_DIM] / denom)
        if ua is None:
            return None
        m = jnp.max(mx, axis=0, keepdims=True)
        sink = sinks(ua)
        if sink is not None:
            tq = m.shape[1] // group
            m = jnp.concatenate([jnp.maximum(m[:, g * tq:(g + 1) * tq], sink[g]) for g in range(group)],
                                axis=1)
        return m

    def sub_steps(k0, count, m):
        for d in range(count):
            m = sweeps(k0 + d + 1, s_scr[(d + 1) % 2], k0 + d, m, s_scr[d % 2])
        return m

    assert iter_units % 2 == 0
    n_sub = n_units - 1
    trips = n_sub // iter_units + jnp.minimum(pl.program_id(0), 0)
    m = lax.fori_loop(0, trips, lambda t, m: sub_steps(iter_units * t, iter_units, m),
                      sweeps(0, s_scr[0], None, None, None))
    done = n_sub // iter_units * iter_units
    m = sub_steps(done, n_sub - done, m)
    sweeps(None, None, n_units - 1, m, s_scr[(n_units - 1) % 2])


def _gattn_kernel(q_ref, k_ref, vt_ref, o_ref, s_even, s_odd, ot_scr):
    seq = k_ref.shape[2]
    tq = TQ_GLOBAL
    n_tiles = q_ref.shape[1] // tq
    heads_per_kv = N_HEADS // N_KV

    def score_chunk(u, g, row0, tk):
        head = u % N_HEADS
        k = k_ref[head // heads_per_kv, head % 2, row0:row0 + tk, :]
        q = q_ref[head // 2, pl.ds(pl.multiple_of((u // N_HEADS) * tq, tq), tq), :]
        return lax.dot_general(k, q, NT, preferred_element_type=F32)

    def value_chunk(u, row0, tk):
        return vt_ref[(u % N_HEADS) // heads_per_kv, :, row0:row0 + tk]

    def emit(u, g, out):
        ot_scr[u] = out

    _softmax_units(n_tiles * N_HEADS, 1, GLOBAL_ITER_UNITS, seq, TK_SCORE, TK_VALUE, score_chunk, value_chunk,
                   (s_even, s_odd), lambda u, g: None, emit)
    for t in range(n_tiles):
        for pair in range(N_PAIRS):
            u = t * N_HEADS + 2 * pair
            both = jnp.concatenate([ot_scr[u], ot_scr[u + 1]], axis=0)
            o_ref[t * tq:(t + 1) * tq, pair * LANES:(pair + 1) * LANES] = both.T.astype(BF16)


def _gattn_call(q, k, vt, batch, seq):
    tq = TQ_GLOBAL
    rows = tq * GLOBAL_TILES_PER_STEP
    nq = seq // rows
    n_tok = batch * seq
    return pl.pallas_call(
        _gattn_kernel,
        grid=(batch, nq),
        in_specs=[
            pl.BlockSpec((N_PAIRS, rows, LANES), lambda b, i: (0, b * nq + i, 0)),
            pl.BlockSpec((N_KV, 2, seq, LANES), lambda b, i: (0, 0, b, 0)),
            pl.BlockSpec((N_KV, VT_ROWS, seq), lambda b, i: (0, 0, b)),
        ],
        out_specs=pl.BlockSpec((rows, Q_WIDTH), lambda b, i: (b * nq + i, 0)),
        out_shape=jax.ShapeDtypeStruct((n_tok, Q_WIDTH), BF16),
        scratch_shapes=[pltpu.VMEM((seq, tq), F32), pltpu.VMEM((seq, tq), F32),
                        pltpu.VMEM((GLOBAL_TILES_PER_STEP * N_HEADS, HEAD_DIM, tq), F32)],
        compiler_params=pltpu.CompilerParams(
            dimension_semantics=("arbitrary", "arbitrary"),
            vmem_limit_bytes=VMEM_LIMIT),
        name="gattn",
    )(q, k, vt)


def _t5_bucket_table():
    half = N_BUCKETS // 2
    max_exact = half // 2
    rel = np.arange(WK)[:, None] - WINDOW - np.arange(WQ)[None, :]
    n = np.abs(rel)
    assert MAX_DISTANCE // max_exact == 16 and half - max_exact == 8
    large = np.zeros_like(n)
    for kk in range(1, 2 * half):
        large += ((n * n) >= (2 ** kk) * max_exact * max_exact)
    large = np.minimum(max_exact + large, half - 1)
    bucket = np.where(rel > 0, half, 0) + np.where(n < max_exact, n, large)
    return bucket.astype(np.int32)


def _wattn_kernel(seq, q_ref, k_ref, vt_ref, bkt_ref, rb_ref, sink_ref, o_ref,
                  kpad, vtpad, bias_edge, bias_mid, s_even, s_odd, ot_scr):
    b = pl.program_id(0)
    i = pl.program_id(1)
    n_q_step = q_ref.shape[1] // WQ
    n_q_seq = seq // WQ
    n_blk_seq = seq // BLOCK
    edge = WINDOW

    assert WINDOW == BLOCK and WQ == 2 * BLOCK and TK_VALUE == 2 * BLOCK
    top_rc = (slice(0, edge), slice(0, BLOCK))
    mid_rc = (slice(edge, WK - edge), slice(0, WQ))
    bot_rc = (slice(WK - edge, WK), slice(BLOCK, WQ))

    @pl.when((b == 0) & (i == 0))
    def _():
        neg = jnp.full((edge, BLOCK), NEG_INF, F32)
        bias_edge[0, N_HEADS] = neg
        bias_edge[1, N_HEADS] = neg

        def table(rc, head):
            bucket = bkt_ref[rc]
            row = lax.broadcasted_iota(jnp.int32, bucket.shape, 0) + rc[0].start
            col = lax.broadcasted_iota(jnp.int32, bucket.shape, 1) + rc[1].start
            bias = jnp.zeros(bucket.shape, F32)
            for bk in range(N_BUCKETS):
                bias = jnp.where(bucket == bk, rb_ref[bk, head], bias)
            return jnp.where(jnp.abs(row - WINDOW - col) <= WINDOW, bias * LOG2E, NEG_INF)

        for head in range(N_HEADS):
            bias_edge[0, head] = table(top_rc, head)
            bias_mid[head] = table(mid_rc, head)
            bias_edge[1, head] = table(bot_rc, head)

    @pl.when(i == 0)
    def _():
        zk = jnp.zeros((WINDOW, LANES), BF16)
        for kv in range(N_KV):
            for half in range(2):
                kpad[kv, half, 0:WINDOW, :] = zk
                kpad[kv, half, WINDOW:WINDOW + seq, :] = k_ref[kv, half]
                kpad[kv, half, WINDOW + seq:2 * WINDOW + seq, :] = zk
        zv = jnp.zeros((N_KV, VT_ROWS, BLOCK), BF16)
        vtpad[0] = zv
        vtpad[pl.ds(1, n_blk_seq)] = vt_ref[...].reshape(n_blk_seq, N_KV, VT_ROWS, BLOCK)
        vtpad[n_blk_seq + 1] = zv

    group = N_HEADS // N_KV

    def put_scores(u, s_write):
        j, kv = u // N_KV, u % N_KV
        qb = i * n_q_step + j
        cms = []
        for g in range(group):
            head = kv * group + g
            q = q_ref[head // 2, pl.ds(pl.multiple_of(j * WQ, WQ), WQ), :]
            st = lax.dot_general(kpad[kv, g % 2, pl.ds(pl.multiple_of(qb * WQ, WQ), WK), :], q,
                                 NT, preferred_element_type=F32)
            top = st[top_rc] + bias_edge[0, jnp.where(qb == 0, N_HEADS, head)]
            mid = st[mid_rc] + bias_mid[head]
            bot = st[bot_rc] + bias_edge[1, jnp.where(qb == n_q_seq - 1, N_HEADS, head)]
            c0 = g * WQ
            s_write[top_rc[0], c0:c0 + BLOCK] = top
            s_write[mid_rc[0], c0:c0 + WQ] = mid
            s_write[bot_rc[0], c0 + BLOCK:c0 + WQ] = bot
            edge_max = jnp.concatenate([_fold8(top, jnp.max), _fold8(bot, jnp.max)], axis=1)
            cms.append(jnp.maximum(_fold8(mid, jnp.max), edge_max))
        return jnp.concatenate(cms, axis=1)

    def get_probs(s_read, row0, tk, m):
        zeros = jnp.zeros((BLOCK, BLOCK), F32)

        def half_rows(rows, lane0):
            parts = []
            for g in range(group):
                c = g * WQ + lane0
                live = jnp.exp2(s_read[rows, c:c + BLOCK] - m[:, c:c + BLOCK])
                parts += [live, zeros] if lane0 == 0 else [zeros, live]
            return jnp.concatenate(parts, axis=1)

        def full_rows(rows):
            return jnp.exp2(s_read[rows, :] - m)

        if row0 == 0:
            p = [half_rows(top_rc[0], 0), full_rows(slice(edge, tk))]
        else:
            p = [full_rows(slice(row0, WK - edge)), half_rows(bot_rc[0], BLOCK)]
        return jnp.concatenate(p, axis=0).astype(BF16)

    def value_chunk(u, row0, tk):
        j, kv = u // N_KV, u % N_KV
        blk0 = (i * n_q_step + j) * (WQ // BLOCK) + row0 // BLOCK
        return jnp.concatenate([vtpad[blk0 + t, kv] for t in range(tk // BLOCK)], axis=1)

    def emit(u, g, out):
        ot_scr[(u // N_KV) * N_HEADS + (u % N_KV) * group + g] = out

    _softmax_units(n_q_step * N_KV, group, WINDOW_ITER_UNITS, WK, WK, TK_VALUE, None, value_chunk,
                   (s_even, s_odd),
                   lambda u, g: sink_ref[(u % N_KV) * group + g] * LOG2E, emit,
                   put_scores=put_scores, get_probs=get_probs)
    for j in range(n_q_step):
        for pair in range(N_PAIRS):
            u = j * N_HEADS + 2 * pair
            both = jnp.concatenate([ot_scr[u], ot_scr[u + 1]], axis=0)
            o_ref[j * WQ:(j + 1) * WQ, pair * LANES:(pair + 1) * LANES] = both.T.astype(BF16)


def _wattn_call(q, k, vt3, batch, seq, bucket, rel_bias, sink):
    tq = TQ_WINDOW
    nq = seq // tq
    n_tok = batch * seq
    n_blk_seq = seq // BLOCK
    return pl.pallas_call(
        functools.partial(_wattn_kernel, seq),
        grid=(batch, nq),
        in_specs=[
            pl.BlockSpec((N_PAIRS, tq, LANES), lambda b, i: (0, b * nq + i, 0)),
            pl.BlockSpec((N_KV, 2, seq, LANES), lambda b, i: (0, 0, b, 0)),
            pl.BlockSpec((n_blk_seq, N_KV * VT_ROWS, BLOCK), lambda b, i: (b, 0, 0)),
            _const_spec((WK, WQ)),
            pl.BlockSpec(memory_space=pltpu.SMEM),
            pl.BlockSpec(memory_space=pltpu.SMEM),
        ],
        out_specs=pl.BlockSpec((tq, Q_WIDTH), lambda b, i: (b * nq + i, 0)),
        out_shape=jax.ShapeDtypeStruct((n_tok, Q_WIDTH), BF16),
        scratch_shapes=[
            pltpu.VMEM((N_KV, 2, seq + 2 * WINDOW, LANES), BF16),
            pltpu.VMEM((n_blk_seq + 2, N_KV, VT_ROWS, BLOCK), BF16),
            pltpu.VMEM((2, N_HEADS + 1, WINDOW, BLOCK), F32),
            pltpu.VMEM((N_HEADS, WK - 2 * WINDOW, WQ), F32),
            pltpu.VMEM((WK, N_HEADS // N_KV * WQ), F32),
            pltpu.VMEM((WK, N_HEADS // N_KV * WQ), F32),
            pltpu.VMEM((tq // WQ * N_HEADS, HEAD_DIM, WQ), F32),
        ],
        compiler_params=pltpu.CompilerParams(
            dimension_semantics=("arbitrary", "arbitrary"),
            vmem_limit_bytes=VMEM_LIMIT),
        name="wattn",
    )(q, k, vt3, bucket, rel_bias, sink)


def _mix_kernel(x_ref, ya_ref, yb_ref, gpre_ref, wg_ref, bg_ref, wa_ref, wb_ref, wo_ref, gpost_ref, o_ref):
    x = x_ref[...]
    h = _rmsnorm(x, gpre_ref[...]).astype(BF16)
    z = jnp.dot(h, wg_ref[...], preferred_element_type=F32) + bg_ref[...]
    gates = 1.0 / (1.0 + jnp.exp2(z * -LOG2E))
    a = jnp.dot(ya_ref[...], wa_ref[...], preferred_element_type=F32)
    b = jnp.dot(yb_ref[...], wb_ref[...], preferred_element_type=F32)
    mix = gates[:, :D_MODEL] * a + gates[:, D_MODEL:] * b
    o = jnp.dot(mix.astype(BF16), wo_ref[...], preferred_element_type=F32)
    o_ref[...] = x + _rmsnorm(o, gpost_ref[...])


def _mix_call(x2d, ya, yb, g_pre, w_gate, b_gate, w_a, w_b, w_out, g_post):
    n_tok = x2d.shape[0]
    tm = TM_MIX
    return pl.pallas_call(
        _mix_kernel,
        grid=(n_tok // tm,),
        in_specs=[
            pl.BlockSpec((tm, D_MODEL), lambda i: (i, 0)),
            pl.BlockSpec((tm, Q_WIDTH), lambda i: (i, 0)),
            pl.BlockSpec((tm, Q_WIDTH), lambda i: (i, 0)),
            _const_spec((1, D_MODEL)),
            _const_spec((D_MODEL, 2 * D_MODEL)),
            _const_spec((1, 2 * D_MODEL)),
            _const_spec((Q_WIDTH, D_MODEL)),
            _const_spec((Q_WIDTH, D_MODEL)),
            _const_spec((D_MODEL, D_MODEL)),
            _const_spec((1, D_MODEL)),
        ],
        out_specs=pl.BlockSpec((tm, D_MODEL), lambda i: (i, 0)),
        out_shape=jax.ShapeDtypeStruct((n_tok, D_MODEL), F32),
        compiler_params=pltpu.CompilerParams(
            dimension_semantics=("arbitrary",), vmem_limit_bytes=VMEM_LIMIT),
        name="mix",
    )(x2d, ya, yb, g_pre, w_gate, b_gate, w_a, w_b, w_out, g_post)


def _ffn_kernel(x_ref, gpre_ref, wg_ref, wu_ref, wd_ref, gpost_ref, o_ref):
    rows = x_ref.shape[0] // FFN_ROW_GROUPS
    for r in range(FFN_ROW_GROUPS):
        rs = slice(r * rows, (r + 1) * rows)
        x = x_ref[rs, :]
        h = _rmsnorm(x, gpre_ref[...]).astype(BF16)
        g = jnp.dot(h, wg_ref[...], preferred_element_type=F32)
        u = jnp.dot(h, wu_ref[...], preferred_element_type=F32)
        act = (g / (1.0 + jnp.exp2(g * -LOG2E))) * u
        f = jnp.dot(act.astype(BF16), wd_ref[...], preferred_element_type=F32)
        o_ref[rs, :] = x + _rmsnorm(f, gpost_ref[...])


def _ffn_call(x2d, g_pre, w_g, w_u, w_d, g_post):
    n_tok = x2d.shape[0]
    tm = TM_FFN
    return pl.pallas_call(
        _ffn_kernel,
        grid=(n_tok // tm,),
        in_specs=[
            pl.BlockSpec((tm, D_MODEL), lambda i: (i, 0)),
            _const_spec((1, D_MODEL)),
            _const_spec((D_MODEL, D_FF)),
            _const_spec((D_MODEL, D_FF)),
            _const_spec((D_FF, D_MODEL)),
            _const_spec((1, D_MODEL)),
        ],
        out_specs=pl.BlockSpec((tm, D_MODEL), lambda i: (i, 0)),
        out_shape=jax.ShapeDtypeStruct((n_tok, D_MODEL), F32),
        compiler_params=pltpu.CompilerParams(
            dimension_semantics=("arbitrary",), vmem_limit_bytes=VMEM_LIMIT),
        name="ffn",
    )(x2d, g_pre, w_g, w_u, w_d, g_post)


def _rope_tables(seq):
    pos = np.arange(seq)
    rows = (pos // GRID_W).astype(np.float32)
    cols = (pos % GRID_W).astype(np.float32)
    inv_freq = (np.float32(1.0) / np.power(
        np.float32(ROPE_THETA), np.arange(0, ROPE_HALF, 2, dtype=np.float32) / np.float32(ROPE_HALF)))
    lane = np.arange(LANES)
    d = lane % HEAD_DIM
    use_col = (d // ROPE_HALF) == 1
    j = d % ROPE_HALF
    f_idx = j % ROPE_QUARTER
    coord = np.where(use_col[None, :], cols[:, None], rows[:, None])
    ang = (coord * inv_freq.astype(np.float32)[f_idx][None, :]).astype(np.float32).astype(np.float64)
    cos, sin = np.cos(ang), np.sin(ang)
    first = (j < ROPE_QUARTER)[None, :]
    sin_up = np.where(first, -sin, 0.0)
    sin_dn = np.where(first, 0.0, sin)
    return jnp.asarray(np.stack([cos, sin_up, sin_dn]).astype(np.float32))


def _layer(x, bucket, bd, p):
    batch, seq, _ = x.shape
    assert seq % TM_PRE == 0 and seq % (TQ_GLOBAL * GLOBAL_TILES_PER_STEP) == 0
    assert seq % TQ_WINDOW == 0 and seq // WQ >= 2
    x2d = x.reshape(batch * seq, D_MODEL)
    qa, ka, vat, qb, kb, vbt = _pre_call(x2d, seq, p["g_mix_pre"], p["w_in"], p["qk_gain"],
                                         _rope_tables(seq), bd)
    ya = _gattn_call(qa, ka, vat, batch, seq)
    yb = _wattn_call(qb, kb, vbt, batch, seq, bucket, p["rel_bias"], p["sink"])
    x1 = _mix_call(x2d, ya, yb, p["g_mix_pre"], p["w_gate"], p["b_gate"], p["w_a"], p["w_b"], p["w_out"],
                   p["g_mix_post"])
    y = _ffn_call(x1, p["g_ffn_pre"], p["w_ffn_gate"], p["w_ffn_up"], p["w_ffn_down"], p["g_ffn_post"])
    return y.reshape(batch, seq, D_MODEL)


def kernel(x_prompt, x_sample, norm_mix_pre, norm_mix_post, w_in, q_norm_a, k_norm_a, sink_b, rel_bias,
           w_branch_a, w_branch_b, w_gate, b_gate, w_out, norm_ffn_pre, norm_ffn_post, w_ffn_gate,
           w_ffn_up, w_ffn_down):
    depth = w_in.shape[0]
    bucket = jnp.asarray(_t5_bucket_table())
    blk = np.arange(LANES) // HEAD_DIM
    bd = jnp.asarray(np.tile(blk[:, None] == blk[None, :], (2, 1)), dtype=BF16)
    y_prompt, y_sample = x_prompt, x_sample
    for l in range(depth):
        p = {
            "g_mix_pre": norm_mix_pre[l].reshape(1, D_MODEL),
            "g_mix_post": norm_mix_post[l].reshape(1, D_MODEL),
            "w_in": w_in[l].astype(BF16),
            "qk_gain": jnp.stack([jnp.tile(q_norm_a[l], 2), jnp.tile(k_norm_a[l], 2)]).reshape(2, 1, LANES),
            "sink": sink_b[l],
            "rel_bias": rel_bias,
            "w_a": w_branch_a[l].astype(BF16),
            "w_b": w_branch_b[l].astype(BF16),
            "w_gate": w_gate[l].astype(BF16),
            "b_gate": b_gate[l].reshape(1, 2 * D_MODEL),
            "w_out": w_out[l].astype(BF16),
            "g_ffn_pre": norm_ffn_pre[l].reshape(1, D_MODEL),
            "g_ffn_post": norm_ffn_post[l].reshape(1, D_MODEL),
            "w_ffn_gate": w_ffn_gate[l].astype(BF16),
            "w_ffn_up": w_ffn_up[l].astype(BF16),
            "w_ffn_down": w_ffn_down[l].astype(BF16),
        }
        y_prompt = _layer(y_prompt, bucket, bd, p)
        y_sample = _layer(y_sample, bucket, bd, p)
    return (y_prompt, y_sample)
```

```python
import functools
import math

import numpy as np
import jax
import jax.numpy as jnp
from jax import lax
from jax.experimental import pallas as pl
from jax.experimental.pallas import tpu as pltpu

F32 = jnp.float32
BF16 = jnp.bfloat16

D_MODEL = 1024
HEAD_DIM = 64
N_HEADS = 8
N_KV = 2
Q_WIDTH = N_HEADS * HEAD_DIM
KV_WIDTH = N_KV * HEAD_DIM
IN_WIDTH = 2 * (Q_WIDTH + 2 * KV_WIDTH)
BLOCK = 128
WINDOW = 128
GRID_W = 64
ROPE_THETA = 10000.0
ROPE_HALF = HEAD_DIM // 2
ROPE_QUARTER = ROPE_HALF // 2
N_BUCKETS = 32
MAX_DISTANCE = 128
D_FF = 2816
EPS = 1e-6
NEG_INF = -1e30
LOG2E = 1.4426950408889634
Q_SCALE = LOG2E / math.sqrt(HEAD_DIM)

LANES = 128
SUBLANES = 8
N_PAIRS = N_HEADS // 2
PAIRS_PER_KV = N_PAIRS // N_KV


TM_PRE = 512
TM_MIX = 1024
TM_FFN = 1024
FFN_ROW_GROUPS = 4
TQ_GLOBAL = 512
TK_SCORE = 512
TK_VALUE = 256
BF16_SUBLANES = 16
VT_ROWS = HEAD_DIM + BF16_SUBLANES
WQ = 256
WK = WQ + 2 * WINDOW
TQ_WINDOW = 2048
VMEM_LIMIT = 56 * 1024 * 1024
GLOBAL_ITER_UNITS = 4
WINDOW_ITER_UNITS = 4
GLOBAL_TILES_PER_STEP = 4

NT = (((1,), (1,)), ((), ()))


def _rmsnorm(x, gain):
    var = jnp.mean(x * x, axis=-1, keepdims=True)
    return x * lax.rsqrt(var + EPS) * gain


def _const_spec(shape):
    zeros = (0,) * len(shape)
    return pl.BlockSpec(shape, lambda *_: zeros, pipeline_mode=pl.Buffered(1))


def _two_stage(n_tiles, stage_a, stage_b, buf_even, buf_odd):
    i = pl.program_id(0)
    steady = (i > 0) & (i < n_tiles)

    @pl.when(i == 0)
    def _():
        stage_a(buf_even)

    @pl.when(steady & (i % 2 == 1))
    def _():
        stage_b(buf_even)
        stage_a(buf_odd)

    @pl.when(steady & (i % 2 == 0))
    def _():
        stage_b(buf_odd)
        stage_a(buf_even)

    @pl.when(i == n_tiles)
    def _():
        stage_b(buf_even if (n_tiles - 1) % 2 == 0 else buf_odd)


def _cur_tile(n_tiles):
    return lambda i: (jnp.minimum(i, n_tiles - 1), 0)


def _done_tile(i):
    return jnp.maximum(i - 1, 0)


def _pre_kernel(n_tiles, x_ref, g_ref, w_ref, qkg_ref, rope_ref, bd_ref,
                qa_ref, ka_ref, vat_ref, qb_ref, kb_ref, vbt_ref, proj_even, proj_odd):
    tm = x_ref.shape[0]

    def project(proj_ref):
        h = _rmsnorm(x_ref[...], g_ref[...]).astype(BF16)
        proj_ref[...] = jnp.dot(h, w_ref[...], preferred_element_type=F32)

    def finish(proj_ref):
        bd = bd_ref[...]
        cos, sin_up, sin_dn = rope_ref[0], rope_ref[1], rope_ref[2]
        lane = lax.broadcasted_iota(jnp.int32, (tm, LANES), 1)
        lo_half = lane < HEAD_DIM

        def slab(col):
            return proj_ref[:, col * LANES:(col + 1) * LANES]

        def qk_norm_rope(z, gain):
            sq = z * z
            hi = sq.astype(BF16)
            lo = (sq - hi.astype(F32)).astype(BF16)
            mean_sq = jnp.dot(jnp.concatenate([hi, lo], axis=1), bd, preferred_element_type=F32)
            zn = z * lax.rsqrt(mean_sq + EPS) * gain
            return (zn * cos
                    + pltpu.roll(zn, LANES - ROPE_QUARTER, 1) * sin_up
                    + pltpu.roll(zn, ROPE_QUARTER, 1) * sin_dn)

        def put_k(ref, z):
            swapped = pltpu.roll(z, HEAD_DIM, 1)
            ref[0, 0] = jnp.where(lo_half, z, 0.0).astype(BF16)
            ref[0, 1] = jnp.where(lo_half, 0.0, swapped).astype(BF16)
            ref[1, 0] = jnp.where(lo_half, swapped, 0.0).astype(BF16)
            ref[1, 1] = jnp.where(lo_half, 0.0, z).astype(BF16)

        def vt_with_ones(v):
            vt = v.T.astype(BF16)
            ones = jnp.ones((VT_ROWS - HEAD_DIM, tm), BF16)
            return jnp.concatenate([vt[:HEAD_DIM], ones, vt[HEAD_DIM:], ones], axis=0)

        q_gain = qkg_ref[0] * Q_SCALE
        for s in range(N_PAIRS):
            qa_ref[s] = qk_norm_rope(slab(s), q_gain).astype(BF16)
        put_k(ka_ref, qk_norm_rope(slab(N_PAIRS), qkg_ref[1]))
        vat_ref[...] = vt_with_ones(slab(N_PAIRS + 1)).reshape(N_KV, VT_ROWS, tm)
        base = N_PAIRS + 2
        for s in range(N_PAIRS):
            qb_ref[s] = (slab(base + s) * Q_SCALE).astype(BF16)
        put_k(kb_ref, slab(base + N_PAIRS))
        vbt = vt_with_ones(slab(base + N_PAIRS + 1))
        for j in range(tm // BLOCK):
            vbt_ref[j] = vbt[:, j * BLOCK:(j + 1) * BLOCK]

    _two_stage(n_tiles, project, finish, proj_even, proj_odd)


def _pre_call(x2d, seq, g_pre, w_in, qk_gain, rope, bd):
    n_tok = x2d.shape[0]
    tm = TM_PRE
    n_tiles = n_tok // tm
    tiles_per_seq = seq // tm

    done = _done_tile
    q_spec = pl.BlockSpec((N_PAIRS, tm, LANES), lambda i: (0, done(i), 0))
    k_spec = pl.BlockSpec((N_KV, 2, tm, LANES), lambda i: (0, 0, done(i), 0))
    q_shape = jax.ShapeDtypeStruct((N_PAIRS, n_tok, LANES), BF16)
    k_shape = jax.ShapeDtypeStruct((N_KV, 2, n_tok, LANES), BF16)
    return pl.pallas_call(
        functools.partial(_pre_kernel, n_tiles),
        grid=(n_tiles + 1,),
        in_specs=[
            pl.BlockSpec((tm, D_MODEL), _cur_tile(n_tiles)),
            _const_spec((1, D_MODEL)),
            _const_spec((D_MODEL, IN_WIDTH)),
            _const_spec((2, 1, LANES)),
            pl.BlockSpec((3, tm, LANES), lambda i: (0, done(i) % tiles_per_seq, 0)),
            _const_spec((2 * LANES, LANES)),
        ],
        out_specs=[
            q_spec, k_spec,
            pl.BlockSpec((N_KV, VT_ROWS, tm), lambda i: (0, 0, done(i))),
            q_spec, k_spec,
            pl.BlockSpec((tm // BLOCK, N_KV * VT_ROWS, BLOCK), lambda i: (done(i), 0, 0)),
        ],
        out_shape=[
            q_shape, k_shape,
            jax.ShapeDtypeStruct((N_KV, VT_ROWS, n_tok), BF16),
            q_shape, k_shape,
            jax.ShapeDtypeStruct((n_tok // BLOCK, N_KV * VT_ROWS, BLOCK), BF16),
        ],
        scratch_shapes=[pltpu.VMEM((tm, IN_WIDTH), F32), pltpu.VMEM((tm, IN_WIDTH), F32)],
        compiler_params=pltpu.CompilerParams(
            dimension_semantics=("arbitrary",), vmem_limit_bytes=VMEM_LIMIT),
        name="pre",
    )(x2d, g_pre, w_in, qk_gain, rope, bd)


def _fold8(x, op):
    return op(x.reshape(x.shape[0] // SUBLANES, SUBLANES, x.shape[1]), axis=0)


def _softmax_units(n_units, group, iter_units, n_keys, tk_a, tk_b, score_chunk, value_chunk, s_scr, sink_of,
                   emit, put_scores=None, get_probs=None):
    assert tk_a % tk_b == 0 and n_keys % tk_a == 0
    n_a, b_per_a = n_keys // tk_a, tk_a // tk_b

    def sinks(u):
        vals = [sink_of(u, g) for g in range(group)]
        return None if vals[0] is None else vals

    def sweeps(ua, s_write, ub, m_b, s_read):
        mx = None
        acc = [None] * group
        for ca in range(n_a):
            if ua is not None:
                if put_scores is None:
                    st = jnp.concatenate([score_chunk(ua, g, ca * tk_a, tk_a) for g in range(group)], axis=1)
                    s_write[ca * tk_a:(ca + 1) * tk_a, :] = st
                    cm = _fold8(st, jnp.max)
                else:
                    assert n_a == 1
                    cm = put_scores(ua, s_write)
                mx = cm if mx is None else jnp.maximum(mx, cm)
            if ub is not None:
                for cb in range(ca * b_per_a, (ca + 1) * b_per_a):
                    if get_probs is None:
                        p = jnp.exp2(s_read[cb * tk_b:(cb + 1) * tk_b, :] - m_b).astype(BF16)
                    else:
                        p = get_probs(s_read, cb * tk_b, tk_b, m_b)
                    vt = value_chunk(ub, cb * tk_b, tk_b)
                    tq = p.shape[1] // group
                    for g in range(group):
                        pv = jnp.dot(vt, p[:, g * tq:(g + 1) * tq], preferred_element_type=F32)
                        acc[g] = pv if acc[g] is None else acc[g] + pv
        if ub is not None:
            sink = sinks(ub)
            for g in range(group):
                denom = acc[g][HEAD_DIM:HEAD_DIM + 1]
                if sink is not None:
                    tq = denom.shape[1]
                    denom = denom + jnp.exp2(sink[g] - m_b[:, g * tq:(g + 1) * tq])
                emit(ub, g, acc[g][:HEAD_DIM] / denom)
        if ua is None:
            return None
        m = jnp.max(mx, axis=0, keepdims=True)
        sink = sinks(ua)
        if sink is not None:
            tq = m.shape[1] // group
            m = jnp.concatenate([jnp.maximum(m[:, g * tq:(g + 1) * tq], sink[g]) for g in range(group)],
                                axis=1)
        return m

    def sub_steps(k0, count, m):
        for d in range(count):
            m = sweeps(k0 + d + 1, s_scr[(d + 1) % 2], k0 + d, m, s_scr[d % 2])
        return m

    assert iter_units % 2 == 0
    n_sub = n_units - 1
    trips = n_sub // iter_units + jnp.minimum(pl.program_id(0), 0)
    m = lax.fori_loop(0, trips, lambda t, m: sub_steps(iter_units * t, iter_units, m),
                      sweeps(0, s_scr[0], None, None, None))
    done = n_sub // iter_units * iter_units
    m = sub_steps(done, n_sub - done, m)
    sweeps(None, None, n_units - 1, m, s_scr[(n_units - 1) % 2])


def _gattn_kernel(q_ref, k_ref, vt_ref, o_ref, s_even, s_odd, ot_scr):
    seq = k_ref.shape[2]
    tq = TQ_GLOBAL
    n_tiles = q_ref.shape[1] // tq
    heads_per_kv = N_HEADS // N_KV

    def score_chunk(u, g, row0, tk):
        head = u % N_HEADS
        k = k_ref[head // heads_per_kv, head % 2, row0:row0 + tk, :]
        q = q_ref[head // 2, pl.ds(pl.multiple_of((u // N_HEADS) * tq, tq), tq), :]
        return lax.dot_general(k, q, NT, preferred_element_type=F32)

    def value_chunk(u, row0, tk):
        return vt_ref[(u % N_HEADS) // heads_per_kv, :, row0:row0 + tk]

    def emit(u, g, out):
        ot_scr[u] = out

    _softmax_units(n_tiles * N_HEADS, 1, GLOBAL_ITER_UNITS, seq, TK_SCORE, TK_VALUE, score_chunk, value_chunk,
                   (s_even, s_odd), lambda u, g: None, emit)
    for t in range(n_tiles):
        for pair in range(N_PAIRS):
            u = t * N_HEADS + 2 * pair
            both = jnp.concatenate([ot_scr[u], ot_scr[u + 1]], axis=0)
            o_ref[t * tq:(t + 1) * tq, pair * LANES:(pair + 1) * LANES] = both.T.astype(BF16)


def _gattn_call(q, k, vt, batch, seq):
    tq = TQ_GLOBAL
    rows = tq * GLOBAL_TILES_PER_STEP
    nq = seq // rows
    n_tok = batch * seq
    return pl.pallas_call(
        _gattn_kernel,
        grid=(batch, nq),
        in_specs=[
            pl.BlockSpec((N_PAIRS, rows, LANES), lambda b, i: (0, b * nq + i, 0)),
            pl.BlockSpec((N_KV, 2, seq, LANES), lambda b, i: (0, 0, b, 0)),
            pl.BlockSpec((N_KV, VT_ROWS, seq), lambda b, i: (0, 0, b)),
        ],
        out_specs=pl.BlockSpec((rows, Q_WIDTH), lambda b, i: (b * nq + i, 0)),
        out_shape=jax.ShapeDtypeStruct((n_tok, Q_WIDTH), BF16),
        scratch_shapes=[pltpu.VMEM((seq, tq), F32), pltpu.VMEM((seq, tq), F32),
                        pltpu.VMEM((GLOBAL_TILES_PER_STEP * N_HEADS, HEAD_DIM, tq), F32)],
        compiler_params=pltpu.CompilerParams(
            dimension_semantics=("arbitrary", "arbitrary"),
            vmem_limit_bytes=VMEM_LIMIT),
        name="gattn",
    )(q, k, vt)


def _t5_bucket_table():
    half = N_BUCKETS // 2
    max_exact = half // 2
    rel = np.arange(WK)[:, None] - WINDOW - np.arange(WQ)[None, :]
    n = np.abs(rel)
    assert MAX_DISTANCE // max_exact == 16 and half - max_exact == 8
    large = np.zeros_like(n)
    for kk in range(1, 2 * half):
        large += ((n * n) >= (2 ** kk) * max_exact * max_exact)
    large = np.minimum(max_exact + large, half - 1)
    bucket = np.where(rel > 0, half, 0) + np.where(n < max_exact, n, large)
    return bucket.astype(np.int32)


def _wattn_kernel(seq, q_ref, k_ref, vt_ref, bkt_ref, rb_ref, sink_ref, o_ref,
                  kpad, vtpad, bias_edge, bias_mid, s_even, s_odd, ot_scr):
    b = pl.program_id(0)
    i = pl.program_id(1)
    n_q_step = q_ref.shape[1] // WQ
    n_q_seq = seq // WQ
    n_blk_seq = seq // BLOCK
    edge = WINDOW

    assert WINDOW == BLOCK and WQ == 2 * BLOCK and TK_VALUE == 2 * BLOCK
    top_rc = (slice(0, edge), slice(0, BLOCK))
    mid_rc = (slice(edge, WK - edge), slice(0, WQ))
    bot_rc = (slice(WK - edge, WK), slice(BLOCK, WQ))

    @pl.when((b == 0) & (i == 0))
    def _():
        neg = jnp.full((edge, BLOCK), NEG_INF, F32)
        bias_edge[0, N_HEADS] = neg
        bias_edge[1, N_HEADS] = neg

        def table(rc, head):
            bucket = bkt_ref[rc]
            row = lax.broadcasted_iota(jnp.int32, bucket.shape, 0) + rc[0].start
            col = lax.broadcasted_iota(jnp.int32, bucket.shape, 1) + rc[1].start
            bias = jnp.zeros(bucket.shape, F32)
            for bk in range(N_BUCKETS):
                bias = jnp.where(bucket == bk, rb_ref[bk, head], bias)
            return jnp.where(jnp.abs(row - WINDOW - col) <= WINDOW, bias * LOG2E, NEG_INF)

        for head in range(N_HEADS):
            bias_edge[0, head] = table(top_rc, head)
            bias_mid[head] = table(mid_rc, head)
            bias_edge[1, head] = table(bot_rc, head)

    @pl.when(i == 0)
    def _():
        zk = jnp.zeros((WINDOW, LANES), BF16)
        for kv in range(N_KV):
            for half in range(2):
                kpad[kv, half, 0:WINDOW, :] = zk
                kpad[kv, half, WINDOW:WINDOW + seq, :] = k_ref[kv, half]
                kpad[kv, half, WINDOW + seq:2 * WINDOW + seq, :] = zk
        zv = jnp.zeros((N_KV, VT_ROWS, BLOCK), BF16)
        vtpad[0] = zv
        vtpad[pl.ds(1, n_blk_seq)] = vt_ref[...].reshape(n_blk_seq, N_KV, VT_ROWS, BLOCK)
        vtpad[n_blk_seq + 1] = zv

    group = N_HEADS // N_KV

    def put_scores(u, s_write):
        j, kv = u // N_KV, u % N_KV
        qb = i * n_q_step + j
        cms = []
        for g in range(group):
            head = kv * group + g
            q = q_ref[head // 2, pl.ds(pl.multiple_of(j * WQ, WQ), WQ), :]
            st = lax.dot_general(kpad[kv, g % 2, pl.ds(pl.multiple_of(qb * WQ, WQ), WK), :], q,
                                 NT, preferred_element_type=F32)
            top = st[top_rc] + bias_edge[0, jnp.where(qb == 0, N_HEADS, head)]
            mid = st[mid_rc] + bias_mid[head]
            bot = st[bot_rc] + bias_edge[1, jnp.where(qb == n_q_seq - 1, N_HEADS, head)]
            c0 = g * WQ
            s_write[top_rc[0], c0:c0 + BLOCK] = top
            s_write[mid_rc[0], c0:c0 + WQ] = mid
            s_write[bot_rc[0], c0 + BLOCK:c0 + WQ] = bot
            edge_max = jnp.concatenate([_fold8(top, jnp.max), _fold8(bot, jnp.max)], axis=1)
            cms.append(jnp.maximum(_fold8(mid, jnp.max), edge_max))
        return jnp.concatenate(cms, axis=1)

    def get_probs(s_read, row0, tk, m):
        zeros = jnp.zeros((BLOCK, BLOCK), F32)

        def half_rows(rows, lane0):
            parts = []
            for g in range(group):
                c = g * WQ + lane0
                live = jnp.exp2(s_read[rows, c:c + BLOCK] - m[:, c:c + BLOCK])
                parts += [live, zeros] if lane0 == 0 else [zeros, live]
            return jnp.concatenate(parts, axis=1)

        def full_rows(rows):
            return jnp.exp2(s_read[rows, :] - m)

        if row0 == 0:
            p = [half_rows(top_rc[0], 0), full_rows(slice(edge, tk))]
        else:
            p = [full_rows(slice(row0, WK - edge)), half_rows(bot_rc[0], BLOCK)]
        return jnp.concatenate(p, axis=0).astype(BF16)

    def value_chunk(u, row0, tk):
        j, kv = u // N_KV, u % N_KV
        blk0 = (i * n_q_step + j) * (WQ // BLOCK) + row0 // BLOCK
        return jnp.concatenate([vtpad[blk0 + t, kv] for t in range(tk // BLOCK)], axis=1)

    def emit(u, g, out):
        ot_scr[(u // N_KV) * N_HEADS + (u % N_KV) * group + g] = out

    _softmax_units(n_q_step * N_KV, group, WINDOW_ITER_UNITS, WK, WK, TK_VALUE, None, value_chunk,
                   (s_even, s_odd),
                   lambda u, g: sink_ref[(u % N_KV) * group + g] * LOG2E, emit,
                   put_scores=put_scores, get_probs=get_probs)
    for j in range(n_q_step):
        for pair in range(N_PAIRS):
            u = j * N_HEADS + 2 * pair
            both = jnp.concatenate([ot_scr[u], ot_scr[u + 1]], axis=0)
            o_ref[j * WQ:(j + 1) * WQ, pair * LANES:(pair + 1) * LANES] = both.T.astype(BF16)


def _wattn_call(q, k, vt3, batch, seq, bucket, rel_bias, sink):
    tq = TQ_WINDOW
    nq = seq // tq
    n_tok = batch * seq
    n_blk_seq = seq // BLOCK
    return pl.pallas_call(
        functools.partial(_wattn_kernel, seq),
        grid=(batch, nq),
        in_specs=[
            pl.BlockSpec((N_PAIRS, tq, LANES), lambda b, i: (0, b * nq + i, 0)),
            pl.BlockSpec((N_KV, 2, seq, LANES), lambda b, i: (0, 0, b, 0)),
            pl.BlockSpec((n_blk_seq, N_KV * VT_ROWS, BLOCK), lambda b, i: (b, 0, 0)),
            _const_spec((WK, WQ)),
            pl.BlockSpec(memory_space=pltpu.SMEM),
            pl.BlockSpec(memory_space=pltpu.SMEM),
        ],
        out_specs=pl.BlockSpec((tq, Q_WIDTH), lambda b, i: (b * nq + i, 0)),
        out_shape=jax.ShapeDtypeStruct((n_tok, Q_WIDTH), BF16),
        scratch_shapes=[
            pltpu.VMEM((N_KV, 2, seq + 2 * WINDOW, LANES), BF16),
            pltpu.VMEM((n_blk_seq + 2, N_KV, VT_ROWS, BLOCK), BF16),
            pltpu.VMEM((2, N_HEADS + 1, WINDOW, BLOCK), F32),
            pltpu.VMEM((N_HEADS, WK - 2 * WINDOW, WQ), F32),
            pltpu.VMEM((WK, N_HEADS // N_KV * WQ), F32),
            pltpu.VMEM((WK, N_HEADS // N_KV * WQ), F32),
            pltpu.VMEM((tq // WQ * N_HEADS, HEAD_DIM, WQ), F32),
        ],
        compiler_params=pltpu.CompilerParams(
            dimension_semantics=("arbitrary", "arbitrary"),
            vmem_limit_bytes=VMEM_LIMIT),
        name="wattn",
    )(q, k, vt3, bucket, rel_bias, sink)


def _mix_kernel(x_ref, ya_ref, yb_ref, gpre_ref, wg_ref, bg_ref, wa_ref, wb_ref, wo_ref, gpost_ref, o_ref):
    x = x_ref[...]
    h = _rmsnorm(x, gpre_ref[...]).astype(BF16)
    z = jnp.dot(h, wg_ref[...], preferred_element_type=F32) + bg_ref[...]
    gates = 1.0 / (1.0 + jnp.exp2(z * -LOG2E))
    a = jnp.dot(ya_ref[...], wa_ref[...], preferred_element_type=F32)
    b = jnp.dot(yb_ref[...], wb_ref[...], preferred_element_type=F32)
    mix = gates[:, :D_MODEL] * a + gates[:, D_MODEL:] * b
    o = jnp.dot(mix.astype(BF16), wo_ref[...], preferred_element_type=F32)
    o_ref[...] = x + _rmsnorm(o, gpost_ref[...])


def _mix_call(x2d, ya, yb, g_pre, w_gate, b_gate, w_a, w_b, w_out, g_post):
    n_tok = x2d.shape[0]
    tm = TM_MIX
    return pl.pallas_call(
        _mix_kernel,
        grid=(n_tok // tm,),
        in_specs=[
            pl.BlockSpec((tm, D_MODEL), lambda i: (i, 0)),
            pl.BlockSpec((tm, Q_WIDTH), lambda i: (i, 0)),
            pl.BlockSpec((tm, Q_WIDTH), lambda i: (i, 0)),
            _const_spec((1, D_MODEL)),
            _const_spec((D_MODEL, 2 * D_MODEL)),
            _const_spec((1, 2 * D_MODEL)),
            _const_spec((Q_WIDTH, D_MODEL)),
            _const_spec((Q_WIDTH, D_MODEL)),
            _const_spec((D_MODEL, D_MODEL)),
            _const_spec((1, D_MODEL)),
        ],
        out_specs=pl.BlockSpec((tm, D_MODEL), lambda i: (i, 0)),
        out_shape=jax.ShapeDtypeStruct((n_tok, D_MODEL), F32),
        compiler_params=pltpu.CompilerParams(
            dimension_semantics=("arbitrary",), vmem_limit_bytes=VMEM_LIMIT),
        name="mix",
    )(x2d, ya, yb, g_pre, w_gate, b_gate, w_a, w_b, w_out, g_post)


def _ffn_kernel(x_ref, gpre_ref, wg_ref, wu_ref, wd_ref, gpost_ref, o_ref):
    rows = x_ref.shape[0] // FFN_ROW_GROUPS
    for r in range(FFN_ROW_GROUPS):
        rs = slice(r * rows, (r + 1) * rows)
        x = x_ref[rs, :]
        h = _rmsnorm(x, gpre_ref[...]).astype(BF16)
        g = jnp.dot(h, wg_ref[...], preferred_element_type=F32)
        u = jnp.dot(h, wu_ref[...], preferred_element_type=F32)
        act = (g / (1.0 + jnp.exp2(g * -LOG2E))) * u
        f = jnp.dot(act.astype(BF16), wd_ref[...], preferred_element_type=F32)
        o_ref[rs, :] = x + _rmsnorm(f, gpost_ref[...])


def _ffn_call(x2d, g_pre, w_g, w_u, w_d, g_post):
    n_tok = x2d.shape[0]
    tm = TM_FFN
    return pl.pallas_call(
        _ffn_kernel,
        grid=(n_tok // tm,),
        in_specs=[
            pl.BlockSpec((tm, D_MODEL), lambda i: (i, 0)),
            _const_spec((1, D_MODEL)),
            _const_spec((D_MODEL, D_FF)),
            _const_spec((D_MODEL, D_FF)),
            _const_spec((D_FF, D_MODEL)),
            _const_spec((1, D_MODEL)),
        ],
        out_specs=pl.BlockSpec((tm, D_MODEL), lambda i: (i, 0)),
        out_shape=jax.ShapeDtypeStruct((n_tok, D_MODEL), F32),
        compiler_params=pltpu.CompilerParams(
            dimension_semantics=("arbitrary",), vmem_limit_bytes=VMEM_LIMIT),
        name="ffn",
    )(x2d, g_pre, w_g, w_u, w_d, g_post)


def _rope_tables(seq):
    pos = np.arange(seq)
    rows = (pos // GRID_W).astype(np.float32)
    cols = (pos % GRID_W).astype(np.float32)
    inv_freq = (np.float32(1.0) / np.power(
        np.float32(ROPE_THETA), np.arange(0, ROPE_HALF, 2, dtype=np.float32) / np.float32(ROPE_HALF)))
    lane = np.arange(LANES)
    d = lane % HEAD_DIM
    use_col = (d // ROPE_HALF) == 1
    j = d % ROPE_HALF
    f_idx = j % ROPE_QUARTER
    coord = np.where(use_col[None, :], cols[:, None], rows[:, None])
    ang = (coord * inv_freq.astype(np.float32)[f_idx][None, :]).astype(np.float32).astype(np.float64)
    cos, sin = np.cos(ang), np.sin(ang)
    first = (j < ROPE_QUARTER)[None, :]
    sin_up = np.where(first, -sin, 0.0)
    sin_dn = np.where(first, 0.0, sin)
    return jnp.asarray(np.stack([cos, sin_up, sin_dn]).astype(np.float32))


def _layer(x, bucket, bd, p):
    batch, seq, _ = x.shape
    assert seq % TM_PRE == 0 and seq % (TQ_GLOBAL * GLOBAL_TILES_PER_STEP) == 0
    assert seq % TQ_WINDOW == 0 and seq // WQ >= 2
    x2d = x.reshape(batch * seq, D_MODEL)
    qa, ka, vat, qb, kb, vbt = _pre_call(x2d, seq, p["g_mix_pre"], p["w_in"], p["qk_gain"],
                                         _rope_tables(seq), bd)
    ya = _gattn_call(qa, ka, vat, batch, seq)
    yb = _wattn_call(qb, kb, vbt, batch, seq, bucket, p["rel_bias"], p["sink"])
    x1 = _mix_call(x2d, ya, yb, p["g_mix_pre"], p["w_gate"], p["b_gate"], p["w_a"], p["w_b"], p["w_out"],
                   p["g_mix_post"])
    y = _ffn_call(x1, p["g_ffn_pre"], p["w_ffn_gate"], p["w_ffn_up"], p["w_ffn_down"], p["g_ffn_post"])
    return y.reshape(batch, seq, D_MODEL)


def kernel(x_prompt, x_sample, norm_mix_pre, norm_mix_post, w_in, q_norm_a, k_norm_a, sink_b, rel_bias,
           w_branch_a, w_branch_b, w_gate, b_gate, w_out, norm_ffn_pre, norm_ffn_post, w_ffn_gate,
           w_ffn_up, w_ffn_down):
    depth = w_in.shape[0]
    bucket = jnp.asarray(_t5_bucket_table())
    blk = np.arange(LANES) // HEAD_DIM
    bd = jnp.asarray(np.tile(blk[:, None] == blk[None, :], (2, 1)) / HEAD_DIM, dtype=BF16)
    y_prompt, y_sample = x_prompt, x_sample
    for l in range(depth):
        p = {
            "g_mix_pre": norm_mix_pre[l].reshape(1, D_MODEL),
            "g_mix_post": norm_mix_post[l].reshape(1, D_MODEL),
            "w_in": w_in[l].astype(BF16),
            "qk_gain": jnp.stack([jnp.tile(q_norm_a[l], 2), jnp.tile(k_norm_a[l], 2)]).reshape(2, 1, LANES),
            "sink": sink_b[l],
            "rel_bias": rel_bias,
            "w_a": w_branch_a[l].astype(BF16),
            "w_b": w_branch_b[l].astype(BF16),
            "w_gate": w_gate[l].astype(BF16),
            "b_gate": b_gate[l].reshape(1, 2 * D_MODEL),
            "w_out": w_out[l].astype(BF16),
            "g_ffn_pre": norm_ffn_pre[l].reshape(1, D_MODEL),
            "g_ffn_post": norm_ffn_post[l].reshape(1, D_MODEL),
            "w_ffn_gate": w_ffn_gate[l].astype(BF16),
            "w_ffn_up": w_ffn_up[l].astype(BF16),
            "w_ffn_down": w_ffn_down[l].astype(BF16),
        }
        y_prompt = _layer(y_prompt, bucket, bd, p)
        y_sample = _layer(y_sample, bucket, bd, p)
    return (y_prompt, y_sample)
```
